```python
import math
import jax, jax.numpy as jnp
from jax import lax
import numpy as np

D_MODEL = 1024
BATCH = 4
SEQ = 8192
DEPTH = 2

HEAD_DIM = 64
N_ATTN_HEADS = D_MODEL // HEAD_DIM
N_DIFF_HEADS = N_ATTN_HEADS // 4
DIL_GROUPS = ((128, 1), (512, 4), (2048, 16))
HEADS_PER_DIL = (N_ATTN_HEADS - N_DIFF_HEADS) // len(DIL_GROUPS)
N_DIL_HEADS = HEADS_PER_DIL * len(DIL_GROUPS)
DIL_WIDTH = N_DIL_HEADS * HEAD_DIM
DIFF_QK_DIM = HEAD_DIM // 2
DIFF_QK_WIDTH = N_DIFF_HEADS * 2 * DIFF_QK_DIM
DIFF_V_WIDTH = N_DIFF_HEADS * HEAD_DIM
ATTN_IN_WIDTH = 3 * DIL_WIDTH + 2 * DIFF_QK_WIDTH + DIFF_V_WIDTH
ATTN_SPLITS = [DIL_WIDTH, 2 * DIL_WIDTH, 3 * DIL_WIDTH,
               3 * DIL_WIDTH + DIFF_QK_WIDTH, 3 * DIL_WIDTH + 2 * DIFF_QK_WIDTH]
BLOCK = 128
HGRN_HEADS = 8
HGRN_EXPAND = 128
HGRN_FORGET_DIM = HGRN_HEADS * HGRN_EXPAND
HGRN_VDIM = D_MODEL // HGRN_HEADS
HGRN_CHUNK = 64
HGRN_IN_WIDTH = 2 * HGRN_FORGET_DIM + 2 * D_MODEL
D_FF = 2816
N_EXPERTS = 8
TOP_K = 2
D_FF_EXPERT = 3584
N_EVEN = (DEPTH + 1) // 2
N_ODD = DEPTH // 2
EPS = 1e-6
NEG_INF = -1e30
F32 = jnp.float32

kernel_name = "hybrid_dilated_diffattn_hgrn2_moe"


def rms_norm(x, gain):
    xf = x.astype(F32)
    y = xf * lax.rsqrt(jnp.mean(xf * xf, axis=-1, keepdims=True) + EPS)
    return (y * gain.astype(F32)).astype(x.dtype)


def alibi_slopes(n):
    return 2.0 ** (-8.0 * (jnp.arange(n, dtype=F32) + 1.0) / n)


def dilated_window_attention(q, k, v, slopes, window, dilation):
    B, S, H, Dh = q.shape
    span = window // dilation
    assert span <= BLOCK
    n_blk = -(-S // (dilation * BLOCK))
    Lp = n_blk * dilation * BLOCK

    def to_classes(t):
        t = jnp.pad(t, ((0, 0), (0, Lp - S), (0, 0), (0, 0)))
        t = t.reshape(B, Lp // dilation, dilation, H, Dh).transpose(0, 2, 1, 3, 4)
        return t.reshape(B, dilation, n_blk, BLOCK, H, Dh)

    def with_prev(t):
        prev = jnp.pad(t[:, :, :-1], ((0, 0), (0, 0), (1, 0), (0, 0), (0, 0), (0, 0)))
        return jnp.concatenate([prev, t], axis=3)

    qc = to_classes(q)
    kb = with_prev(to_classes(k))
    vb = with_prev(to_classes(v))
    step = jnp.arange(BLOCK)[:, None] + BLOCK - jnp.arange(2 * BLOCK)[None, :]
    not_before_start = (jnp.arange(n_blk)[:, None, None] > 0) | (jnp.arange(2 * BLOCK)[None, None, :] >= BLOCK)
    valid = (step >= 0) & (step <= span) & not_before_start
    bias = -slopes[:, None, None] * (step * dilation).astype(F32)
    s = jnp.einsum('brnqhd,brnkhd->brnhqk', qc, kb).astype(F32) * (Dh ** -0.5) + bias
    s = jnp.where(valid[None, None, :, None], s, NEG_INF)
    lse = jax.nn.logsumexp(s, axis=-1)
    p = jnp.exp(s - lse[..., None])
    o = jnp.einsum('brnhqk,brnkhd->brnqhd', p, vb.astype(F32))

    def from_classes(t):
        X = t.shape[-1]
        t = t.reshape(B, dilation, Lp // dilation, H, X).transpose(0, 2, 1, 3, 4)
        return t.reshape(B, Lp, H, X)[:, :S]

    lse_pos = from_classes(jnp.swapaxes(lse, -1, -2)[..., None])[..., 0]
    return from_classes(o), lse_pos


def differential_attention(q1, q2, k1, k2, v, slopes, lam):
    B, S, H, dk = q1.shape
    n_blk = S // BLOCK
    scale = dk ** -0.5
    kpos = jnp.arange(S)
    vf = v.astype(F32)

    def one_block(args):
        qb1, qb2, blk = args
        dist = (blk * BLOCK + jnp.arange(BLOCK))[:, None] - kpos[None, :]
        causal = dist >= 0
        bias = -slopes[:, None, None] * dist.astype(F32)

        def probs(qb, kk):
            s = jnp.einsum('bqhd,bkhd->bhqk', qb, kk).astype(F32) * scale + bias
            return jax.nn.softmax(jnp.where(causal, s, NEG_INF), axis=-1)

        a = probs(qb1, k1) - lam * probs(qb2, k2)
        return jnp.einsum('bhqk,bkhd->bqhd', a, vf)

    to_blocks = lambda t: t.reshape(B, n_blk, BLOCK, H, dk).transpose(1, 0, 2, 3, 4)
    out = lax.map(one_block, (to_blocks(q1), to_blocks(q2), jnp.arange(n_blk)))
    return out.transpose(1, 0, 2, 3, 4).reshape(B, S, H, 2 * dk)


def dilated_diff_mixer(u, w_in, dil_q_gain, dil_k_gain, diff_q_gain, diff_k_gain,
                       lam_q1, lam_k1, lam_q2, lam_k2, diff_out_gain, w_out, layer):
    B, S, _ = u.shape
    dq, dk, dv, fq, fk, fv = jnp.split(u @ w_in, ATTN_SPLITS, axis=-1)
    slopes = alibi_slopes(N_ATTN_HEADS)
    dq = rms_norm(dq.reshape(B, S, N_DIL_HEADS, HEAD_DIM), dil_q_gain)
    dk = rms_norm(dk.reshape(B, S, N_DIL_HEADS, HEAD_DIM), dil_k_gain)
    dv = dv.reshape(B, S, N_DIL_HEADS, HEAD_DIM)
    outs, lses = [], []
    for g, (window, dilation) in enumerate(DIL_GROUPS):
        hs = slice(g * HEADS_PER_DIL, (g + 1) * HEADS_PER_DIL)
        o, lse = dilated_window_attention(dq[:, :, hs], dk[:, :, hs], dv[:, :, hs], slopes[hs], window, dilation)
        outs.append(o)
        lses.append(lse)
    alpha = jax.nn.softmax(jnp.stack(lses), axis=0)
    dil_out = jnp.concatenate([alpha[g][..., None] * outs[g] for g in range(len(DIL_GROUPS))], axis=2)
    fq = rms_norm(fq.reshape(B, S, N_DIFF_HEADS, 2, DIFF_QK_DIM), diff_q_gain)
    fk = rms_norm(fk.reshape(B, S, N_DIFF_HEADS, 2, DIFF_QK_DIM), diff_k_gain)
    fv = fv.reshape(B, S, N_DIFF_HEADS, HEAD_DIM)
    lam_init = 0.8 - 0.6 * math.exp(-0.3 * layer)
    lam = (jnp.exp(jnp.sum(lam_q1.astype(F32) * lam_k1.astype(F32)))
           - jnp.exp(jnp.sum(lam_q2.astype(F32) * lam_k2.astype(F32))) + lam_init)
    diff = differential_attention(fq[..., 0, :], fq[..., 1, :], fk[..., 0, :], fk[..., 1, :],
                                  fv, slopes[N_DIL_HEADS:], lam)
    diff = rms_norm(diff, diff_out_gain) * (1.0 - lam_init)
    mixed = jnp.concatenate([dil_out, diff], axis=2).reshape(B, S, D_MODEL).astype(u.dtype)
    return mixed @ w_out


def chunk_gated_recurrence(q, k, v, log_f):
    B, S, H, K = q.shape
    V = v.shape[-1]
    C = HGRN_CHUNK
    n = S // C
    rs = lambda t: t.reshape(B, n, C, H, t.shape[-1]).transpose(1, 0, 3, 2, 4)
    q, k, v, lf = rs(q), rs(k), rs(v), rs(log_f)
    b = jnp.cumsum(lf, axis=-2)
    b_last = b[..., -1:, :]
    b_mid = b[..., C // 2:C // 2 + 1, :]
    scores = jnp.einsum('nbhtk,nbhsk->nbhts', q * jnp.exp(b - b_mid), k * jnp.exp(b_mid - b))
    causal = jnp.tril(jnp.ones((C, C), dtype=bool))
    o_intra = jnp.einsum('nbhts,nbhsv->nbhtv', jnp.where(causal, scores, 0.0), v)
    q_inter = q * jnp.exp(b)
    k_state = k * jnp.exp(b_last - b)
    chunk_decay = jnp.exp(b_last[..., 0, :])

    def step(state, xs):
        qi, ks, vc, dec = xs
        o = jnp.einsum('bhtk,bhkv->bhtv', qi, state)
        state = dec[..., None] * state + jnp.einsum('bhsk,bhsv->bhkv', ks, vc)
        return state, o

    _, o_inter = lax.scan(step, jnp.zeros((B, H, K, V), F32), (q_inter, k_state, v, chunk_decay))
    o = o_intra + o_inter
    return o.transpose(1, 0, 3, 2, 4).reshape(B, S, H, V)


def hgrn2_mixer(u, w_in, lb_logits, out_gain, w_out, layer):
    B, S, _ = u.shape
    q, f_logit, i, g = jnp.split(u @ w_in, [HGRN_FORGET_DIM, 2 * HGRN_FORGET_DIM,
                                            2 * HGRN_FORGET_DIM + D_MODEL], axis=-1)
    sm = jax.nn.softmax(lb_logits.astype(F32), axis=0)
    lb = (jnp.cumsum(sm, axis=0) - sm[0])[layer]
    fl = f_logit.astype(F32)
    log_f = jnp.logaddexp(jnp.log(lb), jnp.log1p(-lb) + jax.nn.log_sigmoid(fl))
    k = (1.0 - lb) * jax.nn.sigmoid(-fl)
    q = jax.nn.silu(q.astype(F32))
    heads = lambda t, d: t.reshape(B, S, HGRN_HEADS, d)
    o = chunk_gated_recurrence(heads(q, HGRN_EXPAND), heads(k, HGRN_EXPAND),
                               heads(i.astype(F32), HGRN_VDIM), heads(log_f, HGRN_EXPAND))
    o = rms_norm(o, out_gain) * jax.nn.sigmoid(heads(g.astype(F32), HGRN_VDIM))
    return o.reshape(B, S, D_MODEL).astype(u.dtype) @ w_out


def swiglu(u, w_gate, w_up, w_down):
    return (jax.nn.silu(u @ w_gate) * (u @ w_up)) @ w_down


def moe_swiglu(u, w_router, w_gate, w_up, w_down):
    B, S, D = u.shape
    t = u.reshape(B * S, D)
    logits = (t @ w_router).astype(F32)
    top_vals, top_idx = lax.top_k(logits, TOP_K)
    top_w = jax.nn.softmax(top_vals, axis=-1)
    gates = jnp.sum(jax.nn.one_hot(top_idx, N_EXPERTS, dtype=F32) * top_w[..., None], axis=1)
    out = jnp.zeros((B * S, D), F32)
    for e in range(N_EXPERTS):
        y = swiglu(t, w_gate[e], w_up[e], w_down[e]).astype(F32)
        out = out + gates[:, e:e + 1] * y
    return out.reshape(B, S, D).astype(u.dtype)


def setup_inputs(seed: int = 0) -> dict:
    key = jax.random.key(seed)
    ks = iter(jax.random.split(key, 40))
    nrm = lambda shape, scale: jax.random.normal(next(ks), shape, F32) * scale
    gain = lambda shape: 1.0 + nrm(shape, 0.02)
    D, E, O = D_MODEL, N_EVEN, N_ODD
    return {
        "x": nrm((BATCH, SEQ, D), 1.0),
        "attn_norm": gain((E, D)),
        "attn_w_in": nrm((E, D, ATTN_IN_WIDTH), D ** -0.5),
        "dil_q_gain": gain((E, HEAD_DIM)),
        "dil_k_gain": gain((E, HEAD_DIM)),
        "diff_q_gain": gain((E, DIFF_QK_DIM)),
        "diff_k_gain": gain((E, DIFF_QK_DIM)),
        "diff_lambda_q1": nrm((E, DIFF_QK_DIM), 0.1),
        "diff_lambda_k1": nrm((E, DIFF_QK_DIM), 0.1),
        "diff_lambda_q2": nrm((E, DIFF_QK_DIM), 0.1),
        "diff_lambda_k2": nrm((E, DIFF_QK_DIM), 0.1),
        "diff_out_gain": gain((E, HEAD_DIM)),
        "attn_w_out": nrm((E, D, D), D ** -0.5),
        "ffn_norm": gain((E, D)),
        "ffn_w_gate": nrm((E, D, D_FF), D ** -0.5),
        "ffn_w_up": nrm((E, D, D_FF), D ** -0.5),
        "ffn_w_down": nrm((E, D_FF, D), D_FF ** -0.5),
        "hgrn_norm": gain((O, D)),
        "hgrn_w_in": nrm((O, D, HGRN_IN_WIDTH), D ** -0.5),
        "hgrn_lb_logits": nrm((DEPTH, HGRN_FORGET_DIM), 0.5),
        "hgrn_out_gain": gain((O, HGRN_VDIM)),
        "hgrn_w_out": nrm((O, D, D), D ** -0.5),
        "moe_norm": gain((O, D)),
        "moe_w_router": nrm((O, D, N_EXPERTS), D ** -0.5),
        "moe_w_gate": nrm((O, N_EXPERTS, D, D_FF_EXPERT), D ** -0.5),
        "moe_w_up": nrm((O, N_EXPERTS, D, D_FF_EXPERT), D ** -0.5),
        "moe_w_down": nrm((O, N_EXPERTS, D_FF_EXPERT, D), D_FF_EXPERT ** -0.5),
    }


def reference(x, attn_norm, attn_w_in, dil_q_gain, dil_k_gain, diff_q_gain, diff_k_gain,
              diff_lambda_q1, diff_lambda_k1, diff_lambda_q2, diff_lambda_k2, diff_out_gain,
              attn_w_out, ffn_norm, ffn_w_gate, ffn_w_up, ffn_w_down,
              hgrn_norm, hgrn_w_in, hgrn_lb_logits, hgrn_out_gain, hgrn_w_out,
              moe_norm, moe_w_router, moe_w_gate, moe_w_up, moe_w_down):
    h = x
    for layer in range(DEPTH):
        j = layer // 2
        if layer % 2 == 0:
            h = h + dilated_diff_mixer(rms_norm(h, attn_norm[j]), attn_w_in[j], dil_q_gain[j], dil_k_gain[j],
                                       diff_q_gain[j], diff_k_gain[j], diff_lambda_q1[j], diff_lambda_k1[j],
                                       diff_lambda_q2[j], diff_lambda_k2[j], diff_out_gain[j],
                                       attn_w_out[j], layer).astype(h.dtype)
            h = h + swiglu(rms_norm(h, ffn_norm[j]), ffn_w_gate[j], ffn_w_up[j], ffn_w_down[j]).astype(h.dtype)
        else:
            h = h + hgrn2_mixer(rms_norm(h, hgrn_norm[j]), hgrn_w_in[j], hgrn_lb_logits,
                                hgrn_out_gain[j], hgrn_w_out[j], layer).astype(h.dtype)
            h = h + moe_swiglu(rms_norm(h, moe_norm[j]), moe_w_router[j], moe_w_gate[j],
                               moe_w_up[j], moe_w_down[j]).astype(h.dtype)
    return h
```

```python
import functools
import math

import jax
import jax.numpy as jnp
from jax import lax
from jax.experimental import pallas as pl
from jax.experimental.pallas import tpu as pltpu

F32 = jnp.float32
BF16 = jnp.bfloat16
I32 = jnp.int32

D_MODEL = 1024
HEAD_DIM = 64
N_ATTN_HEADS = 16
N_DIL_HEADS = 12
N_DIFF_HEADS = 4
DIL_GROUPS = ((128, 1), (512, 4), (2048, 16))
DIFF_QK_DIM = 32
ATTN_IN_WIDTH = 3072
QBLOCK = 128
HGRN_HEADS = 8
HGRN_CHUNK = 64
D_FF = 2816
N_EXPERTS = 8
D_FF_EXPERT = 3584
EPS = 1e-6
NEG_INF = -1e30
LANES = 128
COLB = 256
VMEM_LIMIT = 56 * 1024 * 1024


def _cparams(*sem):
    return pltpu.CompilerParams(dimension_semantics=sem, vmem_limit_bytes=VMEM_LIMIT)


def _split_dot(x, m, terms):
    acc = None
    r = x
    for t in range(terms):
        part = r.astype(BF16)
        d = jnp.dot(part, m, preferred_element_type=F32)
        acc = d if acc is None else acc + d
        if t + 1 < terms:
            r = r - part.astype(F32)
    return acc


def _seg_matrix(n, seg):
    i = jnp.arange(n)
    return (i[:, None] // seg == i[None, :] // seg).astype(BF16)


def _rms_rows(x, gain_row):
    return x * lax.rsqrt(jnp.mean(x * x, axis=-1, keepdims=True) + EPS) * gain_row


def _sigmoid(x):
    return 1.0 / (1.0 + jnp.exp(-x))


def _silu(x):
    return x * _sigmoid(x)


_ATTN_NORM_SEG = (64, 64, 64, 64, 64, 64, 0, 0, 0, 32, 32, 0)


def _attn_inproj_kernel(h_ref, ng_ref, w_ref, cg_ref, s64_ref, s32_ref, o_ref):
    xb = _rms_rows(h_ref[...], ng_ref[...]).astype(BF16)
    for c, seg in enumerate(_ATTN_NORM_SEG):
        cols = slice(c * COLB, (c + 1) * COLB)
        y = jnp.dot(xb, w_ref[:, cols], preferred_element_type=F32)
        if seg:
            m = s64_ref[...] if seg == 64 else s32_ref[...]
            ms = _split_dot(y * y, m, 2) * (1.0 / seg)
            y = y * lax.rsqrt(ms + EPS) * cg_ref[c:c + 1, :]
        o_ref[:, cols] = y.astype(BF16)


def _attn_inproj(h2d, norm_gain, w_bf, col_gain, tm):
    T = h2d.shape[0]
    const = lambda i: (0, 0)
    return pl.pallas_call(
        _attn_inproj_kernel,
        grid=(T // tm,),
        in_specs=[
            pl.BlockSpec((tm, D_MODEL), lambda i: (i, 0)),
            pl.BlockSpec((1, D_MODEL), const),
            pl.BlockSpec((D_MODEL, ATTN_IN_WIDTH), const),
            pl.BlockSpec((ATTN_IN_WIDTH // COLB, COLB), const),
            pl.BlockSpec((COLB, COLB), const),
            pl.BlockSpec((COLB, COLB), const),
        ],
        out_specs=pl.BlockSpec((tm, ATTN_IN_WIDTH), lambda i: (i, 0)),
        out_shape=jax.ShapeDtypeStruct((T, ATTN_IN_WIDTH), BF16),
        compiler_params=_cparams("parallel"),
        name="attn_inproj",
    )(h2d, norm_gain, w_bf, col_gain, _seg_matrix(COLB, 64), _seg_matrix(COLB, 32))


def _dil_kernel(q_ref, kp_ref, kc_ref, vp_ref, vc_ref, o_ref, lse_ref, *, slopes, dilation, ub):
    u = pl.program_id(2)
    q = q_ref[...]
    kcat = jnp.concatenate([kp_ref[...], kc_ref[...]], axis=0)
    vcat = jnp.concatenate([vp_ref[...], vc_ref[...]], axis=0)
    qi = lax.broadcasted_iota(I32, (QBLOCK, 2 * QBLOCK), 0)
    kj = lax.broadcasted_iota(I32, (QBLOCK, 2 * QBLOCK), 1)
    step = qi + QBLOCK - kj
    in_window = (step >= 0) & (step <= QBLOCK)
    stepf = step.astype(F32)
    lane = lax.broadcasted_iota(I32, (QBLOCK, LANES), 1)
    low_half = lane < HEAD_DIM
    for i in range(ub // QBLOCK):
        valid = in_window
        if i == 0:
            valid = in_window & ((kj >= QBLOCK) | (u > 0))
        qb = q[i * QBLOCK:(i + 1) * QBLOCK]
        kb = kcat[i * QBLOCK:(i + 2) * QBLOCK]
        vb = vcat[i * QBLOCK:(i + 2) * QBLOCK]
        for pair in range(2):
            lanes = slice(pair * LANES, (pair + 1) * LANES)
            qp, kp, vp = qb[:, lanes], kb[:, lanes], vb[:, lanes]
            outs, lses = [], []
            for hh in range(2):
                slope = slopes[pair * 2 + hh]
                qm = jnp.where(low_half if hh == 0 else ~low_half, qp, jnp.zeros_like(qp))
                s = lax.dot_general(qm, kp, (((1,), (1,)), ((), ())), preferred_element_type=F32)
                s = s - (slope * dilation) * stepf
                s = jnp.where(valid, s, NEG_INF)
                m = jnp.max(s, axis=-1, keepdims=True)
                e = jnp.exp(s - m)
                l = jnp.sum(e, axis=-1, keepdims=True)
                pv = jnp.dot(e.astype(BF16), vp, preferred_element_type=F32)
                outs.append(pv / l)
                lses.append(jnp.broadcast_to(m + jnp.log(l), (QBLOCK, LANES)))
            rows = slice(i * QBLOCK, (i + 1) * QBLOCK)
            o_ref[rows, lanes] = jnp.where(low_half, outs[0], outs[1])
            lse_ref[rows, lanes] = jnp.where(low_half, lses[0], lses[1])


def _dilated_group(proj, g, dilation, slopes):
    B, S, W = proj.shape
    U = S // dilation
    ub = min(512, U)
    sub = ub // QBLOCK
    nblk = W // COLB
    view = proj.reshape(B, U, dilation * W)
    qcol = lambda r: r * nblk + g
    kcol = lambda r: r * nblk + 3 + g
    vcol = lambda r: r * nblk + 6 + g
    prev = lambda u: jnp.maximum(u * sub - 1, 0)
    kern = functools.partial(_dil_kernel, slopes=slopes, dilation=dilation, ub=ub)
    o, lse = pl.pallas_call(
        kern,
        grid=(B, dilation, U // ub),
        in_specs=[
            pl.BlockSpec((None, ub, COLB), lambda b, r, u: (b, u, qcol(r))),
            pl.BlockSpec((None, QBLOCK, COLB), lambda b, r, u: (b, prev(u), kcol(r))),
            pl.BlockSpec((None, ub, COLB), lambda b, r, u: (b, u, kcol(r))),
            pl.BlockSpec((None, QBLOCK, COLB), lambda b, r, u: (b, prev(u), vcol(r))),
            pl.BlockSpec((None, ub, COLB), lambda b, r, u: (b, u, vcol(r))),
        ],
        out_specs=[
            pl.BlockSpec((None, ub, COLB), lambda b, r, u: (b, u, r)),
            pl.BlockSpec((None, ub, COLB), lambda b, r, u: (b, u, r)),
        ],
        out_shape=[jax.ShapeDtypeStruct((B, U, dilation * COLB), F32)] * 2,
        compiler_params=_cparams("parallel", "parallel", "parallel"),
        name=f"dilated_attn_d{dilation}",
    )(view, view, view, view, view)
    return o.reshape(B, S, COLB), lse.reshape(B, S, COLB)


def _diff_kernel(lam_ref, q_ref, k_ref, v_ref, pos_ref, sl_ref, og_ref, s64_ref, o_ref,
                 m_ref, l_ref, acc_ref, *, tq):
    i = pl.program_id(1)
    q = q_ref[...]
    lane = lax.broadcasted_iota(I32, (tq, LANES), 1)
    low_half = lane < HEAD_DIM
    qaug = []
    for pair in range(2):
        qp = q[:, pair * LANES:(pair + 1) * LANES]
        for hh in range(2):
            h = pair * 2 + hh
            feat = jnp.broadcast_to(sl_ref[h:h + 1, :], (tq, LANES)).astype(BF16)
            for mu in range(2):
                lo = hh * HEAD_DIM + mu * DIFF_QK_DIM
                sel = (lane >= lo) & (lane < lo + DIFF_QK_DIM)
                qaug.append(jnp.concatenate([jnp.where(sel, qp, jnp.zeros_like(qp)), feat], axis=1))

    m_ref[...] = jnp.full(m_ref.shape, NEG_INF, F32)
    l_ref[...] = jnp.zeros(l_ref.shape, F32)
    acc_ref[...] = jnp.zeros(acc_ref.shape, F32)

    def block(j, masked):
        ks = pl.multiple_of(j * tq, tq)
        kblk = k_ref[pl.ds(ks, tq), :]
        vblk = v_ref[pl.ds(ks, tq), :]
        pblk = pos_ref[pl.ds(ks, tq), :]
        if masked:
            row = lax.broadcasted_iota(I32, (tq, tq), 0)
            col = lax.broadcasted_iota(I32, (tq, tq), 1)
            causal = col <= row
        for pair in range(2):
            lanes = slice(pair * LANES, (pair + 1) * LANES)
            kaug = jnp.concatenate([kblk[:, lanes], pblk], axis=1)
            vp = vblk[:, lanes]
            for mu in range(2):
                acc = acc_ref[pair * 2 + mu]
                for hh in range(2):
                    idx = (pair * 2 + hh) * 2 + mu
                    s = lax.dot_general(qaug[idx], kaug, (((1,), (1,)), ((), ())),
                                        preferred_element_type=F32)
                    if masked:
                        s = jnp.where(causal, s, NEG_INF)
                    m_prev = m_ref[idx]
                    m_next = jnp.maximum(m_prev, jnp.max(s, axis=-1, keepdims=True))
                    alpha = jnp.exp(m_prev - m_next)
                    p = jnp.exp(s - m_next[:, :1])
                    l_ref[idx] = alpha * l_ref[idx] + jnp.sum(p, axis=-1, keepdims=True)
                    m_ref[idx] = m_next
                    pv = jnp.dot(p.astype(BF16), vp, preferred_element_type=F32)
                    acc = jnp.where(low_half if hh == 0 else ~low_half, alpha * acc + pv, acc)
                acc_ref[pair * 2 + mu] = acc

    def full_block(j, carry):
        block(j, False)
        return carry

    lax.fori_loop(0, i, full_block, 0)
    block(i, True)

    lam = lam_ref[0]
    for pair in range(2):
        o = None
        for mu in range(2):
            l_sel = jnp.where(low_half, l_ref[(pair * 2) * 2 + mu], l_ref[(pair * 2 + 1) * 2 + mu])
            term = acc_ref[pair * 2 + mu] / l_sel
            o = term if mu == 0 else o - lam * term
        ms = _split_dot(o * o, s64_ref[...], 2) * (1.0 / HEAD_DIM)
        o = o * lax.rsqrt(ms + EPS) * og_ref[:, pair * LANES:(pair + 1) * LANES]
        o_ref[:, pair * LANES:(pair + 1) * LANES] = o.astype(BF16)


def _diff_attention(proj, lam, slope_feat, pos_feat, out_gain, tq):
    B, S, W = proj.shape
    kern = functools.partial(_diff_kernel, tq=tq)
    return pl.pallas_call(
        kern,
        grid=(B, S // tq),
        in_specs=[
            pl.BlockSpec(memory_space=pltpu.SMEM),
            pl.BlockSpec((None, tq, COLB), lambda b, i: (b, i, 9)),
            pl.BlockSpec((None, S, COLB), lambda b, i: (b, 0, 10)),
            pl.BlockSpec((None, S, COLB), lambda b, i: (b, 0, 11)),
            pl.BlockSpec((S, LANES), lambda b, i: (0, 0)),
            pl.BlockSpec((8, LANES), lambda b, i: (0, 0)),
            pl.BlockSpec((1, COLB), lambda b, i: (0, 0)),
            pl.BlockSpec((LANES, LANES), lambda b, i: (0, 0)),
        ],
        out_specs=pl.BlockSpec((None, tq, COLB), lambda b, i: (b, i, 0)),
        out_shape=jax.ShapeDtypeStruct((B, S, COLB), BF16),
        scratch_shapes=[
            pltpu.VMEM((8, tq, LANES), F32),
            pltpu.VMEM((8, tq, LANES), F32),
            pltpu.VMEM((4, tq, LANES), F32),
        ],
        compiler_params=_cparams("parallel", "parallel"),
        name="diff_attn",
    )(lam, proj, proj, proj, pos_feat, slope_feat, out_gain, _seg_matrix(LANES, HEAD_DIM))


def _attn_out_kernel(h_ref, o0, o1, o2, l0, l1, l2, d_ref, w_ref, out_ref):
    a0, a1, a2 = l0[...], l1[...], l2[...]
    mx = jnp.maximum(jnp.maximum(a0, a1), a2)
    e0, e1, e2 = jnp.exp(a0 - mx), jnp.exp(a1 - mx), jnp.exp(a2 - mx)
    inv = 1.0 / (e0 + e1 + e2)
    mixed = jnp.concatenate(
        [(e0 * inv * o0[...]).astype(BF16), (e1 * inv * o1[...]).astype(BF16),
         (e2 * inv * o2[...]).astype(BF16), d_ref[...]], axis=1)
    out_ref[...] = h_ref[...] + jnp.dot(mixed, w_ref[...], preferred_element_type=F32)


def _attn_outproj(h2d, outs, lses, diff, w_bf, tm):
    T = h2d.shape[0]
    row = lambda i: (i, 0)
    blk = pl.BlockSpec((tm, COLB), row)
    return pl.pallas_call(
        _attn_out_kernel,
        grid=(T // tm,),
        in_specs=[pl.BlockSpec((tm, D_MODEL), row)] + [blk] * 7
                 + [pl.BlockSpec((D_MODEL, D_MODEL), lambda i: (0, 0))],
        out_specs=pl.BlockSpec((tm, D_MODEL), row),
        out_shape=jax.ShapeDtypeStruct((T, D_MODEL), F32),
        compiler_params=_cparams("parallel"),
        name="attn_outproj",
    )(h2d, *outs, *lses, diff, w_bf)


def _ffn_kernel(h_ref, ng_ref, wg_ref, wu_ref, wd_ref, o_ref):
    x = h_ref[...]
    xb = _rms_rows(x, ng_ref[...]).astype(BF16)
    g = jnp.dot(xb, wg_ref[...], preferred_element_type=F32)
    u = jnp.dot(xb, wu_ref[...], preferred_element_type=F32)
    a = (_silu(g) * u).astype(BF16)
    o_ref[...] = x + jnp.dot(a, wd_ref[...], preferred_element_type=F32)


def _ffn(h2d, norm_gain, wg, wu, wd, tm):
    T = h2d.shape[0]
    const = lambda i: (0, 0)
    return pl.pallas_call(
        _ffn_kernel,
        grid=(T // tm,),
        in_specs=[
            pl.BlockSpec((tm, D_MODEL), lambda i: (i, 0)),
            pl.BlockSpec((1, D_MODEL), const),
            pl.BlockSpec((D_MODEL, D_FF), const),
            pl.BlockSpec((D_MODEL, D_FF), const),
            pl.BlockSpec((D_FF, D_MODEL), const),
        ],
        out_specs=pl.BlockSpec((tm, D_MODEL), lambda i: (i, 0)),
        out_shape=jax.ShapeDtypeStruct((T, D_MODEL), F32),
        compiler_params=_cparams("parallel"),
        name="ffn_swiglu",
    )(h2d, norm_gain, wg, wu, wd)


def _hgrn_inproj_kernel(h_ref, ng_ref, w_ref, lb_ref, q_ref, lf_ref, k_ref, v_ref, g_ref):
    xb = _rms_rows(h_ref[...], ng_ref[...]).astype(BF16)
    sec = lambda c: jnp.dot(xb, w_ref[:, c * D_MODEL:(c + 1) * D_MODEL], preferred_element_type=F32)
    q_ref[...] = _silu(sec(0)).astype(BF16)
    fl = sec(1)
    log_lb, log1m_lb, one_m_lb = lb_ref[0:1, :], lb_ref[1:2, :], lb_ref[2:3, :]
    log_sig = jnp.minimum(fl, 0.0) - jnp.log1p(jnp.exp(-jnp.abs(fl)))
    c = log1m_lb + log_sig
    lf_ref[...] = jnp.maximum(log_lb, c) + jnp.log1p(jnp.exp(-jnp.abs(log_lb - c)))
    k_ref[...] = (one_m_lb * _sigmoid(-fl)).astype(BF16)
    v_ref[...] = sec(2).astype(BF16)
    g_ref[...] = _sigmoid(sec(3)).astype(BF16)


def _hgrn_inproj(h2d, norm_gain, w_bf, lb_rows, tm):
    T = h2d.shape[0]
    const = lambda i: (0, 0)
    row = pl.BlockSpec((tm, D_MODEL), lambda i: (i, 0))
    bf = jax.ShapeDtypeStruct((T, D_MODEL), BF16)
    return pl.pallas_call(
        _hgrn_inproj_kernel,
        grid=(T // tm,),
        in_specs=[row, pl.BlockSpec((1, D_MODEL), const),
                  pl.BlockSpec((D_MODEL, 4 * D_MODEL), const),
                  pl.BlockSpec((8, D_MODEL), const)],
        out_specs=[row] * 5,
        out_shape=[bf, jax.ShapeDtypeStruct((T, D_MODEL), F32), bf, bf, bf],
        compiler_params=_cparams("parallel"),
        name="hgrn_inproj",
    )(h2d, norm_gain, w_bf, lb_rows)


def _hgrn_kernel(q_ref, lf_ref, k_ref, v_ref, g_ref, og_ref, o_ref, st_ref, *, rows):
    @pl.when(pl.program_id(1) == 0)
    def _():
        st_ref[...] = jnp.zeros(st_ref.shape, F32)

    C = HGRN_CHUNK
    ti = lax.broadcasted_iota(I32, (C, C), 0)
    si = lax.broadcasted_iota(I32, (C, C), 1)
    causal = si <= ti
    tri = jnp.where(causal, 1.0, 0.0).astype(BF16)
    for hd in range(HGRN_HEADS):
        lanes = slice(hd * LANES, (hd + 1) * LANES)
        for c in range(rows // C):
            rws = slice(c * C, (c + 1) * C)
            q = q_ref[rws, lanes].astype(F32)
            k = k_ref[rws, lanes].astype(F32)
            v = v_ref[rws, lanes]
            b = _split_dot_lhs(tri, lf_ref[rws, lanes])
            b_last = b[C - 1:C, :]
            b_mid = b[C // 2:C // 2 + 1, :]
            qs = (q * jnp.exp(b - b_mid)).astype(BF16)
            ks = (k * jnp.exp(b_mid - b)).astype(BF16)
            scores = lax.dot_general(qs, ks, (((1,), (1,)), ((), ())), preferred_element_type=F32)
            scores = jnp.where(causal, scores, 0.0).astype(BF16)
            o = jnp.dot(scores, v, preferred_element_type=F32)
            st = st_ref[hd]
            qi = (q * jnp.exp(b)).astype(BF16)
            o = o + lax.dot_general(qi, st.astype(BF16), (((1,), (1,)), ((), ())),
                                    preferred_element_type=F32)
            kst = (k * jnp.exp(b_last - b)).astype(BF16)
            st_ref[hd] = jnp.exp(b_last) * st + lax.dot_general(
                v, kst, (((0,), (0,)), ((), ())), preferred_element_type=F32)
            o = o * lax.rsqrt(jnp.mean(o * o, axis=-1, keepdims=True) + EPS) * og_ref[...]
            o_ref[rws, lanes] = (o * g_ref[rws, lanes].astype(F32)).astype(BF16)


def _split_dot_lhs(m, x):
    acc = None
    r = x
    for t in range(3):
        part = r.astype(BF16)
        d = jnp.dot(m, part, preferred_element_type=F32)
        acc = d if acc is None else acc + d
        if t < 2:
            r = r - part.astype(F32)
    return acc


def _hgrn_recurrence(q, lf, k, v, g, out_gain, B, S, rows):
    T = B * S
    nseq = S // rows
    row = pl.BlockSpec((rows, D_MODEL), lambda b, s: (b * nseq + s, 0))
    kern = functools.partial(_hgrn_kernel, rows=rows)
    return pl.pallas_call(
        kern,
        grid=(B, nseq),
        in_specs=[row] * 5 + [pl.BlockSpec((1, LANES), lambda b, s: (0, 0))],
        out_specs=row,
        out_shape=jax.ShapeDtypeStruct((T, D_MODEL), BF16),
        scratch_shapes=[pltpu.VMEM((HGRN_HEADS, LANES, LANES), F32)],
        compiler_params=_cparams("parallel", "arbitrary"),
        name="hgrn_recurrence",
    )(q, lf, k, v, g, out_gain)


def _proj_res_kernel(h_ref, x_ref, w_ref, o_ref):
    o_ref[...] = h_ref[...] + jnp.dot(x_ref[...], w_ref[...], preferred_element_type=F32)


def _proj_residual(h2d, x_bf, w_bf, tm):
    T = h2d.shape[0]
    row = pl.BlockSpec((tm, D_MODEL), lambda i: (i, 0))
    return pl.pallas_call(
        _proj_res_kernel,
        grid=(T // tm,),
        in_specs=[row, row, pl.BlockSpec((D_MODEL, D_MODEL), lambda i: (0, 0))],
        out_specs=row,
        out_shape=jax.ShapeDtypeStruct((T, D_MODEL), F32),
        compiler_params=_cparams("parallel"),
        name="proj_residual",
    )(h2d, x_bf, w_bf)


def _router_kernel(h_ref, ng_ref, whi_ref, wlo_ref, xn_ref, info_ref, cnt_ref, carry_ref, *, tm):
    @pl.when(pl.program_id(0) == 0)
    def _():
        carry_ref[...] = jnp.zeros(carry_ref.shape, F32)

    xn = _rms_rows(h_ref[...], ng_ref[...])
    xn_ref[...] = xn
    xhi = xn.astype(BF16)
    xlo = (xn - xhi.astype(F32)).astype(BF16)
    logits = (jnp.dot(xhi, whi_ref[...], preferred_element_type=F32)
              + jnp.dot(xhi, wlo_ref[...], preferred_element_type=F32)
              + jnp.dot(xlo, whi_ref[...], preferred_element_type=F32))
    lane = lax.broadcasted_iota(I32, (tm, LANES), 1)
    lanef = lane.astype(F32)
    logits = jnp.where(lane < N_EXPERTS, logits, -jnp.inf)
    m1 = jnp.max(logits, axis=-1, keepdims=True)
    i1 = jnp.min(jnp.where(logits == m1, lanef, float(LANES)), axis=-1, keepdims=True)
    oh1 = lanef == i1
    rest = jnp.where(oh1, -jnp.inf, logits)
    m2 = jnp.max(rest, axis=-1, keepdims=True)
    i2 = jnp.min(jnp.where(rest == m2, lanef, float(LANES)), axis=-1, keepdims=True)
    oh2 = lanef == i2
    e = jnp.exp(m2 - m1)
    w1 = 1.0 / (1.0 + e)
    w2 = e * w1

    chosen = jnp.where(oh1 | oh2, 1.0, 0.0)
    r = lax.broadcasted_iota(I32, (tm, tm), 0)
    c = lax.broadcasted_iota(I32, (tm, tm), 1)
    before = jnp.where(c < r, 1.0, 0.0).astype(BF16)
    excl = jnp.dot(before, chosen.astype(BF16), preferred_element_type=F32) + carry_ref[0:1, :]
    rank1 = jnp.sum(jnp.where(oh1, excl, 0.0), axis=-1, keepdims=True)
    rank2 = jnp.sum(jnp.where(oh2, excl, 0.0), axis=-1, keepdims=True)
    total = carry_ref[0:1, :] + jnp.sum(chosen, axis=0, keepdims=True)
    carry_ref[...] = jnp.broadcast_to(total, carry_ref.shape)
    cnt_ref[...] = jnp.broadcast_to(total, cnt_ref.shape)

    info = jnp.where(lane == 0, i1, 0.0)
    info = jnp.where(lane == 1, i2, info)
    info = jnp.where(lane == 2, rank1, info)
    info = jnp.where(lane == 3, rank2, info)
    info = jnp.where(lane == 4, w1, info)
    info = jnp.where(lane == 5, w2, info)
    info_ref[...] = info


def _router(h2d, norm_gain, w_hi, w_lo, tm):
    T = h2d.shape[0]
    const = lambda i: (0, 0)
    kern = functools.partial(_router_kernel, tm=tm)
    return pl.pallas_call(
        kern,
        grid=(T // tm,),
        in_specs=[
            pl.BlockSpec((tm, D_MODEL), lambda i: (i, 0)),
            pl.BlockSpec((1, D_MODEL), const),
            pl.BlockSpec((D_MODEL, LANES), const),
            pl.BlockSpec((D_MODEL, LANES), const),
        ],
        out_specs=[
            pl.BlockSpec((tm, D_MODEL), lambda i: (i, 0)),
            pl.BlockSpec((tm, LANES), lambda i: (i, 0)),
            pl.BlockSpec((8, LANES), const),
        ],
        out_shape=[
            jax.ShapeDtypeStruct((T, D_MODEL), F32),
            jax.ShapeDtypeStruct((T, LANES), F32),
            jax.ShapeDtypeStruct((8, LANES), F32),
        ],
        scratch_shapes=[pltpu.VMEM((8, LANES), F32)],
        compiler_params=_cparams("arbitrary"),
        name="moe_router",
    )(h2d, norm_gain, w_hi, w_lo)


def _dispatch_kernel(pos_ref, xn_hbm, init_hbm, xs_hbm, sem, *, tt):
    del init_hbm
    base = pl.program_id(0) * tt

    def copy(t, s):
        return pltpu.make_async_copy(
            xn_hbm.at[pl.ds(base + t, 1)], xs_hbm.at[pl.ds(pos_ref[0, 0, 2 * t + s], 1)], sem)

    def start(t, carry):
        copy(t, 0).start()
        copy(t, 1).start()
        return carry

    def wait(t, carry):
        copy(t, 0).wait()
        copy(t, 1).wait()
        return carry

    lax.fori_loop(0, tt, start, 0)
    lax.fori_loop(0, tt, wait, 0)


def _dispatch(xn, pos, n_rows, tt):
    T = xn.shape[0]
    pos3 = pos.reshape(T // tt, 1, 2 * tt)
    kern = functools.partial(_dispatch_kernel, tt=tt)
    return pl.pallas_call(
        kern,
        grid=(T // tt,),
        in_specs=[
            pl.BlockSpec((1, 1, 2 * tt), lambda i: (i, 0, 0), memory_space=pltpu.SMEM),
            pl.BlockSpec(memory_space=pl.ANY),
            pl.BlockSpec(memory_space=pl.ANY),
        ],
        out_specs=pl.BlockSpec(memory_space=pl.ANY),
        out_shape=jax.ShapeDtypeStruct((n_rows, D_MODEL), F32),
        scratch_shapes=[pltpu.SemaphoreType.DMA(())],
        input_output_aliases={2: 0},
        compiler_params=_cparams("arbitrary"),
        name="moe_dispatch",
    )(pos3, xn, jnp.zeros((n_rows, D_MODEL), F32))


def _expert_kernel(te_ref, na_ref, xs_ref, wg_ref, wu_ref, wd_ref, y_ref, xb_ref, acc_ref):
    del te_ref
    f = pl.program_id(1)
    active = pl.program_id(0) < na_ref[0]

    @pl.when(jnp.logical_not(active) & (f == pl.num_programs(1) - 1))
    def _():
        y_ref[...] = jnp.zeros(y_ref.shape, F32)

    @pl.when(active)
    def _():
        @pl.when(f == 0)
        def _():
            xb_ref[...] = xs_ref[...].astype(BF16)
            acc_ref[...] = jnp.zeros(acc_ref.shape, F32)

        xb = xb_ref[...]
        g = jnp.dot(xb, wg_ref[0], preferred_element_type=F32)
        u = jnp.dot(xb, wu_ref[0], preferred_element_type=F32)
        a = (_silu(g) * u).astype(BF16)
        acc_ref[...] += jnp.dot(a, wd_ref[0], preferred_element_type=F32)

        @pl.when(f == pl.num_programs(1) - 1)
        def _():
            y_ref[...] = acc_ref[...]


def _experts(xs, tile_expert, n_active, wg, wu, wd, tme, tf):
    n_rows = xs.shape[0]
    n_tiles = n_rows // tme
    nf = D_FF_EXPERT // tf

    def tile(i, na):
        return jnp.minimum(i, na[0] - 1)

    def fidx(i, f, na):
        return jnp.where(i < na[0], f, nf - 1)

    grid_spec = pltpu.PrefetchScalarGridSpec(
        num_scalar_prefetch=2,
        grid=(n_tiles, nf),
        in_specs=[
            pl.BlockSpec((tme, D_MODEL), lambda i, f, te, na: (tile(i, na), 0)),
            pl.BlockSpec((1, D_MODEL, tf), lambda i, f, te, na: (te[tile(i, na)], 0, fidx(i, f, na))),
            pl.BlockSpec((1, D_MODEL, tf), lambda i, f, te, na: (te[tile(i, na)], 0, fidx(i, f, na))),
            pl.BlockSpec((1, tf, D_MODEL), lambda i, f, te, na: (te[tile(i, na)], fidx(i, f, na), 0)),
        ],
        out_specs=pl.BlockSpec((tme, D_MODEL), lambda i, f, te, na: (i, 0)),
        scratch_shapes=[pltpu.VMEM((tme, D_MODEL), BF16), pltpu.VMEM((tme, D_MODEL), F32)],
    )
    return pl.pallas_call(
        _expert_kernel,
        grid_spec=grid_spec,
        out_shape=jax.ShapeDtypeStruct((n_rows, D_MODEL), F32),
        compiler_params=_cparams("arbitrary", "arbitrary"),
        name="moe_experts",
    )(tile_expert, n_active, xs, wg, wu, wd)


def _combine_kernel(pos_ref, info_ref, h_ref, y_hbm, o_ref, buf_ref, sem, *, tc):
    def copy(t, s):
        return pltpu.make_async_copy(
            y_hbm.at[pl.ds(pos_ref[0, 0, 2 * t + s], 1)], buf_ref.at[s, pl.ds(t, 1)], sem)

    def start(t, carry):
        copy(t, 0).start()
        copy(t, 1).start()
        return carry

    def wait(t, carry):
        copy(t, 0).wait()
        copy(t, 1).wait()
        return carry

    lax.fori_loop(0, tc, start, 0)
    lax.fori_loop(0, tc, wait, 0)
    info = info_ref[...]
    lane = lax.broadcasted_iota(I32, (tc, LANES), 1)
    w1 = jnp.sum(jnp.where(lane == 4, info, 0.0), axis=-1, keepdims=True)
    w2 = jnp.sum(jnp.where(lane == 5, info, 0.0), axis=-1, keepdims=True)
    o_ref[...] = h_ref[...] + (w1 * buf_ref[0] + w2 * buf_ref[1])


def _combine(h2d, info, pos, y, tc):
    T = h2d.shape[0]
    pos3 = pos.reshape(T // tc, 1, 2 * tc)
    kern = functools.partial(_combine_kernel, tc=tc)
    return pl.pallas_call(
        kern,
        grid=(T // tc,),
        in_specs=[
            pl.BlockSpec((1, 1, 2 * tc), lambda i: (i, 0, 0), memory_space=pltpu.SMEM),
            pl.BlockSpec((tc, LANES), lambda i: (i, 0)),
            pl.BlockSpec((tc, D_MODEL), lambda i: (i, 0)),
            pl.BlockSpec(memory_space=pl.ANY),
        ],
        out_specs=pl.BlockSpec((tc, D_MODEL), lambda i: (i, 0)),
        out_shape=jax.ShapeDtypeStruct((T, D_MODEL), F32),
        scratch_shapes=[pltpu.VMEM((2, tc, D_MODEL), F32), pltpu.SemaphoreType.DMA(())],
        compiler_params=_cparams("arbitrary"),
        name="moe_combine",
    )(pos3, info, h2d, y)


def _bf16_pieces(x, n):
    pieces, r = [], x.astype(F32)
    for _ in range(n):
        p = r.astype(BF16)
        pieces.append(p)
        r = r - p.astype(F32)
    return pieces


def _alibi_slopes():
    return [2.0 ** (-8.0 * (h + 1.0) / N_ATTN_HEADS) for h in range(N_ATTN_HEADS)]


def _attn_col_gain(dil_q_gain, dil_k_gain, diff_q_gain, diff_k_gain):
    rep = lambda g, n: jnp.tile(g.astype(F32), n)
    ones = jnp.ones((COLB,), F32)
    dq = rep(dil_q_gain, COLB // HEAD_DIM) * (HEAD_DIM ** -0.5)
    dk = rep(dil_k_gain, COLB // HEAD_DIM)
    fq = rep(diff_q_gain, COLB // DIFF_QK_DIM) * (DIFF_QK_DIM ** -0.5)
    fk = rep(diff_k_gain, COLB // DIFF_QK_DIM)
    return jnp.stack([dq, dq, dq, dk, dk, dk, ones, ones, ones, fq, fk, ones])


def _layer0(h2d, B, S, attn_norm, w_in, dq_g, dk_g, fq_g, fk_g, lq1, lk1, lq2, lk2, out_g, w_out,
            ffn_norm, w_gate, w_up, w_down, layer):
    T = B * S
    tm = min(512, T)
    slopes = _alibi_slopes()
    proj = _attn_inproj(h2d, attn_norm.reshape(1, D_MODEL).astype(F32), w_in.astype(BF16),
                        _attn_col_gain(dq_g, dk_g, fq_g, fk_g), tm)
    proj3 = proj.reshape(B, S, ATTN_IN_WIDTH)
    outs, lses = [], []
    for g, (window, dilation) in enumerate(DIL_GROUPS):
        assert window // dilation == QBLOCK and S % (dilation * QBLOCK) == 0
        o, lse = _dilated_group(proj3, g, dilation, tuple(slopes[4 * g:4 * g + 4]))
        outs.append(o.reshape(T, COLB))
        lses.append(lse.reshape(T, COLB))

    lam_init = 0.8 - 0.6 * math.exp(-0.3 * layer)
    lam = (jnp.exp(jnp.sum(lq1.astype(F32) * lk1.astype(F32)))
           - jnp.exp(jnp.sum(lq2.astype(F32) * lk2.astype(F32))) + lam_init).reshape(1)
    sl = jnp.asarray(slopes[N_DIL_HEADS:], F32)
    s_hi, s_lo = _bf16_pieces(sl, 2)
    slope_feat = jnp.zeros((8, LANES), F32)
    slope_feat = slope_feat.at[:N_DIFF_HEADS, 0].set(s_hi.astype(F32)).at[:N_DIFF_HEADS, 1].set(s_hi.astype(F32))
    slope_feat = slope_feat.at[:N_DIFF_HEADS, 2].set(s_lo.astype(F32)).at[:N_DIFF_HEADS, 3].set(s_lo.astype(F32))
    p_hi, p_lo = _bf16_pieces(jnp.arange(S, dtype=F32), 2)
    pos_feat = jnp.zeros((S, LANES), BF16)
    pos_feat = pos_feat.at[:, 0].set(p_hi).at[:, 1].set(p_lo).at[:, 2].set(p_hi).at[:, 3].set(p_lo)
    out_gain = (jnp.tile(out_g.astype(F32), COLB // HEAD_DIM) * (1.0 - lam_init)).reshape(1, COLB)
    diff = _diff_attention(proj3, lam, slope_feat, pos_feat, out_gain, min(512, S)).reshape(T, COLB)

    h2d = _attn_outproj(h2d, outs, lses, diff, w_out.astype(BF16), tm)
    return _ffn(h2d, ffn_norm.reshape(1, D_MODEL).astype(F32), w_gate.astype(BF16),
                w_up.astype(BF16), w_down.astype(BF16), min(256, T))


def _layer1(h2d, B, S, hgrn_norm, w_in, lb_logits, out_gain, w_out,
            moe_norm, w_router, w_gate, w_up, w_down, layer):
    T = B * S
    tm = min(512, T)
    sm = jax.nn.softmax(lb_logits.astype(F32), axis=0)
    lb = (jnp.cumsum(sm, axis=0) - sm[0])[layer]
    lb_rows = jnp.zeros((8, D_MODEL), F32).at[0].set(jnp.log(lb)).at[1].set(jnp.log1p(-lb)).at[2].set(1.0 - lb)
    q, lf, k, v, g = _hgrn_inproj(h2d, hgrn_norm.reshape(1, D_MODEL).astype(F32), w_in.astype(BF16),
                                  lb_rows, tm)
    o = _hgrn_recurrence(q, lf, k, v, g, out_gain.reshape(1, LANES).astype(F32), B, S, min(128, S))
    h2d = _proj_residual(h2d, o, w_out.astype(BF16), tm)
    return _moe_block(h2d, moe_norm, w_router, w_gate, w_up, w_down)


def _moe_block(h2d, moe_norm, w_router, w_gate, w_up, w_down):
    T = h2d.shape[0]
    tme = 512
    w_pad = jnp.zeros((D_MODEL, LANES), F32).at[:, :N_EXPERTS].set(w_router.astype(F32))
    w_hi, w_lo = _bf16_pieces(w_pad, 2)
    xn, info, counts = _router(h2d, moe_norm.reshape(1, D_MODEL).astype(F32), w_hi, w_lo, min(256, T))
    cnt = counts[0, :N_EXPERTS].astype(I32)
    padded = ((cnt + tme - 1) // tme) * tme
    ends = jnp.cumsum(padded)
    starts = ends - padded
    experts = info[:, 0:2].astype(I32)
    ranks = info[:, 2:4].astype(I32)
    pos = (starts[experts] + ranks).reshape(-1)
    n_rows = 2 * T + N_EXPERTS * tme
    n_tiles = n_rows // tme
    tile_expert = jnp.minimum(
        jnp.searchsorted(ends, jnp.arange(n_tiles, dtype=I32) * tme, side="right"), N_EXPERTS - 1).astype(I32)
    n_active = (ends[-1] // tme).astype(I32).reshape(1)
    xs = _dispatch(xn, pos, n_rows, min(256, T))
    y = _experts(xs, tile_expert, n_active, w_gate.astype(BF16), w_up.astype(BF16), w_down.astype(BF16),
                 tme, 512)
    return _combine(h2d, info, pos, y, min(256, T))


def kernel(x, attn_norm, attn_w_in, dil_q_gain, dil_k_gain, diff_q_gain, diff_k_gain, diff_lambda_q1, diff_lambda_k1, diff_lambda_q2, diff_lambda_k2, diff_out_gain, attn_w_out, ffn_norm, ffn_w_gate, ffn_w_up, ffn_w_down, hgrn_norm, hgrn_w_in, hgrn_lb_logits, hgrn_out_gain, hgrn_w_out, moe_norm, moe_w_router, moe_w_gate, moe_w_up, moe_w_down):
    B, S, D = x.shape
    assert D == D_MODEL
    h = x.astype(F32).reshape(B * S, D)
    h = _layer0(h, B, S, attn_norm[0], attn_w_in[0], dil_q_gain[0], dil_k_gain[0], diff_q_gain[0],
                diff_k_gain[0], diff_lambda_q1[0], diff_lambda_k1[0], diff_lambda_q2[0],
                diff_lambda_k2[0], diff_out_gain[0], attn_w_out[0], ffn_norm[0], ffn_w_gate[0],
                ffn_w_up[0], ffn_w_down[0], 0)
    h = _layer1(h, B, S, hgrn_norm[0], hgrn_w_in[0], hgrn_lb_logits, hgrn_out_gain[0], hgrn_w_out[0],
                moe_norm[0], moe_w_router[0], moe_w_gate[0], moe_w_up[0], moe_w_down[0], 1)
    return h.reshape(B, S, D).astype(x.dtype)
```

```python
import functools
import math

import jax
import jax.numpy as jnp
from jax import lax
from jax.experimental import pallas as pl
from jax.experimental.pallas import tpu as pltpu

F32 = jnp.float32
BF16 = jnp.bfloat16
I32 = jnp.int32

D_MODEL = 1024
HEAD_DIM = 64
N_ATTN_HEADS = 16
N_DIL_HEADS = 12
N_DIFF_HEADS = 4
DIL_GROUPS = ((128, 1), (512, 4), (2048, 16))
DIFF_QK_DIM = 32
ATTN_IN_WIDTH = 3072
QBLOCK = 128
HGRN_HEADS = 8
HGRN_CHUNK = 64
D_FF = 2816
N_EXPERTS = 8
D_FF_EXPERT = 3584
EPS = 1e-6
NEG_INF = -1e30
LOG2E = 1.4426950408889634
LANES = 128
COLB = 256
VMEM_LIMIT = 56 * 1024 * 1024


def _cparams(*sem):
    return pltpu.CompilerParams(dimension_semantics=sem, vmem_limit_bytes=VMEM_LIMIT)


def _split_dot(x, m, terms):
    acc = None
    r = x
    for t in range(terms):
        part = r.astype(BF16)
        d = jnp.dot(part, m, preferred_element_type=F32)
        acc = d if acc is None else acc + d
        if t + 1 < terms:
            r = r - part.astype(F32)
    return acc


def _seg_matrix(n, seg):
    i = jnp.arange(n)
    return (i[:, None] // seg == i[None, :] // seg).astype(BF16)


def _rms_rows(x, gain_row):
    return x * lax.rsqrt(jnp.mean(x * x, axis=-1, keepdims=True) + EPS) * gain_row


def _sigmoid(x):
    return 1.0 / (1.0 + jnp.exp(-x))


def _silu(x):
    return x * _sigmoid(x)


_ATTN_NORM_SEG = (64, 64, 64, 64, 64, 64, 0, 0, 0, 32, 32, 0)
_QKV = 3 * COLB


def _attn_inproj_kernel(h_ref, ng_ref, w_ref, cg_ref, s64_ref, s32_ref,
                        c0_ref, c1_ref, c2_ref, df_ref, y_ref, *, tm):
    cls_refs = (c0_ref, c1_ref, c2_ref)
    xb = _rms_rows(h_ref[...], ng_ref[...]).astype(BF16)
    for c, seg in enumerate(_ATTN_NORM_SEG):
        y = jnp.dot(xb, w_ref[:, c * COLB:(c + 1) * COLB], preferred_element_type=F32)
        if seg:
            m = s64_ref[...] if seg == 64 else s32_ref[...]
            ms = _split_dot(y * y, m, 2) * (1.0 / seg)
            y = y * lax.rsqrt(ms + EPS) * cg_ref[c:c + 1, :]
        part, g = divmod(c, 3)
        if part == 3:
            df_ref[:, g * COLB:(g + 1) * COLB] = y.astype(BF16)
            continue
        d = DIL_GROUPS[g][1]
        if d == 1:
            cls_refs[g][:, part * COLB:(part + 1) * COLB] = y.astype(BF16)
            continue
        y_ref[0] = y[:, :LANES]
        y_ref[1] = y[:, LANES:]
        for r in range(d):
            col = r * _QKV + part * COLB
            rows = pl.ds(r, tm // d, stride=d)
            cls_refs[g][:, col:col + COLB] = jnp.concatenate(
                [y_ref[0, rows, :], y_ref[1, rows, :]], axis=1).astype(BF16)


def _attn_inproj(h2d, norm_gain, w_bf, col_gain, tm):
    T = h2d.shape[0]
    const = lambda i: (0, 0)
    row = lambda i: (i, 0)
    dils = [d for _, d in DIL_GROUPS]
    kern = functools.partial(_attn_inproj_kernel, tm=tm)
    return pl.pallas_call(
        kern,
        grid=(T // tm,),
        in_specs=[
            pl.BlockSpec((tm, D_MODEL), row),
            pl.BlockSpec((1, D_MODEL), const),
            pl.BlockSpec((D_MODEL, ATTN_IN_WIDTH), const),
            pl.BlockSpec((ATTN_IN_WIDTH // COLB, COLB), const),
            pl.BlockSpec((COLB, COLB), const),
            pl.BlockSpec((COLB, COLB), const),
        ],
        out_specs=[pl.BlockSpec((tm // d, d * _QKV), row) for d in dils] + [pl.BlockSpec((tm, _QKV), row)],
        out_shape=[jax.ShapeDtypeStruct((T // d, d * _QKV), BF16) for d in dils]
                  + [jax.ShapeDtypeStruct((T, _QKV), BF16)],
        scratch_shapes=[pltpu.VMEM((2, tm, LANES), F32)],
        compiler_params=_cparams("parallel"),
        name="attn_inproj",
    )(h2d, norm_gain, w_bf, col_gain, _seg_matrix(COLB, 64), _seg_matrix(COLB, 32))


def _dil_kernel(q_ref, kp_ref, kc_ref, vp_ref, vc_ref, o0_ref, o1_ref, l0_ref, l1_ref,
                *, slopes, dilation, ub):
    u = pl.program_id(1)
    r = pl.program_id(2)
    q = q_ref[...]
    kcat = jnp.concatenate([kp_ref[...], kc_ref[...]], axis=0)
    vcat = jnp.concatenate([vp_ref[...], vc_ref[...]], axis=0)
    qi = lax.broadcasted_iota(I32, (QBLOCK, 2 * QBLOCK), 0)
    kj = lax.broadcasted_iota(I32, (QBLOCK, 2 * QBLOCK), 1)
    step = qi + QBLOCK - kj
    in_window = (step >= 0) & (step <= QBLOCK)
    stepf = step.astype(F32)
    lane = lax.broadcasted_iota(I32, (QBLOCK, LANES), 1)
    low_half = lane < HEAD_DIM
    for i in range(ub // QBLOCK):
        valid = in_window
        if i == 0:
            valid = in_window & ((kj >= QBLOCK) | (u > 0))
        qb = q[i * QBLOCK:(i + 1) * QBLOCK]
        kb = kcat[i * QBLOCK:(i + 2) * QBLOCK]
        vb = vcat[i * QBLOCK:(i + 2) * QBLOCK]
        for pair in range(2):
            lanes = slice(pair * LANES, (pair + 1) * LANES)
            qp, kp, vp = qb[:, lanes], kb[:, lanes], vb[:, lanes]
            outs, lses = [], []
            for hh in range(2):
                slope = slopes[pair * 2 + hh]
                qm = jnp.where(low_half if hh == 0 else ~low_half, qp, jnp.zeros_like(qp))
                s = lax.dot_general(qm, kp, (((1,), (1,)), ((), ())), preferred_element_type=F32)
                s = s - (slope * dilation) * stepf
                s = jnp.where(valid, s, NEG_INF)
                m = jnp.max(s, axis=-1, keepdims=True)
                e = jnp.exp(s - m)
                l = jnp.sum(e, axis=-1, keepdims=True)
                pv = jnp.dot(e.astype(BF16), vp, preferred_element_type=F32)
                outs.append(pv / l)
                lses.append(jnp.broadcast_to(m + jnp.log(l), (QBLOCK, LANES)))
            if dilation == 1:
                rows = slice(i * QBLOCK, (i + 1) * QBLOCK)
            else:
                rows = pl.ds(r + i * QBLOCK * dilation, QBLOCK, stride=dilation)
            (o0_ref, o1_ref)[pair][rows, :] = jnp.where(low_half, outs[0], outs[1])
            (l0_ref, l1_ref)[pair][rows, :] = jnp.where(low_half, lses[0], lses[1])


def _dilated_group(cls, B, S, dilation, slopes):
    U = S // dilation
    ub = min(512, U)
    sub = ub // QBLOCK
    view = cls.reshape(B, U, dilation * _QKV)
    prev = lambda u: jnp.maximum(u * sub - 1, 0)
    kern = functools.partial(_dil_kernel, slopes=slopes, dilation=dilation, ub=ub)
    out_spec = pl.BlockSpec((None, ub * dilation, LANES), lambda b, u, r: (b, u, 0))
    res = pl.pallas_call(
        kern,
        grid=(B, U // ub, dilation),
        in_specs=[
            pl.BlockSpec((None, ub, COLB), lambda b, u, r: (b, u, 3 * r)),
            pl.BlockSpec((None, QBLOCK, COLB), lambda b, u, r: (b, prev(u), 3 * r + 1)),
            pl.BlockSpec((None, ub, COLB), lambda b, u, r: (b, u, 3 * r + 1)),
            pl.BlockSpec((None, QBLOCK, COLB), lambda b, u, r: (b, prev(u), 3 * r + 2)),
            pl.BlockSpec((None, ub, COLB), lambda b, u, r: (b, u, 3 * r + 2)),
        ],
        out_specs=[out_spec] * 4,
        out_shape=[jax.ShapeDtypeStruct((B, S, LANES), F32)] * 4,
        compiler_params=_cparams("parallel", "parallel", "arbitrary"),
        name=f"dilated_attn_d{dilation}",
    )(view, view, view, view, view)
    res = [a.reshape(B * S, LANES) for a in res]
    return res[:2], res[2:]


def _diff_kernel(lam_ref, q_ref, k_ref, v_ref, pos_ref, sl_ref, og_ref, s64_ref, o_ref,
                 m_ref, l_ref, acc_ref, qa_ref, *, tq):
    i = pl.program_id(1)
    q = q_ref[...]
    lane = lax.broadcasted_iota(I32, (tq, LANES), 1)
    low_half = lane < HEAD_DIM
    for pair in range(2):
        qp = q[:, pair * LANES:(pair + 1) * LANES]
        for hh in range(2):
            feat = jnp.broadcast_to(sl_ref[pair * 2 + hh:pair * 2 + hh + 1, :], (tq, LANES)).astype(BF16)
            for mu in range(2):
                lo = hh * HEAD_DIM + mu * DIFF_QK_DIM
                sel = (lane >= lo) & (lane < lo + DIFF_QK_DIM)
                r = hh * 2 + mu
                qa_ref[pair, r * tq:(r + 1) * tq, :] = jnp.concatenate(
                    [jnp.where(sel, qp, jnp.zeros_like(qp)), feat], axis=1)

    m_ref[...] = jnp.full(m_ref.shape, NEG_INF, F32)
    l_ref[...] = jnp.zeros(l_ref.shape, F32)
    acc_ref[...] = jnp.zeros(acc_ref.shape, F32)
    ones = jnp.ones((tq, LANES), BF16)

    def block(j, masked):
        ks = pl.multiple_of(j * tq, tq)
        kblk = k_ref[pl.ds(ks, tq), :]
        vblk = v_ref[pl.ds(ks, tq), :]
        pblk = pos_ref[pl.ds(ks, tq), :]
        if masked:
            row = lax.broadcasted_iota(I32, (tq, tq), 0)
            col = lax.broadcasted_iota(I32, (tq, tq), 1)
            causal = col <= row
        for pair in range(2):
            lanes = slice(pair * LANES, (pair + 1) * LANES)
            kaug = jnp.concatenate([kblk[:, lanes], pblk], axis=1)
            vaug = jnp.concatenate([vblk[:, lanes], ones], axis=1)
            s_all = lax.dot_general(qa_ref[pair], kaug, (((1,), (1,)), ((), ())),
                                    preferred_element_type=F32)
            ps, alphas = [], []
            for r in range(4):
                idx = pair * 4 + r
                s = s_all[r * tq:(r + 1) * tq]
                if masked:
                    s = jnp.where(causal, s, NEG_INF)
                m_prev = m_ref[idx]
                m_next = jnp.maximum(m_prev, jnp.max(s, axis=-1, keepdims=True))
                m_ref[idx] = m_next
                alphas.append(jnp.exp2(m_prev - m_next))
                ps.append(jnp.exp2(s - jnp.concatenate([m_next] * (tq // LANES), axis=1)).astype(BF16))
            pv = jnp.dot(jnp.concatenate(ps, axis=0), vaug, preferred_element_type=F32)
            for r in range(4):
                idx = pair * 4 + r
                part = pv[r * tq:(r + 1) * tq]
                acc_ref[idx] = alphas[r] * acc_ref[idx] + part[:, :LANES]
                l_ref[idx] = alphas[r] * l_ref[idx] + part[:, LANES:]

    def full_block(j, carry):
        block(j, False)
        return carry

    lax.fori_loop(0, i, full_block, 0)
    block(i, True)

    lam = lam_ref[0]
    for pair in range(2):
        o = None
        for mu in range(2):
            lo_idx, hi_idx = pair * 4 + mu, pair * 4 + 2 + mu
            term = jnp.where(low_half, acc_ref[lo_idx] / l_ref[lo_idx], acc_ref[hi_idx] / l_ref[hi_idx])
            o = term if mu == 0 else o - lam * term
        ms = _split_dot(o * o, s64_ref[...], 2) * (1.0 / HEAD_DIM)
        o = o * lax.rsqrt(ms + EPS) * og_ref[:, pair * LANES:(pair + 1) * LANES]
        o_ref[:, pair * LANES:(pair + 1) * LANES] = o.astype(BF16)


def _diff_attention(proj, lam, slope_feat, pos_feat, out_gain, tq):
    B, S, W = proj.shape
    kern = functools.partial(_diff_kernel, tq=tq)
    return pl.pallas_call(
        kern,
        grid=(B, S // tq),
        in_specs=[
            pl.BlockSpec(memory_space=pltpu.SMEM),
            pl.BlockSpec((None, tq, COLB), lambda b, i: (b, i, 0)),
            pl.BlockSpec((None, S, COLB), lambda b, i: (b, 0, 1)),
            pl.BlockSpec((None, S, COLB), lambda b, i: (b, 0, 2)),
            pl.BlockSpec((S, LANES), lambda b, i: (0, 0)),
            pl.BlockSpec((8, LANES), lambda b, i: (0, 0)),
            pl.BlockSpec((1, COLB), lambda b, i: (0, 0)),
            pl.BlockSpec((LANES, LANES), lambda b, i: (0, 0)),
        ],
        out_specs=pl.BlockSpec((None, tq, COLB), lambda b, i: (b, i, 0)),
        out_shape=jax.ShapeDtypeStruct((B, S, COLB), BF16),
        scratch_shapes=[
            pltpu.VMEM((8, tq, LANES), F32),
            pltpu.VMEM((8, tq, LANES), F32),
            pltpu.VMEM((8, tq, LANES), F32),
            pltpu.VMEM((2, 4 * tq, 2 * LANES), BF16),
        ],
        compiler_params=_cparams("parallel", "parallel"),
        name="diff_attn",
    )(lam, proj, proj, proj, pos_feat, slope_feat, out_gain, _seg_matrix(LANES, HEAD_DIM))


def _attn_out_kernel(h_ref, *refs):
    n = 2 * len(DIL_GROUPS)
    o_refs, l_refs, (d_ref, w_ref, out_ref) = refs[:n], refs[n:2 * n], refs[2 * n:]
    pieces = [None] * n
    for pair in range(2):
        ls = [l_refs[2 * g + pair][...] for g in range(len(DIL_GROUPS))]
        mx = functools.reduce(jnp.maximum, ls)
        es = [jnp.exp(l - mx) for l in ls]
        inv = 1.0 / functools.reduce(jnp.add, es)
        for g, e in enumerate(es):
            pieces[2 * g + pair] = (e * inv * o_refs[2 * g + pair][...]).astype(BF16)
    mixed = jnp.concatenate(pieces + [d_ref[...]], axis=1)
    out_ref[...] = h_ref[...] + jnp.dot(mixed, w_ref[...], preferred_element_type=F32)


def _attn_outproj(h2d, outs, lses, diff, w_bf, tm):
    T = h2d.shape[0]
    row = lambda i: (i, 0)
    half = pl.BlockSpec((tm, LANES), row)
    return pl.pallas_call(
        _attn_out_kernel,
        grid=(T // tm,),
        in_specs=[pl.BlockSpec((tm, D_MODEL), row)] + [half] * (len(outs) + len(lses))
                 + [pl.BlockSpec((tm, COLB), row), pl.BlockSpec((D_MODEL, D_MODEL), lambda i: (0, 0))],
        out_specs=pl.BlockSpec((tm, D_MODEL), row),
        out_shape=jax.ShapeDtypeStruct((T, D_MODEL), F32),
        compiler_params=_cparams("parallel"),
        name="attn_outproj",
    )(h2d, *outs, *lses, diff, w_bf)


def _ffn_kernel(h_ref, ng_ref, wg_ref, wu_ref, wd_ref, o_ref):
    x = h_ref[...]
    xb = _rms_rows(x, ng_ref[...]).astype(BF16)
    g = jnp.dot(xb, wg_ref[...], preferred_element_type=F32)
    u = jnp.dot(xb, wu_ref[...], preferred_element_type=F32)
    a = (_silu(g) * u).astype(BF16)
    o_ref[...] = x + jnp.dot(a, wd_ref[...], preferred_element_type=F32)


def _ffn(h2d, norm_gain, wg, wu, wd, tm):
    T = h2d.shape[0]
    const = lambda i: (0, 0)
    return pl.pallas_call(
        _ffn_kernel,
        grid=(T // tm,),
        in_specs=[
            pl.BlockSpec((tm, D_MODEL), lambda i: (i, 0)),
            pl.BlockSpec((1, D_MODEL), const),
            pl.BlockSpec((D_MODEL, D_FF), const),
            pl.BlockSpec((D_MODEL, D_FF), const),
            pl.BlockSpec((D_FF, D_MODEL), const),
        ],
        out_specs=pl.BlockSpec((tm, D_MODEL), lambda i: (i, 0)),
        out_shape=jax.ShapeDtypeStruct((T, D_MODEL), F32),
        compiler_params=_cparams("parallel"),
        name="ffn_swiglu",
    )(h2d, norm_gain, wg, wu, wd)


def _hgrn_inproj_kernel(h_ref, ng_ref, w_ref, lb_ref, q_ref, lf_ref, k_ref, v_ref, g_ref):
    xb = _rms_rows(h_ref[...], ng_ref[...]).astype(BF16)
    sec = lambda c: jnp.dot(xb, w_ref[:, c * D_MODEL:(c + 1) * D_MODEL], preferred_element_type=F32)
    q_ref[...] = _silu(sec(0)).astype(BF16)
    fl = sec(1)
    log_lb, log1m_lb, one_m_lb = lb_ref[0:1, :], lb_ref[1:2, :], lb_ref[2:3, :]
    log_sig = jnp.minimum(fl, 0.0) - jnp.log1p(jnp.exp(-jnp.abs(fl)))
    c = log1m_lb + log_sig
    lf_ref[...] = jnp.maximum(log_lb, c) + jnp.log1p(jnp.exp(-jnp.abs(log_lb - c)))
    k_ref[...] = (one_m_lb * _sigmoid(-fl)).astype(BF16)
    v_ref[...] = sec(2).astype(BF16)
    g_ref[...] = _sigmoid(sec(3)).astype(BF16)


def _hgrn_inproj(h2d, norm_gain, w_bf, lb_rows, tm):
    T = h2d.shape[0]
    const = lambda i: (0, 0)
    row = pl.BlockSpec((tm, D_MODEL), lambda i: (i, 0))
    bf = jax.ShapeDtypeStruct((T, D_MODEL), BF16)
    return pl.pallas_call(
        _hgrn_inproj_kernel,
        grid=(T // tm,),
        in_specs=[row, pl.BlockSpec((1, D_MODEL), const),
                  pl.BlockSpec((D_MODEL, 4 * D_MODEL), const),
                  pl.BlockSpec((8, D_MODEL), const)],
        out_specs=[row] * 5,
        out_shape=[bf, jax.ShapeDtypeStruct((T, D_MODEL), F32), bf, bf, bf],
        compiler_params=_cparams("parallel"),
        name="hgrn_inproj",
    )(h2d, norm_gain, w_bf, lb_rows)


def _hgrn_kernel(q_ref, lf_ref, k_ref, v_ref, g_ref, og_ref, o_ref, st_ref, *, rows):
    @pl.when(pl.program_id(1) == 0)
    def _():
        st_ref[...] = jnp.zeros(st_ref.shape, F32)

    C = HGRN_CHUNK
    ti = lax.broadcasted_iota(I32, (C, C), 0)
    si = lax.broadcasted_iota(I32, (C, C), 1)
    causal = si <= ti
    tri = jnp.where(causal, 1.0, 0.0).astype(BF16)
    for hd in range(HGRN_HEADS):
        lanes = slice(hd * LANES, (hd + 1) * LANES)
        for c in range(rows // C):
            rws = slice(c * C, (c + 1) * C)
            q = q_ref[rws, lanes].astype(F32)
            k = k_ref[rws, lanes].astype(F32)
            v = v_ref[rws, lanes]
            b = _split_dot_lhs(tri, lf_ref[rws, lanes])
            b_last = b[C - 1:C, :]
            b_mid = b[C // 2:C // 2 + 1, :]
            qs = (q * jnp.exp(b - b_mid)).astype(BF16)
            ks = (k * jnp.exp(b_mid - b)).astype(BF16)
            scores = lax.dot_general(qs, ks, (((1,), (1,)), ((), ())), preferred_element_type=F32)
            scores = jnp.where(causal, scores, 0.0).astype(BF16)
            o = jnp.dot(scores, v, preferred_element_type=F32)
            st = st_ref[hd]
            qi = (q * jnp.exp(b)).astype(BF16)
            o = o + lax.dot_general(qi, st.astype(BF16), (((1,), (1,)), ((), ())),
                                    preferred_element_type=F32)
            kst = (k * jnp.exp(b_last - b)).astype(BF16)
            st_ref[hd] = jnp.exp(b_last) * st + lax.dot_general(
                v, kst, (((0,), (0,)), ((), ())), preferred_element_type=F32)
            o = o * lax.rsqrt(jnp.mean(o * o, axis=-1, keepdims=True) + EPS) * og_ref[...]
            o_ref[rws, lanes] = (o * g_ref[rws, lanes].astype(F32)).astype(BF16)


def _split_dot_lhs(m, x):
    acc = None
    r = x
    for t in range(3):
        part = r.astype(BF16)
        d = jnp.dot(m, part, preferred_element_type=F32)
        acc = d if acc is None else acc + d
        if t < 2:
            r = r - part.astype(F32)
    return acc


def _hgrn_recurrence(q, lf, k, v, g, out_gain, B, S, rows):
    T = B * S
    nseq = S // rows
    row = pl.BlockSpec((rows, D_MODEL), lambda b, s: (b * nseq + s, 0))
    kern = functools.partial(_hgrn_kernel, rows=rows)
    return pl.pallas_call(
        kern,
        grid=(B, nseq),
        in_specs=[row] * 5 + [pl.BlockSpec((1, LANES), lambda b, s: (0, 0))],
        out_specs=row,
        out_shape=jax.ShapeDtypeStruct((T, D_MODEL), BF16),
        scratch_shapes=[pltpu.VMEM((HGRN_HEADS, LANES, LANES), F32)],
        compiler_params=_cparams("parallel", "arbitrary"),
        name="hgrn_recurrence",
    )(q, lf, k, v, g, out_gain)


def _proj_res_kernel(h_ref, x_ref, w_ref, o_ref):
    o_ref[...] = h_ref[...] + jnp.dot(x_ref[...], w_ref[...], preferred_element_type=F32)


def _proj_residual(h2d, x_bf, w_bf, tm):
    T = h2d.shape[0]
    row = pl.BlockSpec((tm, D_MODEL), lambda i: (i, 0))
    return pl.pallas_call(
        _proj_res_kernel,
        grid=(T // tm,),
        in_specs=[row, row, pl.BlockSpec((D_MODEL, D_MODEL), lambda i: (0, 0))],
        out_specs=row,
        out_shape=jax.ShapeDtypeStruct((T, D_MODEL), F32),
        compiler_params=_cparams("parallel"),
        name="proj_residual",
    )(h2d, x_bf, w_bf)


def _router_kernel(h_ref, ng_ref, whi_ref, wlo_ref, xn_ref, info_ref, cnt_ref, carry_ref, *, tm):
    @pl.when(pl.program_id(0) == 0)
    def _():
        carry_ref[...] = jnp.zeros(carry_ref.shape, F32)

    xn = _rms_rows(h_ref[...], ng_ref[...])
    xn_ref[...] = xn
    xhi = xn.astype(BF16)
    xlo = (xn - xhi.astype(F32)).astype(BF16)
    logits = (jnp.dot(xhi, whi_ref[...], preferred_element_type=F32)
              + jnp.dot(xhi, wlo_ref[...], preferred_element_type=F32)
              + jnp.dot(xlo, whi_ref[...], preferred_element_type=F32))
    lane = lax.broadcasted_iota(I32, (tm, LANES), 1)
    lanef = lane.astype(F32)
    logits = jnp.where(lane < N_EXPERTS, logits, -jnp.inf)
    m1 = jnp.max(logits, axis=-1, keepdims=True)
    i1 = jnp.min(jnp.where(logits == m1, lanef, float(LANES)), axis=-1, keepdims=True)
    oh1 = lanef == i1
    rest = jnp.where(oh1, -jnp.inf, logits)
    m2 = jnp.max(rest, axis=-1, keepdims=True)
    i2 = jnp.min(jnp.where(rest == m2, lanef, float(LANES)), axis=-1, keepdims=True)
    oh2 = lanef == i2
    e = jnp.exp(m2 - m1)
    w1 = 1.0 / (1.0 + e)
    w2 = e * w1

    chosen = jnp.where(oh1 | oh2, 1.0, 0.0)
    r = lax.broadcasted_iota(I32, (tm, tm), 0)
    c = lax.broadcasted_iota(I32, (tm, tm), 1)
    before = jnp.where(c < r, 1.0, 0.0).astype(BF16)
    excl = jnp.dot(before, chosen.astype(BF16), preferred_element_type=F32) + carry_ref[0:1, :]
    rank1 = jnp.sum(jnp.where(oh1, excl, 0.0), axis=-1, keepdims=True)
    rank2 = jnp.sum(jnp.where(oh2, excl, 0.0), axis=-1, keepdims=True)
    total = carry_ref[0:1, :] + jnp.sum(chosen, axis=0, keepdims=True)
    carry_ref[...] = jnp.broadcast_to(total, carry_ref.shape)
    cnt_ref[...] = jnp.broadcast_to(total, cnt_ref.shape)

    info = jnp.where(lane == 0, i1, 0.0)
    info = jnp.where(lane == 1, i2, info)
    info = jnp.where(lane == 2, rank1, info)
    info = jnp.where(lane == 3, rank2, info)
    info = jnp.where(lane == 4, w1, info)
    info = jnp.where(lane == 5, w2, info)
    info_ref[...] = info


def _router(h2d, norm_gain, w_hi, w_lo, tm):
    T = h2d.shape[0]
    const = lambda i: (0, 0)
    kern = functools.partial(_router_kernel, tm=tm)
    return pl.pallas_call(
        kern,
        grid=(T // tm,),
        in_specs=[
            pl.BlockSpec((tm, D_MODEL), lambda i: (i, 0)),
            pl.BlockSpec((1, D_MODEL), const),
            pl.BlockSpec((D_MODEL, LANES), const),
            pl.BlockSpec((D_MODEL, LANES), const),
        ],
        out_specs=[
            pl.BlockSpec((tm, D_MODEL), lambda i: (i, 0)),
            pl.BlockSpec((tm, LANES), lambda i: (i, 0)),
            pl.BlockSpec((8, LANES), const),
        ],
        out_shape=[
            jax.ShapeDtypeStruct((T, D_MODEL), F32),
            jax.ShapeDtypeStruct((T, LANES), F32),
            jax.ShapeDtypeStruct((8, LANES), F32),
        ],
        scratch_shapes=[pltpu.VMEM((8, LANES), F32)],
        compiler_params=_cparams("arbitrary"),
        name="moe_router",
    )(h2d, norm_gain, w_hi, w_lo)


def _dispatch_kernel(pos_ref, xn_ref, init_hbm, xs_hbm, sem, *, tt):
    del init_hbm

    def copy(t, s):
        return pltpu.make_async_copy(
            xn_ref.at[pl.ds(t, 1)], xs_hbm.at[pl.ds(pos_ref[0, 0, 2 * t + s], 1)], sem)

    def start(t, carry):
        copy(t, 0).start()
        copy(t, 1).start()
        return carry

    def wait(t, carry):
        copy(t, 0).wait()
        copy(t, 1).wait()
        return carry

    lax.fori_loop(0, tt, start, 0, unroll=8)
    lax.fori_loop(0, tt, wait, 0, unroll=8)


def _dispatch(xn, pos, n_rows, tt):
    T = xn.shape[0]
    pos3 = pos.reshape(T // tt, 1, 2 * tt)
    kern = functools.partial(_dispatch_kernel, tt=tt)
    return pl.pallas_call(
        kern,
        grid=(T // tt,),
        in_specs=[
            pl.BlockSpec((1, 1, 2 * tt), lambda i: (i, 0, 0), memory_space=pltpu.SMEM),
            pl.BlockSpec((tt, D_MODEL), lambda i: (i, 0)),
            pl.BlockSpec(memory_space=pl.ANY),
        ],
        out_specs=pl.BlockSpec(memory_space=pl.ANY),
        out_shape=jax.ShapeDtypeStruct((n_rows, D_MODEL), F32),
        scratch_shapes=[pltpu.SemaphoreType.DMA(())],
        input_output_aliases={2: 0},
        compiler_params=_cparams("arbitrary"),
        name="moe_dispatch",
    )(pos3, xn, jnp.zeros((n_rows, D_MODEL), F32))


def _expert_kernel(te_ref, na_ref, xs_ref, wg_ref, wu_ref, wd_ref, y_ref, xb_ref, acc_ref):
    del te_ref
    f = pl.program_id(1)
    active = pl.program_id(0) < na_ref[0]

    @pl.when(jnp.logical_not(active) & (f == pl.num_programs(1) - 1))
    def _():
        y_ref[...] = jnp.zeros(y_ref.shape, F32)

    @pl.when(active)
    def _():
        @pl.when(f == 0)
        def _():
            xb_ref[...] = xs_ref[...].astype(BF16)
            acc_ref[...] = jnp.zeros(acc_ref.shape, F32)

        xb = xb_ref[...]
        g = jnp.dot(xb, wg_ref[0], preferred_element_type=F32)
        u = jnp.dot(xb, wu_ref[0], preferred_element_type=F32)
        a = (_silu(g) * u).astype(BF16)
        acc_ref[...] += jnp.dot(a, wd_ref[0], preferred_element_type=F32)

        @pl.when(f == pl.num_programs(1) - 1)
        def _():
            y_ref[...] = acc_ref[...]


def _experts(xs, tile_expert, n_active, wg, wu, wd, tme, tf):
    n_rows = xs.shape[0]
    n_tiles = n_rows // tme
    nf = D_FF_EXPERT // tf

    def tile(i, na):
        return jnp.minimum(i, na[0] - 1)

    def fidx(i, f, na):
        return jnp.where(i < na[0], f, nf - 1)

    grid_spec = pltpu.PrefetchScalarGridSpec(
        num_scalar_prefetch=2,
        grid=(n_tiles, nf),
        in_specs=[
            pl.BlockSpec((tme, D_MODEL), lambda i, f, te, na: (tile(i, na), 0)),
            pl.BlockSpec((1, D_MODEL, tf), lambda i, f, te, na: (te[tile(i, na)], 0, fidx(i, f, na))),
            pl.BlockSpec((1, D_MODEL, tf), lambda i, f, te, na: (te[tile(i, na)], 0, fidx(i, f, na))),
            pl.BlockSpec((1, tf, D_MODEL), lambda i, f, te, na: (te[tile(i, na)], fidx(i, f, na), 0)),
        ],
        out_specs=pl.BlockSpec((tme, D_MODEL), lambda i, f, te, na: (i, 0)),
        scratch_shapes=[pltpu.VMEM((tme, D_MODEL), BF16), pltpu.VMEM((tme, D_MODEL), F32)],
    )
    return pl.pallas_call(
        _expert_kernel,
        grid_spec=grid_spec,
        out_shape=jax.ShapeDtypeStruct((n_rows, D_MODEL), F32),
        compiler_params=_cparams("arbitrary", "arbitrary"),
        name="moe_experts",
    )(tile_expert, n_active, xs, wg, wu, wd)


def _combine_kernel(pos_ref, info_ref, h_ref, y_hbm, o_ref, buf_ref, sem, *, tc):
    def copy(t, s):
        return pltpu.make_async_copy(
            y_hbm.at[pl.ds(pos_ref[0, 0, 2 * t + s], 1)], buf_ref.at[s, pl.ds(t, 1)], sem)

    def start(t, carry):
        copy(t, 0).start()
        copy(t, 1).start()
        return carry

    def wait(t, carry):
        copy(t, 0).wait()
        copy(t, 1).wait()
        return carry

    lax.fori_loop(0, tc, start, 0, unroll=8)
    lax.fori_loop(0, tc, wait, 0, unroll=8)
    info = info_ref[...]
    lane = lax.broadcasted_iota(I32, (tc, LANES), 1)
    w1 = jnp.sum(jnp.where(lane == 4, info, 0.0), axis=-1, keepdims=True)
    w2 = jnp.sum(jnp.where(lane == 5, info, 0.0), axis=-1, keepdims=True)
    o_ref[...] = h_ref[...] + (w1 * buf_ref[0] + w2 * buf_ref[1])


def _combine(h2d, info, pos, y, tc):
    T = h2d.shape[0]
    pos3 = pos.reshape(T // tc, 1, 2 * tc)
    kern = functools.partial(_combine_kernel, tc=tc)
    return pl.pallas_call(
        kern,
        grid=(T // tc,),
        in_specs=[
            pl.BlockSpec((1, 1, 2 * tc), lambda i: (i, 0, 0), memory_space=pltpu.SMEM),
            pl.BlockSpec((tc, LANES), lambda i: (i, 0)),
            pl.BlockSpec((tc, D_MODEL), lambda i: (i, 0)),
            pl.BlockSpec(memory_space=pl.ANY),
        ],
        out_specs=pl.BlockSpec((tc, D_MODEL), lambda i: (i, 0)),
        out_shape=jax.ShapeDtypeStruct((T, D_MODEL), F32),
        scratch_shapes=[pltpu.VMEM((2, tc, D_MODEL), F32), pltpu.SemaphoreType.DMA(())],
        compiler_params=_cparams("arbitrary"),
        name="moe_combine",
    )(pos3, info, h2d, y)


def _bf16_pieces(x, n):
    pieces, r = [], x.astype(F32)
    for _ in range(n):
        p = r.astype(BF16)
        pieces.append(p)
        r = r - p.astype(F32)
    return pieces


def _alibi_slopes():
    return [2.0 ** (-8.0 * (h + 1.0) / N_ATTN_HEADS) for h in range(N_ATTN_HEADS)]


def _attn_col_gain(dil_q_gain, dil_k_gain, diff_q_gain, diff_k_gain):
    rep = lambda g, n: jnp.tile(g.astype(F32), n)
    ones = jnp.ones((COLB,), F32)
    dq = rep(dil_q_gain, COLB // HEAD_DIM) * (HEAD_DIM ** -0.5)
    dk = rep(dil_k_gain, COLB // HEAD_DIM)
    fq = rep(diff_q_gain, COLB // DIFF_QK_DIM) * (DIFF_QK_DIM ** -0.5 * LOG2E)
    fk = rep(diff_k_gain, COLB // DIFF_QK_DIM)
    return jnp.stack([dq, dq, dq, dk, dk, dk, ones, ones, ones, fq, fk, ones])


def _layer0(h2d, B, S, attn_norm, w_in, dq_g, dk_g, fq_g, fk_g, lq1, lk1, lq2, lk2, out_g, w_out,
            ffn_norm, w_gate, w_up, w_down, layer):
    T = B * S
    tm = min(512, T)
    slopes = _alibi_slopes()
    *cls, dproj = _attn_inproj(h2d, attn_norm.reshape(1, D_MODEL).astype(F32), w_in.astype(BF16),
                               _attn_col_gain(dq_g, dk_g, fq_g, fk_g), tm)
    outs, lses = [], []
    for g, (window, dilation) in enumerate(DIL_GROUPS):
        assert window // dilation == QBLOCK and S % (dilation * QBLOCK) == 0 and tm % dilation == 0
        o, lse = _dilated_group(cls[g], B, S, dilation, tuple(slopes[4 * g:4 * g + 4]))
        outs.extend(o)
        lses.extend(lse)

    lam_init = 0.8 - 0.6 * math.exp(-0.3 * layer)
    lam = (jnp.exp(jnp.sum(lq1.astype(F32) * lk1.astype(F32)))
           - jnp.exp(jnp.sum(lq2.astype(F32) * lk2.astype(F32))) + lam_init).reshape(1)
    sl = jnp.asarray(slopes[N_DIL_HEADS:], F32) * LOG2E
    s_hi, s_lo = _bf16_pieces(sl, 2)
    slope_feat = jnp.zeros((8, LANES), F32)
    slope_feat = slope_feat.at[:N_DIFF_HEADS, 0].set(s_hi.astype(F32)).at[:N_DIFF_HEADS, 1].set(s_hi.astype(F32))
    slope_feat = slope_feat.at[:N_DIFF_HEADS, 2].set(s_lo.astype(F32)).at[:N_DIFF_HEADS, 3].set(s_lo.astype(F32))
    p_hi, p_lo = _bf16_pieces(jnp.arange(S, dtype=F32), 2)
    pos_feat = jnp.zeros((S, LANES), BF16)
    pos_feat = pos_feat.at[:, 0].set(p_hi).at[:, 1].set(p_lo).at[:, 2].set(p_hi).at[:, 3].set(p_lo)
    out_gain = (jnp.tile(out_g.astype(F32), COLB // HEAD_DIM) * (1.0 - lam_init)).reshape(1, COLB)
    diff = _diff_attention(dproj.reshape(B, S, _QKV), lam, slope_feat, pos_feat, out_gain, min(512, S)).reshape(T, COLB)

    h2d = _attn_outproj(h2d, outs, lses, diff, w_out.astype(BF16), tm)
    return _ffn(h2d, ffn_norm.reshape(1, D_MODEL).astype(F32), w_gate.astype(BF16),
                w_up.astype(BF16), w_down.astype(BF16), min(256, T))


def _layer1(h2d, B, S, hgrn_norm, w_in, lb_logits, out_gain, w_out,
            moe_norm, w_router, w_gate, w_up, w_down, layer):
    T = B * S
    tm = min(512, T)
    sm = jax.nn.softmax(lb_logits.astype(F32), axis=0)
    lb = (jnp.cumsum(sm, axis=0) - sm[0])[layer]
    lb_rows = jnp.zeros((8, D_MODEL), F32).at[0].set(jnp.log(lb)).at[1].set(jnp.log1p(-lb)).at[2].set(1.0 - lb)
    q, lf, k, v, g = _hgrn_inproj(h2d, hgrn_norm.reshape(1, D_MODEL).astype(F32), w_in.astype(BF16),
                                  lb_rows, tm)
    o = _hgrn_recurrence(q, lf, k, v, g, out_gain.reshape(1, LANES).astype(F32), B, S, min(128, S))
    h2d = _proj_residual(h2d, o, w_out.astype(BF16), tm)
    return _moe_block(h2d, moe_norm, w_router, w_gate, w_up, w_down)


def _moe_block(h2d, moe_norm, w_router, w_gate, w_up, w_down):
    T = h2d.shape[0]
    tme = 512
    w_pad = jnp.zeros((D_MODEL, LANES), F32).at[:, :N_EXPERTS].set(w_router.astype(F32))
    w_hi, w_lo = _bf16_pieces(w_pad, 2)
    xn, info, counts = _router(h2d, moe_norm.reshape(1, D_MODEL).astype(F32), w_hi, w_lo, min(256, T))
    cnt = counts[0, :N_EXPERTS].astype(I32)
    padded = ((cnt + tme - 1) // tme) * tme
    ends = jnp.cumsum(padded)
    starts = ends - padded
    experts = info[:, 0:2].astype(I32)
    ranks = info[:, 2:4].astype(I32)
    pos = (starts[experts] + ranks).reshape(-1)
    n_rows = 2 * T + N_EXPERTS * tme
    n_tiles = n_rows // tme
    tile_start = jnp.arange(n_tiles, dtype=I32) * tme
    tile_expert = jnp.minimum(jnp.sum((ends[None, :] <= tile_start[:, None]).astype(I32), axis=1), N_EXPERTS - 1)
    n_active = (ends[-1] // tme).astype(I32).reshape(1)
    xs = _dispatch(xn, pos, n_rows, min(256, T))
    y = _experts(xs, tile_expert, n_active, w_gate.astype(BF16), w_up.astype(BF16), w_down.astype(BF16),
                 tme, 1792)
    return _combine(h2d, info, pos, y, min(256, T))


def kernel(x, attn_norm, attn_w_in, dil_q_gain, dil_k_gain, diff_q_gain, diff_k_gain, diff_lambda_q1, diff_lambda_k1, diff_lambda_q2, diff_lambda_k2, diff_out_gain, attn_w_out, ffn_norm, ffn_w_gate, ffn_w_up, ffn_w_down, hgrn_norm, hgrn_w_in, hgrn_lb_logits, hgrn_out_gain, hgrn_w_out, moe_norm, moe_w_router, moe_w_gate, moe_w_up, moe_w_down):
    B, S, D = x.shape
    assert D == D_MODEL
    h = x.astype(F32).reshape(B * S, D)
    h = _layer0(h, B, S, attn_norm[0], attn_w_in[0], dil_q_gain[0], dil_k_gain[0], diff_q_gain[0],
                diff_k_gain[0], diff_lambda_q1[0], diff_lambda_k1[0], diff_lambda_q2[0],
                diff_lambda_k2[0], diff_out_gain[0], attn_w_out[0], ffn_norm[0], ffn_w_gate[0],
                ffn_w_up[0], ffn_w_down[0], 0)
    h = _layer1(h, B, S, hgrn_norm[0], hgrn_w_in[0], hgrn_lb_logits, hgrn_out_gain[0], hgrn_w_out[0],
                moe_norm[0], moe_w_router[0], moe_w_gate[0], moe_w_up[0], moe_w_down[0], 1)
    return h.reshape(B, S, D).astype(x.dtype)
```

```python
import functools
import math

import jax
import jax.numpy as jnp
from jax import lax
from jax.experimental import pallas as pl
from jax.experimental.pallas import tpu as pltpu

F32 = jnp.float32
BF16 = jnp.bfloat16
I32 = jnp.int32

D_MODEL = 1024
HEAD_DIM = 64
N_ATTN_HEADS = 16
N_DIL_HEADS = 12
N_DIFF_HEADS = 4
DIL_GROUPS = ((128, 1), (512, 4), (2048, 16))
DIFF_QK_DIM = 32
ATTN_IN_WIDTH = 3072
QBLOCK = 128
HGRN_HEADS = 8
HGRN_CHUNK = 64
D_FF = 2816
N_EXPERTS = 8
D_FF_EXPERT = 3584
EPS = 1e-6
NEG_INF = -1e30
LOG2E = 1.4426950408889634
LANES = 128
COLB = 256
VMEM_LIMIT = 56 * 1024 * 1024


def _cparams(*sem):
    return pltpu.CompilerParams(dimension_semantics=sem, vmem_limit_bytes=VMEM_LIMIT)


def _split_dot(x, m, terms):
    acc = None
    r = x
    for t in range(terms):
        part = r.astype(BF16)
        d = jnp.dot(part, m, preferred_element_type=F32)
        acc = d if acc is None else acc + d
        if t + 1 < terms:
            r = r - part.astype(F32)
    return acc


def _seg_matrix(n, seg):
    i = jnp.arange(n)
    return (i[:, None] // seg == i[None, :] // seg).astype(BF16)


def _rms_rows(x, gain_row):
    return x * lax.rsqrt(jnp.mean(x * x, axis=-1, keepdims=True) + EPS) * gain_row


def _sigmoid(x):
    return 1.0 / (1.0 + jnp.exp(-x))


def _silu(x):
    return x * _sigmoid(x)


_ATTN_NORM_SEG = (64, 64, 64, 64, 64, 64, 0, 0, 0, 32, 32, 0)
_QKV = 3 * COLB


def _attn_inproj_kernel(h_ref, ng_ref, w_ref, cg_ref, s64_ref, s32_ref,
                        c0_ref, c1_ref, c2_ref, df_ref, y_ref, *, tm):
    cls_refs = (c0_ref, c1_ref, c2_ref)
    xb = _rms_rows(h_ref[...], ng_ref[...]).astype(BF16)
    for c, seg in enumerate(_ATTN_NORM_SEG):
        y = jnp.dot(xb, w_ref[:, c * COLB:(c + 1) * COLB], preferred_element_type=F32)
        if seg:
            m = s64_ref[...] if seg == 64 else s32_ref[...]
            ms = _split_dot(y * y, m, 2) * (1.0 / seg)
            y = y * lax.rsqrt(ms + EPS) * cg_ref[c:c + 1, :]
        part, g = divmod(c, 3)
        if part == 3:
            df_ref[:, g * COLB:(g + 1) * COLB] = y.astype(BF16)
            continue
        d = DIL_GROUPS[g][1]
        if d == 1:
            cls_refs[g][:, part * COLB:(part + 1) * COLB] = y.astype(BF16)
            continue
        y_ref[0] = y[:, :LANES]
        y_ref[1] = y[:, LANES:]
        for r in range(d):
            col = r * _QKV + part * COLB
            rows = pl.ds(r, tm // d, stride=d)
            cls_refs[g][:, col:col + COLB] = jnp.concatenate(
                [y_ref[0, rows, :], y_ref[1, rows, :]], axis=1).astype(BF16)


def _attn_inproj(h2d, norm_gain, w_bf, col_gain, tm):
    T = h2d.shape[0]
    const = lambda i: (0, 0)
    row = lambda i: (i, 0)
    dils = [d for _, d in DIL_GROUPS]
    kern = functools.partial(_attn_inproj_kernel, tm=tm)
    return pl.pallas_call(
        kern,
        grid=(T // tm,),
        in_specs=[
            pl.BlockSpec((tm, D_MODEL), row),
            pl.BlockSpec((1, D_MODEL), const),
            pl.BlockSpec((D_MODEL, ATTN_IN_WIDTH), const),
            pl.BlockSpec((ATTN_IN_WIDTH // COLB, COLB), const),
            pl.BlockSpec((COLB, COLB), const),
            pl.BlockSpec((COLB, COLB), const),
        ],
        out_specs=[pl.BlockSpec((tm // d, d * _QKV), row) for d in dils] + [pl.BlockSpec((tm, _QKV), row)],
        out_shape=[jax.ShapeDtypeStruct((T // d, d * _QKV), BF16) for d in dils]
                  + [jax.ShapeDtypeStruct((T, _QKV), BF16)],
        scratch_shapes=[pltpu.VMEM((2, tm, LANES), F32)],
        compiler_params=_cparams("parallel"),
        name="attn_inproj",
    )(h2d, norm_gain, w_bf, col_gain, _seg_matrix(COLB, 64), _seg_matrix(COLB, 32))


def _dil_kernel(q_ref, kp_ref, kc_ref, vp_ref, vc_ref, o0_ref, o1_ref, l0_ref, l1_ref,
                *, slopes, dilation, ub):
    u = pl.program_id(1)
    r = pl.program_id(2)
    q = q_ref[...]
    kcat = jnp.concatenate([kp_ref[...], kc_ref[...]], axis=0)
    vcat = jnp.concatenate([vp_ref[...], vc_ref[...]], axis=0)
    qi = lax.broadcasted_iota(I32, (QBLOCK, 2 * QBLOCK), 0)
    kj = lax.broadcasted_iota(I32, (QBLOCK, 2 * QBLOCK), 1)
    step = qi + QBLOCK - kj
    in_window = (step >= 0) & (step <= QBLOCK)
    stepf = step.astype(F32)
    lane = lax.broadcasted_iota(I32, (QBLOCK, LANES), 1)
    low_half = lane < HEAD_DIM
    for i in range(ub // QBLOCK):
        valid = in_window
        if i == 0:
            valid = in_window & ((kj >= QBLOCK) | (u > 0))
        qb = q[i * QBLOCK:(i + 1) * QBLOCK]
        kb = kcat[i * QBLOCK:(i + 2) * QBLOCK]
        vb = vcat[i * QBLOCK:(i + 2) * QBLOCK]
        for pair in range(2):
            lanes = slice(pair * LANES, (pair + 1) * LANES)
            qp, kp, vp = qb[:, lanes], kb[:, lanes], vb[:, lanes]
            outs, lses = [], []
            for hh in range(2):
                slope = slopes[pair * 2 + hh]
                qm = jnp.where(low_half if hh == 0 else ~low_half, qp, jnp.zeros_like(qp))
                s = lax.dot_general(qm, kp, (((1,), (1,)), ((), ())), preferred_element_type=F32)
                s = s - (slope * dilation) * stepf
                s = jnp.where(valid, s, NEG_INF)
                m = jnp.max(s, axis=-1, keepdims=True)
                e = jnp.exp(s - m)
                l = jnp.sum(e, axis=-1, keepdims=True)
                pv = jnp.dot(e.astype(BF16), vp, preferred_element_type=F32)
                outs.append(pv / l)
                lses.append(jnp.broadcast_to(m + jnp.log(l), (QBLOCK, LANES)))
            if dilation == 1:
                rows = slice(i * QBLOCK, (i + 1) * QBLOCK)
            else:
                rows = pl.ds(r + i * QBLOCK * dilation, QBLOCK, stride=dilation)
            (o0_ref, o1_ref)[pair][rows, :] = jnp.where(low_half, outs[0], outs[1])
            (l0_ref, l1_ref)[pair][rows, :] = jnp.where(low_half, lses[0], lses[1])


def _dilated_group(cls, B, S, dilation, slopes):
    U = S // dilation
    ub = min(512, U)
    sub = ub // QBLOCK
    view = cls.reshape(B, U, dilation * _QKV)
    prev = lambda u: jnp.maximum(u * sub - 1, 0)
    kern = functools.partial(_dil_kernel, slopes=slopes, dilation=dilation, ub=ub)
    out_spec = pl.BlockSpec((None, ub * dilation, LANES), lambda b, u, r: (b, u, 0))
    res = pl.pallas_call(
        kern,
        grid=(B, U // ub, dilation),
        in_specs=[
            pl.BlockSpec((None, ub, COLB), lambda b, u, r: (b, u, 3 * r)),
            pl.BlockSpec((None, QBLOCK, COLB), lambda b, u, r: (b, prev(u), 3 * r + 1)),
            pl.BlockSpec((None, ub, COLB), lambda b, u, r: (b, u, 3 * r + 1)),
            pl.BlockSpec((None, QBLOCK, COLB), lambda b, u, r: (b, prev(u), 3 * r + 2)),
            pl.BlockSpec((None, ub, COLB), lambda b, u, r: (b, u, 3 * r + 2)),
        ],
        out_specs=[out_spec] * 4,
        out_shape=[jax.ShapeDtypeStruct((B, S, LANES), F32)] * 4,
        compiler_params=_cparams("parallel", "parallel", "arbitrary"),
        name=f"dilated_attn_d{dilation}",
    )(view, view, view, view, view)
    res = [a.reshape(B * S, LANES) for a in res]
    return res[:2], res[2:]


def _diff_kernel(lam_ref, q_ref, k_ref, v_ref, pos_ref, sl_ref, og_ref, s64_ref, o_ref,
                 m_ref, l_ref, acc_ref, qa_ref, *, tq):
    i = pl.program_id(1)
    q = q_ref[...]
    lane = lax.broadcasted_iota(I32, (tq, LANES), 1)
    low_half = lane < HEAD_DIM
    for pair in range(2):
        qp = q[:, pair * LANES:(pair + 1) * LANES]
        for hh in range(2):
            feat = jnp.broadcast_to(sl_ref[pair * 2 + hh:pair * 2 + hh + 1, :], (tq, LANES)).astype(BF16)
            for mu in range(2):
                lo = hh * HEAD_DIM + mu * DIFF_QK_DIM
                sel = (lane >= lo) & (lane < lo + DIFF_QK_DIM)
                r = hh * 2 + mu
                qa_ref[pair, r * tq:(r + 1) * tq, :] = jnp.concatenate(
                    [jnp.where(sel, qp, jnp.zeros_like(qp)), feat], axis=1)

    m_ref[...] = jnp.full(m_ref.shape, NEG_INF, F32)
    l_ref[...] = jnp.zeros(l_ref.shape, F32)
    acc_ref[...] = jnp.zeros(acc_ref.shape, F32)
    ones = jnp.ones((tq, LANES), BF16)

    def scores(j):
        ks = pl.multiple_of(j * tq, tq)
        kblk = k_ref[pl.ds(ks, tq), :]
        pblk = pos_ref[pl.ds(ks, tq), :]
        s_alls = []
        for pair in range(2):
            kaug = jnp.concatenate([kblk[:, pair * LANES:(pair + 1) * LANES], pblk], axis=1)
            s_alls.append(lax.dot_general(qa_ref[pair], kaug, (((1,), (1,)), ((), ())),
                                          preferred_element_type=F32))
        return s_alls, v_ref[pl.ds(ks, tq), :]

    def finish(s_alls, vblk, masked):
        if masked:
            row = lax.broadcasted_iota(I32, (tq, tq), 0)
            col = lax.broadcasted_iota(I32, (tq, tq), 1)
            causal = col <= row
        probs, alphas = [], []
        for pair in range(2):
            ps = []
            for r in range(4):
                idx = pair * 4 + r
                s = s_alls[pair][r * tq:(r + 1) * tq]
                if masked:
                    s = jnp.where(causal, s, NEG_INF)
                m_prev = m_ref[idx]
                m_next = jnp.maximum(m_prev, jnp.max(s, axis=-1, keepdims=True))
                m_ref[idx] = m_next
                alphas.append(jnp.exp2(m_prev - m_next))
                ps.append(jnp.exp2(s - jnp.concatenate([m_next] * (tq // LANES), axis=1)).astype(BF16))
            probs.append(jnp.concatenate(ps, axis=0))
        for pair in range(2):
            vaug = jnp.concatenate([vblk[:, pair * LANES:(pair + 1) * LANES], ones], axis=1)
            pv = jnp.dot(probs[pair], vaug, preferred_element_type=F32)
            for r in range(4):
                idx = pair * 4 + r
                part = pv[r * tq:(r + 1) * tq]
                acc_ref[idx] = alphas[idx] * acc_ref[idx] + part[:, :LANES]
                l_ref[idx] = alphas[idx] * l_ref[idx] + part[:, LANES:]

    def two_full_blocks(jj, carry):
        first, second = scores(2 * jj), scores(2 * jj + 1)
        finish(*first, False)
        finish(*second, False)
        return carry

    lax.fori_loop(0, i // 2, two_full_blocks, 0)

    @pl.when(i % 2 == 1)
    def _():
        finish(*scores(i - 1), False)

    finish(*scores(i), True)

    lam = lam_ref[0]
    for pair in range(2):
        o = None
        for mu in range(2):
            lo_idx, hi_idx = pair * 4 + mu, pair * 4 + 2 + mu
            term = jnp.where(low_half, acc_ref[lo_idx] / l_ref[lo_idx], acc_ref[hi_idx] / l_ref[hi_idx])
            o = term if mu == 0 else o - lam * term
        ms = _split_dot(o * o, s64_ref[...], 2) * (1.0 / HEAD_DIM)
        o = o * lax.rsqrt(ms + EPS) * og_ref[:, pair * LANES:(pair + 1) * LANES]
        o_ref[:, pair * LANES:(pair + 1) * LANES] = o.astype(BF16)


def _diff_attention(proj, lam, slope_feat, pos_feat, out_gain, tq):
    B, S, W = proj.shape
    kern = functools.partial(_diff_kernel, tq=tq)
    return pl.pallas_call(
        kern,
        grid=(B, S // tq),
        in_specs=[
            pl.BlockSpec(memory_space=pltpu.SMEM),
            pl.BlockSpec((None, tq, COLB), lambda b, i: (b, i, 0)),
            pl.BlockSpec((None, S, COLB), lambda b, i: (b, 0, 1), pipeline_mode=pl.Buffered(1)),
            pl.BlockSpec((None, S, COLB), lambda b, i: (b, 0, 2), pipeline_mode=pl.Buffered(1)),
            pl.BlockSpec((S, LANES), lambda b, i: (0, 0), pipeline_mode=pl.Buffered(1)),
            pl.BlockSpec((8, LANES), lambda b, i: (0, 0)),
            pl.BlockSpec((1, COLB), lambda b, i: (0, 0)),
            pl.BlockSpec((LANES, LANES), lambda b, i: (0, 0)),
        ],
        out_specs=pl.BlockSpec((None, tq, COLB), lambda b, i: (b, i, 0)),
        out_shape=jax.ShapeDtypeStruct((B, S, COLB), BF16),
        scratch_shapes=[
            pltpu.VMEM((8, tq, LANES), F32),
            pltpu.VMEM((8, tq, LANES), F32),
            pltpu.VMEM((8, tq, LANES), F32),
            pltpu.VMEM((2, 4 * tq, 2 * LANES), BF16),
        ],
        compiler_params=_cparams("parallel", "parallel"),
        name="diff_attn",
    )(lam, proj, proj, proj, pos_feat, slope_feat, out_gain, _seg_matrix(LANES, HEAD_DIM))


def _attn_out_kernel(h_ref, *refs):
    n = 2 * len(DIL_GROUPS)
    o_refs, l_refs, (d_ref, w_ref, out_ref) = refs[:n], refs[n:2 * n], refs[2 * n:]
    pieces = [None] * n
    for pair in range(2):
        ls = [l_refs[2 * g + pair][...] for g in range(len(DIL_GROUPS))]
        mx = functools.reduce(jnp.maximum, ls)
        es = [jnp.exp(l - mx) for l in ls]
        inv = 1.0 / functools.reduce(jnp.add, es)
        for g, e in enumerate(es):
            pieces[2 * g + pair] = (e * inv * o_refs[2 * g + pair][...]).astype(BF16)
    mixed = jnp.concatenate(pieces + [d_ref[...]], axis=1)
    out_ref[...] = h_ref[...] + jnp.dot(mixed, w_ref[...], preferred_element_type=F32)


def _attn_outproj(h2d, outs, lses, diff, w_bf, tm):
    T = h2d.shape[0]
    row = lambda i: (i, 0)
    half = pl.BlockSpec((tm, LANES), row)
    return pl.pallas_call(
        _attn_out_kernel,
        grid=(T // tm,),
        in_specs=[pl.BlockSpec((tm, D_MODEL), row)] + [half] * (len(outs) + len(lses))
                 + [pl.BlockSpec((tm, COLB), row), pl.BlockSpec((D_MODEL, D_MODEL), lambda i: (0, 0))],
        out_specs=pl.BlockSpec((tm, D_MODEL), row),
        out_shape=jax.ShapeDtypeStruct((T, D_MODEL), F32),
        compiler_params=_cparams("parallel"),
        name="attn_outproj",
    )(h2d, *outs, *lses, diff, w_bf)


def _ffn_kernel(h_ref, ng_ref, wg_ref, wu_ref, wd_ref, o_ref):
    x = h_ref[...]
    xb = _rms_rows(x, ng_ref[...]).astype(BF16)
    g = jnp.dot(xb, wg_ref[...], preferred_element_type=F32)
    u = jnp.dot(xb, wu_ref[...], preferred_element_type=F32)
    a = (_silu(g) * u).astype(BF16)
    o_ref[...] = x + jnp.dot(a, wd_ref[...], preferred_element_type=F32)


def _ffn(h2d, norm_gain, wg, wu, wd, tm):
    T = h2d.shape[0]
    const = lambda i: (0, 0)
    return pl.pallas_call(
        _ffn_kernel,
        grid=(T // tm,),
        in_specs=[
            pl.BlockSpec((tm, D_MODEL), lambda i: (i, 0)),
            pl.BlockSpec((1, D_MODEL), const),
            pl.BlockSpec((D_MODEL, D_FF), const),
            pl.BlockSpec((D_MODEL, D_FF), const),
            pl.BlockSpec((D_FF, D_MODEL), const),
        ],
        out_specs=pl.BlockSpec((tm, D_MODEL), lambda i: (i, 0)),
        out_shape=jax.ShapeDtypeStruct((T, D_MODEL), F32),
        compiler_params=_cparams("parallel"),
        name="ffn_swiglu",
    )(h2d, norm_gain, wg, wu, wd)


def _hgrn_inproj_kernel(h_ref, ng_ref, w_ref, lb_ref, q_ref, lf_ref, k_ref, v_ref, g_ref):
    xb = _rms_rows(h_ref[...], ng_ref[...]).astype(BF16)
    sec = lambda c: jnp.dot(xb, w_ref[:, c * D_MODEL:(c + 1) * D_MODEL], preferred_element_type=F32)
    q_ref[...] = _silu(sec(0)).astype(BF16)
    fl = sec(1)
    log_lb, log1m_lb, one_m_lb = lb_ref[0:1, :], lb_ref[1:2, :], lb_ref[2:3, :]
    t = jnp.exp(-jnp.abs(fl))
    r = 1.0 / (1.0 + t)
    c = log1m_lb + (jnp.minimum(fl, 0.0) + jnp.log(r))
    lf_ref[...] = jnp.maximum(log_lb, c) + jnp.log(1.0 + jnp.exp(-jnp.abs(log_lb - c)))
    k_ref[...] = (one_m_lb * jnp.where(fl >= 0.0, t * r, r)).astype(BF16)
    v_ref[...] = sec(2).astype(BF16)
    g_ref[...] = _sigmoid(sec(3)).astype(BF16)


def _hgrn_inproj(h2d, norm_gain, w_bf, lb_rows, tm):
    T = h2d.shape[0]
    const = lambda i: (0, 0)
    row = pl.BlockSpec((tm, D_MODEL), lambda i: (i, 0))
    bf = jax.ShapeDtypeStruct((T, D_MODEL), BF16)
    return pl.pallas_call(
        _hgrn_inproj_kernel,
        grid=(T // tm,),
        in_specs=[row, pl.BlockSpec((1, D_MODEL), const),
                  pl.BlockSpec((D_MODEL, 4 * D_MODEL), const),
                  pl.BlockSpec((8, D_MODEL), const)],
        out_specs=[row] * 5,
        out_shape=[bf, jax.ShapeDtypeStruct((T, D_MODEL), F32), bf, bf, bf],
        compiler_params=_cparams("parallel"),
        name="hgrn_inproj",
    )(h2d, norm_gain, w_bf, lb_rows)


def _hgrn_kernel(q_ref, lf_ref, k_ref, v_ref, g_ref, og_ref, o_ref, st_ref, *, rows):
    @pl.when(pl.program_id(1) == 0)
    def _():
        st_ref[...] = jnp.zeros(st_ref.shape, F32)

    C = HGRN_CHUNK
    nc = rows // C
    ti = lax.broadcasted_iota(I32, (rows, rows), 0)
    si = lax.broadcasted_iota(I32, (rows, rows), 1)
    causal = (si <= ti) & (si >= (ti // C) * C)
    tri = jnp.where(causal, 1.0, 0.0).astype(BF16)
    width = lf_ref.shape[1]
    b = _split_dot_lhs(tri, lf_ref[...])
    chunk_row = lambda i: jnp.concatenate(
        [jnp.broadcast_to(b[c * C + i:c * C + i + 1, :], (C, width)) for c in range(nc)], axis=0)
    b_mid = chunk_row(C // 2)
    b_last = chunk_row(C - 1)
    q = q_ref[...].astype(F32)
    k = k_ref[...].astype(F32)
    v = v_ref[...]
    qs = (q * jnp.exp(b - b_mid)).astype(BF16)
    ks = (k * jnp.exp(b_mid - b)).astype(BF16)
    qi = (q * jnp.exp(b)).astype(BF16)
    kst = (k * jnp.exp(b_last - b)).astype(BF16)
    dec = jnp.exp(b_last)
    nt = (((1,), (1,)), ((), ()))
    for hd in range(HGRN_HEADS):
        lanes = slice(hd * LANES, (hd + 1) * LANES)
        scores = lax.dot_general(qs[:, lanes], ks[:, lanes], nt, preferred_element_type=F32)
        scores = jnp.where(causal, scores, 0.0).astype(BF16)
        o = jnp.dot(scores, v[:, lanes], preferred_element_type=F32)
        st = st_ref[hd]
        inter = []
        for c in range(nc):
            rws = slice(c * C, (c + 1) * C)
            inter.append(lax.dot_general(qi[rws, lanes], st.astype(BF16), nt, preferred_element_type=F32))
            st = dec[c * C:c * C + 1, lanes] * st + lax.dot_general(
                v[rws, lanes], kst[rws, lanes], (((0,), (0,)), ((), ())), preferred_element_type=F32)
        st_ref[hd] = st
        o = o + jnp.concatenate(inter, axis=0)
        o = o * lax.rsqrt(jnp.mean(o * o, axis=-1, keepdims=True) + EPS) * og_ref[...]
        o_ref[:, lanes] = (o * g_ref[:, lanes].astype(F32)).astype(BF16)


def _split_dot_lhs(m, x):
    acc = None
    r = x
    for t in range(3):
        part = r.astype(BF16)
        d = jnp.dot(m, part, preferred_element_type=F32)
        acc = d if acc is None else acc + d
        if t < 2:
            r = r - part.astype(F32)
    return acc


def _hgrn_recurrence(q, lf, k, v, g, out_gain, B, S, rows):
    T = B * S
    nseq = S // rows
    row = pl.BlockSpec((rows, D_MODEL), lambda b, s: (b * nseq + s, 0))
    kern = functools.partial(_hgrn_kernel, rows=rows)
    return pl.pallas_call(
        kern,
        grid=(B, nseq),
        in_specs=[row] * 5 + [pl.BlockSpec((1, LANES), lambda b, s: (0, 0))],
        out_specs=row,
        out_shape=jax.ShapeDtypeStruct((T, D_MODEL), BF16),
        scratch_shapes=[pltpu.VMEM((HGRN_HEADS, LANES, LANES), F32)],
        compiler_params=_cparams("parallel", "arbitrary"),
        name="hgrn_recurrence",
    )(q, lf, k, v, g, out_gain)


def _proj_res_kernel(h_ref, x_ref, w_ref, o_ref):
    o_ref[...] = h_ref[...] + jnp.dot(x_ref[...], w_ref[...], preferred_element_type=F32)


def _proj_residual(h2d, x_bf, w_bf, tm):
    T = h2d.shape[0]
    row = pl.BlockSpec((tm, D_MODEL), lambda i: (i, 0))
    return pl.pallas_call(
        _proj_res_kernel,
        grid=(T // tm,),
        in_specs=[row, row, pl.BlockSpec((D_MODEL, D_MODEL), lambda i: (0, 0))],
        out_specs=row,
        out_shape=jax.ShapeDtypeStruct((T, D_MODEL), F32),
        compiler_params=_cparams("parallel"),
        name="proj_residual",
    )(h2d, x_bf, w_bf)


def _router_kernel(h_ref, ng_ref, whi_ref, wlo_ref, xn_ref, info_ref, cnt_ref, carry_ref, *, tm):
    @pl.when(pl.program_id(0) == 0)
    def _():
        carry_ref[...] = jnp.zeros(carry_ref.shape, F32)

    xn = _rms_rows(h_ref[...], ng_ref[...])
    xn_ref[...] = xn
    xhi = xn.astype(BF16)
    xlo = (xn - xhi.astype(F32)).astype(BF16)
    logits = (jnp.dot(xhi, whi_ref[...], preferred_element_type=F32)
              + jnp.dot(xhi, wlo_ref[...], preferred_element_type=F32)
              + jnp.dot(xlo, whi_ref[...], preferred_element_type=F32))
    lane = lax.broadcasted_iota(I32, (tm, LANES), 1)
    lanef = lane.astype(F32)
    logits = jnp.where(lane < N_EXPERTS, logits, -jnp.inf)
    m1 = jnp.max(logits, axis=-1, keepdims=True)
    i1 = jnp.min(jnp.where(logits == m1, lanef, float(LANES)), axis=-1, keepdims=True)
    oh1 = lanef == i1
    rest = jnp.where(oh1, -jnp.inf, logits)
    m2 = jnp.max(rest, axis=-1, keepdims=True)
    i2 = jnp.min(jnp.where(rest == m2, lanef, float(LANES)), axis=-1, keepdims=True)
    oh2 = lanef == i2
    e = jnp.exp(m2 - m1)
    w1 = 1.0 / (1.0 + e)
    w2 = e * w1

    chosen = jnp.where(oh1 | oh2, 1.0, 0.0)
    r = lax.broadcasted_iota(I32, (tm, tm), 0)
    c = lax.broadcasted_iota(I32, (tm, tm), 1)
    before = jnp.where(c < r, 1.0, 0.0).astype(BF16)
    excl = jnp.dot(before, chosen.astype(BF16), preferred_element_type=F32) + carry_ref[0:1, :]
    rank1 = jnp.sum(jnp.where(oh1, excl, 0.0), axis=-1, keepdims=True)
    rank2 = jnp.sum(jnp.where(oh2, excl, 0.0), axis=-1, keepdims=True)
    total = carry_ref[0:1, :] + jnp.sum(chosen, axis=0, keepdims=True)
    carry_ref[...] = jnp.broadcast_to(total, carry_ref.shape)
    cnt_ref[...] = jnp.broadcast_to(total, cnt_ref.shape)

    info = jnp.where(lane == 0, i1, 0.0)
    info = jnp.where(lane == 1, i2, info)
    info = jnp.where(lane == 2, rank1, info)
    info = jnp.where(lane == 3, rank2, info)
    info = jnp.where(lane == 4, w1, info)
    info = jnp.where(lane == 5, w2, info)
    info_ref[...] = info


def _router(h2d, norm_gain, w_hi, w_lo, tm):
    T = h2d.shape[0]
    const = lambda i: (0, 0)
    kern = functools.partial(_router_kernel, tm=tm)
    return pl.pallas_call(
        kern,
        grid=(T // tm,),
        in_specs=[
            pl.BlockSpec((tm, D_MODEL), lambda i: (i, 0)),
            pl.BlockSpec((1, D_MODEL), const),
            pl.BlockSpec((D_MODEL, LANES), const),
            pl.BlockSpec((D_MODEL, LANES), const),
        ],
        out_specs=[
            pl.BlockSpec((tm, D_MODEL), lambda i: (i, 0)),
            pl.BlockSpec((tm, LANES), lambda i: (i, 0)),
            pl.BlockSpec((8, LANES), const),
        ],
        out_shape=[
            jax.ShapeDtypeStruct((T, D_MODEL), F32),
            jax.ShapeDtypeStruct((T, LANES), F32),
            jax.ShapeDtypeStruct((8, LANES), F32),
        ],
        scratch_shapes=[pltpu.VMEM((8, LANES), F32)],
        compiler_params=_cparams("arbitrary"),
        name="moe_router",
    )(h2d, norm_gain, w_hi, w_lo)


def _dispatch_kernel(zs_ref, pos_ref, xn_ref, xs_hbm, zero_ref, sem, zsem, *, tt, tme):
    @pl.when(pl.program_id(0) == 0)
    def _():
        zero_ref[...] = jnp.zeros(zero_ref.shape, F32)
        for e in range(2 * N_EXPERTS):
            row0 = pl.multiple_of(zs_ref[e], tme)
            fill = pltpu.make_async_copy(zero_ref, xs_hbm.at[pl.ds(row0, tme)], zsem)
            fill.start()
            fill.wait()

    def copy(t, s):
        return pltpu.make_async_copy(
            xn_ref.at[pl.ds(t, 1)], xs_hbm.at[pl.ds(pos_ref[0, 0, 2 * t + s], 1)], sem)

    def start(t, carry):
        copy(t, 0).start()
        copy(t, 1).start()
        return carry

    def wait(t, carry):
        copy(t, 0).wait()
        copy(t, 1).wait()
        return carry

    lax.fori_loop(0, tt, start, 0, unroll=8)
    lax.fori_loop(0, tt, wait, 0, unroll=8)


def _dispatch(xn, pos, last_tile_start, n_rows, tt, tme):
    T = xn.shape[0]
    pos3 = pos.reshape(T // tt, 1, 2 * tt)
    kern = functools.partial(_dispatch_kernel, tt=tt, tme=tme)
    grid_spec = pltpu.PrefetchScalarGridSpec(
        num_scalar_prefetch=1,
        grid=(T // tt,),
        in_specs=[
            pl.BlockSpec((1, 1, 2 * tt), lambda i, zs: (i, 0, 0), memory_space=pltpu.SMEM),
            pl.BlockSpec((tt, D_MODEL), lambda i, zs: (i, 0)),
        ],
        out_specs=pl.BlockSpec(memory_space=pl.ANY),
        scratch_shapes=[pltpu.VMEM((tme, D_MODEL), F32), pltpu.SemaphoreType.DMA(()),
                        pltpu.SemaphoreType.DMA(())],
    )
    return pl.pallas_call(
        kern,
        grid_spec=grid_spec,
        out_shape=jax.ShapeDtypeStruct((n_rows, D_MODEL), F32),
        compiler_params=_cparams("arbitrary"),
        name="moe_dispatch",
    )(last_tile_start, pos3, xn)


def _expert_kernel(te_ref, na_ref, xs_ref, wg_ref, wu_ref, wd_ref, y_ref, *, nchunk):
    del te_ref
    active = pl.program_id(0) < na_ref[0]

    @pl.when(jnp.logical_not(active))
    def _():
        y_ref[...] = jnp.zeros(y_ref.shape, F32)

    @pl.when(active)
    def _():
        xb = xs_ref[...].astype(BF16)
        tf = D_FF_EXPERT // nchunk
        acc = None
        for c in range(nchunk):
            cols = slice(c * tf, (c + 1) * tf)
            g = jnp.dot(xb, wg_ref[0, :, cols], preferred_element_type=F32)
            u = jnp.dot(xb, wu_ref[0, :, cols], preferred_element_type=F32)
            a = (_silu(g) * u).astype(BF16)
            d = jnp.dot(a, wd_ref[0, cols, :], preferred_element_type=F32)
            acc = d if acc is None else acc + d
        y_ref[...] = acc


def _experts(xs, tile_expert, n_active, wg, wu, wd, tme, nchunk):
    n_rows = xs.shape[0]
    n_tiles = n_rows // tme

    def tile(i, na):
        return jnp.minimum(i, na[0] - 1)

    resident = pl.Buffered(1)
    grid_spec = pltpu.PrefetchScalarGridSpec(
        num_scalar_prefetch=2,
        grid=(n_tiles,),
        in_specs=[
            pl.BlockSpec((tme, D_MODEL), lambda i, te, na: (tile(i, na), 0)),
            pl.BlockSpec((1, D_MODEL, D_FF_EXPERT), lambda i, te, na: (te[tile(i, na)], 0, 0),
                         pipeline_mode=resident),
            pl.BlockSpec((1, D_MODEL, D_FF_EXPERT), lambda i, te, na: (te[tile(i, na)], 0, 0),
                         pipeline_mode=resident),
            pl.BlockSpec((1, D_FF_EXPERT, D_MODEL), lambda i, te, na: (te[tile(i, na)], 0, 0),
                         pipeline_mode=resident),
        ],
        out_specs=pl.BlockSpec((tme, D_MODEL), lambda i, te, na: (i, 0)),
    )
    return pl.pallas_call(
        functools.partial(_expert_kernel, nchunk=nchunk),
        grid_spec=grid_spec,
        out_shape=jax.ShapeDtypeStruct((n_rows, D_MODEL), F32),
        compiler_params=_cparams("arbitrary"),
        name="moe_experts",
    )(tile_expert, n_active, xs, wg, wu, wd)


def _combine_kernel(pos_ref, info_ref, h_ref, y_hbm, o_ref, buf_ref, sem, *, tc):
    def copy(t, s):
        return pltpu.make_async_copy(
            y_hbm.at[pl.ds(pos_ref[0, 0, 2 * t + s], 1)], buf_ref.at[s, pl.ds(t, 1)], sem)

    def start(t, carry):
        copy(t, 0).start()
        copy(t, 1).start()
        return carry

    def wait(t, carry):
        copy(t, 0).wait()
        copy(t, 1).wait()
        return carry

    lax.fori_loop(0, tc, start, 0, unroll=8)
    lax.fori_loop(0, tc, wait, 0, unroll=8)
    info = info_ref[...]
    lane = lax.broadcasted_iota(I32, (tc, LANES), 1)
    w1 = jnp.sum(jnp.where(lane == 4, info, 0.0), axis=-1, keepdims=True)
    w2 = jnp.sum(jnp.where(lane == 5, info, 0.0), axis=-1, keepdims=True)
    o_ref[...] = h_ref[...] + (w1 * buf_ref[0] + w2 * buf_ref[1])


def _combine(h2d, info, pos, y, tc):
    T = h2d.shape[0]
    pos3 = pos.reshape(T // tc, 1, 2 * tc)
    kern = functools.partial(_combine_kernel, tc=tc)
    return pl.pallas_call(
        kern,
        grid=(T // tc,),
        in_specs=[
            pl.BlockSpec((1, 1, 2 * tc), lambda i: (i, 0, 0), memory_space=pltpu.SMEM),
            pl.BlockSpec((tc, LANES), lambda i: (i, 0)),
            pl.BlockSpec((tc, D_MODEL), lambda i: (i, 0)),
            pl.BlockSpec(memory_space=pl.ANY),
        ],
        out_specs=pl.BlockSpec((tc, D_MODEL), lambda i: (i, 0)),
        out_shape=jax.ShapeDtypeStruct((T, D_MODEL), F32),
        scratch_shapes=[pltpu.VMEM((2, tc, D_MODEL), F32), pltpu.SemaphoreType.DMA(())],
        compiler_params=_cparams("arbitrary"),
        name="moe_combine",
    )(pos3, info, h2d, y)


def _bf16_pieces(x, n):
    pieces, r = [], x.astype(F32)
    for _ in range(n):
        p = r.astype(BF16)
        pieces.append(p)
        r = r - p.astype(F32)
    return pieces


def _alibi_slopes():
    return [2.0 ** (-8.0 * (h + 1.0) / N_ATTN_HEADS) for h in range(N_ATTN_HEADS)]


def _attn_col_gain(dil_q_gain, dil_k_gain, diff_q_gain, diff_k_gain):
    rep = lambda g, n: jnp.tile(g.astype(F32), n)
    ones = jnp.ones((COLB,), F32)
    dq = rep(dil_q_gain, COLB // HEAD_DIM) * (HEAD_DIM ** -0.5)
    dk = rep(dil_k_gain, COLB // HEAD_DIM)
    fq = rep(diff_q_gain, COLB // DIFF_QK_DIM) * (DIFF_QK_DIM ** -0.5 * LOG2E)
    fk = rep(diff_k_gain, COLB // DIFF_QK_DIM)
    return jnp.stack([dq, dq, dq, dk, dk, dk, ones, ones, ones, fq, fk, ones])


def _layer0(h2d, B, S, attn_norm, w_in, dq_g, dk_g, fq_g, fk_g, lq1, lk1, lq2, lk2, out_g, w_out,
            ffn_norm, w_gate, w_up, w_down, layer):
    T = B * S
    tm = min(512, T)
    slopes = _alibi_slopes()
    *cls, dproj = _attn_inproj(h2d, attn_norm.reshape(1, D_MODEL).astype(F32), w_in.astype(BF16),
                               _attn_col_gain(dq_g, dk_g, fq_g, fk_g), tm)
    outs, lses = [], []
    for g, (window, dilation) in enumerate(DIL_GROUPS):
        assert window // dilation == QBLOCK and S % (dilation * QBLOCK) == 0 and tm % dilation == 0
        o, lse = _dilated_group(cls[g], B, S, dilation, tuple(slopes[4 * g:4 * g + 4]))
        outs.extend(o)
        lses.extend(lse)

    lam_init = 0.8 - 0.6 * math.exp(-0.3 * layer)
    lam = (jnp.exp(jnp.sum(lq1.astype(F32) * lk1.astype(F32)))
           - jnp.exp(jnp.sum(lq2.astype(F32) * lk2.astype(F32))) + lam_init).reshape(1)
    sl = jnp.asarray(slopes[N_DIL_HEADS:], F32) * LOG2E
    s_hi, s_lo = _bf16_pieces(sl, 2)
    slope_feat = jnp.zeros((8, LANES), F32)
    slope_feat = slope_feat.at[:N_DIFF_HEADS, 0].set(s_hi.astype(F32)).at[:N_DIFF_HEADS, 1].set(s_hi.astype(F32))
    slope_feat = slope_feat.at[:N_DIFF_HEADS, 2].set(s_lo.astype(F32)).at[:N_DIFF_HEADS, 3].set(s_lo.astype(F32))
    p_hi, p_lo = _bf16_pieces(jnp.arange(S, dtype=F32), 2)
    pos_feat = jnp.zeros((S, LANES), BF16)
    pos_feat = pos_feat.at[:, 0].set(p_hi).at[:, 1].set(p_lo).at[:, 2].set(p_hi).at[:, 3].set(p_lo)
    out_gain = (jnp.tile(out_g.astype(F32), COLB // HEAD_DIM) * (1.0 - lam_init)).reshape(1, COLB)
    diff = _diff_attention(dproj.reshape(B, S, _QKV), lam, slope_feat, pos_feat, out_gain, min(512, S)).reshape(T, COLB)

    h2d = _attn_outproj(h2d, outs, lses, diff, w_out.astype(BF16), tm)
    return _ffn(h2d, ffn_norm.reshape(1, D_MODEL).astype(F32), w_gate.astype(BF16),
                w_up.astype(BF16), w_down.astype(BF16), min(256, T))


def _layer1(h2d, B, S, hgrn_norm, w_in, lb_logits, out_gain, w_out,
            moe_norm, w_router, w_gate, w_up, w_down, layer):
    T = B * S
    tm = min(512, T)
    sm = jax.nn.softmax(lb_logits.astype(F32), axis=0)
    lb = (jnp.cumsum(sm, axis=0) - sm[0])[layer]
    lb_rows = jnp.zeros((8, D_MODEL), F32).at[0].set(jnp.log(lb)).at[1].set(jnp.log1p(-lb)).at[2].set(1.0 - lb)
    q, lf, k, v, g = _hgrn_inproj(h2d, hgrn_norm.reshape(1, D_MODEL).astype(F32), w_in.astype(BF16),
                                  lb_rows, tm)
    o = _hgrn_recurrence(q, lf, k, v, g, out_gain.reshape(1, LANES).astype(F32), B, S, min(256, S))
    h2d = _proj_residual(h2d, o, w_out.astype(BF16), tm)
    return _moe_block(h2d, moe_norm, w_router, w_gate, w_up, w_down)


def _moe_block(h2d, moe_norm, w_router, w_gate, w_up, w_down):
    T = h2d.shape[0]
    tme = 512
    w_pad = jnp.zeros((D_MODEL, LANES), F32).at[:, :N_EXPERTS].set(w_router.astype(F32))
    w_hi, w_lo = _bf16_pieces(w_pad, 2)
    xn, info, counts = _router(h2d, moe_norm.reshape(1, D_MODEL).astype(F32), w_hi, w_lo, min(256, T))
    cnt = counts[0, :N_EXPERTS].astype(I32)
    padded = ((cnt + tme - 1) // tme) * tme
    ends = jnp.cumsum(padded)
    starts = ends - padded
    experts = info[:, 0:2].astype(I32)
    ranks = info[:, 2:4].astype(I32)
    pos = (starts[experts] + ranks).reshape(-1)
    n_rows = 2 * T + N_EXPERTS * tme
    n_tiles = n_rows // tme
    tile_start = jnp.arange(n_tiles, dtype=I32) * tme
    tile_expert = jnp.minimum(jnp.sum((ends[None, :] <= tile_start[:, None]).astype(I32), axis=1), N_EXPERTS - 1)
    n_active = (ends[-1] // tme).astype(I32).reshape(1)
    tail = jnp.minimum(ends[-1] + jnp.arange(N_EXPERTS, dtype=I32) * tme, n_rows - tme)
    zero_tiles = jnp.concatenate([jnp.maximum(ends - tme, 0), tail]).astype(I32)
    xs = _dispatch(xn, pos, zero_tiles, n_rows, min(256, T), tme)
    y = _experts(xs, tile_expert, n_active, w_gate.astype(BF16), w_up.astype(BF16), w_down.astype(BF16),
                 tme, 2)
    return _combine(h2d, info, pos, y, min(256, T))


def kernel(x, attn_norm, attn_w_in, dil_q_gain, dil_k_gain, diff_q_gain, diff_k_gain, diff_lambda_q1, diff_lambda_k1, diff_lambda_q2, diff_lambda_k2, diff_out_gain, attn_w_out, ffn_norm, ffn_w_gate, ffn_w_up, ffn_w_down, hgrn_norm, hgrn_w_in, hgrn_lb_logits, hgrn_out_gain, hgrn_w_out, moe_norm, moe_w_router, moe_w_gate, moe_w_up, moe_w_down):
    B, S, D = x.shape
    assert D == D_MODEL
    h = x.astype(F32).reshape(B * S, D)
    h = _layer0(h, B, S, attn_norm[0], attn_w_in[0], dil_q_gain[0], dil_k_gain[0], diff_q_gain[0],
                diff_k_gain[0], diff_lambda_q1[0], diff_lambda_k1[0], diff_lambda_q2[0],
                diff_lambda_k2[0], diff_out_gain[0], attn_w_out[0], ffn_norm[0], ffn_w_gate[0],
                ffn_w_up[0], ffn_w_down[0], 0)
    h = _layer1(h, B, S, hgrn_norm[0], hgrn_w_in[0], hgrn_lb_logits, hgrn_out_gain[0], hgrn_w_out[0],
                moe_norm[0], moe_w_router[0], moe_w_gate[0], moe_w_up[0], moe_w_down[0], 1)
    return h.reshape(B, S, D).astype(x.dtype)
```

```python
import functools
import math

import jax
import jax.numpy as jnp
from jax import lax
from jax.experimental import pallas as pl
from jax.experimental.pallas import tpu as pltpu

F32 = jnp.float32
BF16 = jnp.bfloat16
I32 = jnp.int32

D_MODEL = 1024
HEAD_DIM = 64
N_ATTN_HEADS = 16
N_DIL_HEADS = 12
N_DIFF_HEADS = 4
DIL_GROUPS = ((128, 1), (512, 4), (2048, 16))
DIFF_QK_DIM = 32
ATTN_IN_WIDTH = 3072
QBLOCK = 128
HGRN_HEADS = 8
HGRN_CHUNK = 64
D_FF = 2816
N_EXPERTS = 8
D_FF_EXPERT = 3584
EPS = 1e-6
NEG_INF = -1e30
LOG2E = 1.4426950408889634
LANES = 128
COLB = 256
VMEM_LIMIT = 56 * 1024 * 1024


def _cparams(*sem):
    return pltpu.CompilerParams(dimension_semantics=sem, vmem_limit_bytes=VMEM_LIMIT)


def _split_dot(x, m, terms):
    acc = None
    r = x
    for t in range(terms):
        part = r.astype(BF16)
        d = jnp.dot(part, m, preferred_element_type=F32)
        acc = d if acc is None else acc + d
        if t + 1 < terms:
            r = r - part.astype(F32)
    return acc


def _seg_matrix(n, seg):
    i = jnp.arange(n)
    return (i[:, None] // seg == i[None, :] // seg).astype(BF16)


def _rms_rows(x, gain_row):
    return x * lax.rsqrt(jnp.mean(x * x, axis=-1, keepdims=True) + EPS) * gain_row


def _sigmoid(x):
    return 1.0 / (1.0 + jnp.exp(-x))


def _silu(x):
    return x * _sigmoid(x)


_ATTN_NORM_SEG = (64, 64, 64, 64, 64, 64, 0, 0, 0, 32, 32, 0)
_QKV = 3 * COLB


def _attn_inproj_kernel(h_ref, ng_ref, w_ref, cg_ref, s64_ref, s32_ref,
                        c0_ref, c1_ref, c2_ref, df_ref, y_ref, *, tm):
    cls_refs = (c0_ref, c1_ref, c2_ref)
    xb = _rms_rows(h_ref[...], ng_ref[...]).astype(BF16)
    for c, seg in enumerate(_ATTN_NORM_SEG):
        y = jnp.dot(xb, w_ref[:, c * COLB:(c + 1) * COLB], preferred_element_type=F32)
        if seg:
            m = s64_ref[...] if seg == 64 else s32_ref[...]
            ms = _split_dot(y * y, m, 2) * (1.0 / seg)
            y = y * lax.rsqrt(ms + EPS) * cg_ref[c:c + 1, :]
        part, g = divmod(c, 3)
        if part == 3:
            df_ref[:, g * COLB:(g + 1) * COLB] = y.astype(BF16)
            continue
        d = DIL_GROUPS[g][1]
        if d == 1:
            cls_refs[g][:, part * COLB:(part + 1) * COLB] = y.astype(BF16)
            continue
        y_ref[0] = y[:, :LANES]
        y_ref[1] = y[:, LANES:]
        for r in range(d):
            col = r * _QKV + part * COLB
            rows = pl.ds(r, tm // d, stride=d)
            cls_refs[g][:, col:col + COLB] = jnp.concatenate(
                [y_ref[0, rows, :], y_ref[1, rows, :]], axis=1).astype(BF16)


def _attn_inproj(h2d, norm_gain, w_bf, col_gain, tm):
    T = h2d.shape[0]
    const = lambda i: (0, 0)
    row = lambda i: (i, 0)
    dils = [d for _, d in DIL_GROUPS]
    kern = functools.partial(_attn_inproj_kernel, tm=tm)
    return pl.pallas_call(
        kern,
        grid=(T // tm,),
        in_specs=[
            pl.BlockSpec((tm, D_MODEL), row),
            pl.BlockSpec((1, D_MODEL), const),
            pl.BlockSpec((D_MODEL, ATTN_IN_WIDTH), const),
            pl.BlockSpec((ATTN_IN_WIDTH // COLB, COLB), const),
            pl.BlockSpec((COLB, COLB), const),
            pl.BlockSpec((COLB, COLB), const),
        ],
        out_specs=[pl.BlockSpec((tm // d, d * _QKV), row) for d in dils] + [pl.BlockSpec((tm, _QKV), row)],
        out_shape=[jax.ShapeDtypeStruct((T // d, d * _QKV), BF16) for d in dils]
                  + [jax.ShapeDtypeStruct((T, _QKV), BF16)],
        scratch_shapes=[pltpu.VMEM((2, tm, LANES), F32)],
        compiler_params=_cparams("parallel"),
        name="attn_inproj",
    )(h2d, norm_gain, w_bf, col_gain, _seg_matrix(COLB, 64), _seg_matrix(COLB, 32))


def _dil_kernel(q_ref, kp_ref, kc_ref, vp_ref, vc_ref, o0_ref, o1_ref, l0_ref, l1_ref,
                *, slopes, dilation, ub):
    u = pl.program_id(1)
    r = pl.program_id(2)
    q = q_ref[...]
    kcat = jnp.concatenate([kp_ref[...], kc_ref[...]], axis=0)
    vcat = jnp.concatenate([vp_ref[...], vc_ref[...]], axis=0)
    qi = lax.broadcasted_iota(I32, (QBLOCK, 2 * QBLOCK), 0)
    kj = lax.broadcasted_iota(I32, (QBLOCK, 2 * QBLOCK), 1)
    step = qi + QBLOCK - kj
    in_window = (step >= 0) & (step <= QBLOCK)
    first_window = in_window & ((kj >= QBLOCK) | (u > 0))
    stepf = step.astype(F32)
    bias = [(-slopes[h] * dilation * LOG2E) * stepf for h in range(COLB // HEAD_DIM)]
    mask_bias = [jnp.where(in_window, b, NEG_INF) for b in bias]
    mask_bias_first = [jnp.where(first_window, b, NEG_INF) for b in bias]
    lane = lax.broadcasted_iota(I32, (QBLOCK, LANES), 1)
    low_half = lane < HEAD_DIM
    for i in range(ub // QBLOCK):
        qb = q[i * QBLOCK:(i + 1) * QBLOCK]
        kb = kcat[i * QBLOCK:(i + 2) * QBLOCK]
        vb = vcat[i * QBLOCK:(i + 2) * QBLOCK]
        for pair in range(2):
            lanes = slice(pair * LANES, (pair + 1) * LANES)
            qp, kp, vp = qb[:, lanes], kb[:, lanes], vb[:, lanes]
            outs, lses = [], []
            for hh in range(2):
                h = pair * 2 + hh
                qm = jnp.where(low_half if hh == 0 else ~low_half, qp, jnp.zeros_like(qp))
                s = lax.dot_general(qm, kp, (((1,), (1,)), ((), ())), preferred_element_type=F32)
                s = s + (mask_bias_first[h] if i == 0 else mask_bias[h])
                m = jnp.max(s, axis=-1, keepdims=True)
                e = jnp.exp2(s - m)
                l = jnp.sum(e, axis=-1, keepdims=True)
                pv = jnp.dot(e.astype(BF16), vp, preferred_element_type=F32)
                outs.append(pv / l)
                lses.append(jnp.broadcast_to(m + jnp.log2(l), (QBLOCK, LANES)))
            if dilation == 1:
                rows = slice(i * QBLOCK, (i + 1) * QBLOCK)
            else:
                rows = pl.ds(r + i * QBLOCK * dilation, QBLOCK, stride=dilation)
            (o0_ref, o1_ref)[pair][rows, :] = jnp.where(low_half, outs[0], outs[1])
            (l0_ref, l1_ref)[pair][rows, :] = jnp.where(low_half, lses[0], lses[1])


def _dilated_group(cls, B, S, dilation, slopes):
    U = S // dilation
    ub = min(512, U)
    sub = ub // QBLOCK
    view = cls.reshape(B, U, dilation * _QKV)
    prev = lambda u: jnp.maximum(u * sub - 1, 0)
    kern = functools.partial(_dil_kernel, slopes=slopes, dilation=dilation, ub=ub)
    out_spec = pl.BlockSpec((None, ub * dilation, LANES), lambda b, u, r: (b, u, 0))
    res = pl.pallas_call(
        kern,
        grid=(B, U // ub, dilation),
        in_specs=[
            pl.BlockSpec((None, ub, COLB), lambda b, u, r: (b, u, 3 * r)),
            pl.BlockSpec((None, QBLOCK, COLB), lambda b, u, r: (b, prev(u), 3 * r + 1)),
            pl.BlockSpec((None, ub, COLB), lambda b, u, r: (b, u, 3 * r + 1)),
            pl.BlockSpec((None, QBLOCK, COLB), lambda b, u, r: (b, prev(u), 3 * r + 2)),
            pl.BlockSpec((None, ub, COLB), lambda b, u, r: (b, u, 3 * r + 2)),
        ],
        out_specs=[out_spec] * 4,
        out_shape=[jax.ShapeDtypeStruct((B, S, LANES), F32)] * 4,
        compiler_params=_cparams("parallel", "parallel", "arbitrary"),
        name=f"dilated_attn_d{dilation}",
    )(view, view, view, view, view)
    res = [a.reshape(B * S, LANES) for a in res]
    return res[:2], res[2:]


def _diff_kernel(lam_ref, q_ref, k_ref, v_ref, pos_ref, sl_ref, og_ref, s64_ref, o_ref,
                 m_ref, l_ref, acc_ref, qa_ref, *, tq):
    i = pl.program_id(1)
    q = q_ref[...]
    lane = lax.broadcasted_iota(I32, (tq, LANES), 1)
    low_half = lane < HEAD_DIM
    for pair in range(2):
        qp = q[:, pair * LANES:(pair + 1) * LANES]
        for hh in range(2):
            feat = jnp.broadcast_to(sl_ref[pair * 2 + hh:pair * 2 + hh + 1, :], (tq, LANES)).astype(BF16)
            for mu in range(2):
                lo = hh * HEAD_DIM + mu * DIFF_QK_DIM
                sel = (lane >= lo) & (lane < lo + DIFF_QK_DIM)
                r = hh * 2 + mu
                qa_ref[pair, r * tq:(r + 1) * tq, :] = jnp.concatenate(
                    [jnp.where(sel, qp, jnp.zeros_like(qp)), feat], axis=1)

    m_ref[...] = jnp.full(m_ref.shape, NEG_INF, F32)
    l_ref[...] = jnp.zeros(l_ref.shape, F32)
    acc_ref[...] = jnp.zeros(acc_ref.shape, F32)

    def scores(j, nk):
        ks = pl.multiple_of(j * tq, tq)
        kblk = k_ref[pl.ds(ks, nk * tq), :]
        pblk = pos_ref[pl.ds(ks, nk * tq), :]
        s_alls = []
        for pair in range(2):
            kaug = jnp.concatenate([kblk[:, pair * LANES:(pair + 1) * LANES], pblk], axis=1)
            s_alls.append(lax.dot_general(qa_ref[pair], kaug, (((1,), (1,)), ((), ())),
                                          preferred_element_type=F32))
        return s_alls, v_ref[pl.ds(ks, nk * tq), :]

    def finish(s_alls, vblk, masked):
        tk = vblk.shape[0]
        ones = jnp.ones((tk, LANES), BF16)
        if masked:
            row = lax.broadcasted_iota(I32, (tq, tk), 0)
            col = lax.broadcasted_iota(I32, (tq, tk), 1)
            causal = col <= row
        probs, alphas = [], []
        for pair in range(2):
            ps = []
            for r in range(4):
                idx = pair * 4 + r
                s = s_alls[pair][r * tq:(r + 1) * tq]
                if masked:
                    s = jnp.where(causal, s, NEG_INF)
                m_prev = m_ref[idx]
                m_next = jnp.maximum(m_prev, jnp.max(s, axis=-1, keepdims=True))
                m_ref[idx] = m_next
                alphas.append(jnp.exp2(m_prev - m_next))
                ps.append(jnp.exp2(s - jnp.concatenate([m_next] * (tk // LANES), axis=1)).astype(BF16))
            probs.append(jnp.concatenate(ps, axis=0))
        for pair in range(2):
            vaug = jnp.concatenate([vblk[:, pair * LANES:(pair + 1) * LANES], ones], axis=1)
            pv = jnp.dot(probs[pair], vaug, preferred_element_type=F32)
            for r in range(4):
                idx = pair * 4 + r
                part = pv[r * tq:(r + 1) * tq]
                acc_ref[idx] = alphas[idx] * acc_ref[idx] + part[:, :LANES]
                l_ref[idx] = alphas[idx] * l_ref[idx] + part[:, LANES:]

    def two_full_blocks(jj, carry):
        finish(*scores(2 * jj, 2), False)
        return carry

    lax.fori_loop(0, i // 2, two_full_blocks, 0)

    @pl.when(i % 2 == 1)
    def _():
        finish(*scores(i - 1, 1), False)

    finish(*scores(i, 1), True)

    lam = lam_ref[0]
    for pair in range(2):
        o = None
        for mu in range(2):
            lo_idx, hi_idx = pair * 4 + mu, pair * 4 + 2 + mu
            term = jnp.where(low_half, acc_ref[lo_idx] / l_ref[lo_idx], acc_ref[hi_idx] / l_ref[hi_idx])
            o = term if mu == 0 else o - lam * term
        ms = _split_dot(o * o, s64_ref[...], 2) * (1.0 / HEAD_DIM)
        o = o * lax.rsqrt(ms + EPS) * og_ref[:, pair * LANES:(pair + 1) * LANES]
        o_ref[:, pair * LANES:(pair + 1) * LANES] = o.astype(BF16)


def _diff_attention(proj, lam, slope_feat, pos_feat, out_gain, tq):
    B, S, W = proj.shape
    kern = functools.partial(_diff_kernel, tq=tq)
    return pl.pallas_call(
        kern,
        grid=(B, S // tq),
        in_specs=[
            pl.BlockSpec(memory_space=pltpu.SMEM),
            pl.BlockSpec((None, tq, COLB), lambda b, i: (b, i, 0)),
            pl.BlockSpec((None, S, COLB), lambda b, i: (b, 0, 1), pipeline_mode=pl.Buffered(1)),
            pl.BlockSpec((None, S, COLB), lambda b, i: (b, 0, 2), pipeline_mode=pl.Buffered(1)),
            pl.BlockSpec((S, LANES), lambda b, i: (0, 0), pipeline_mode=pl.Buffered(1)),
            pl.BlockSpec((8, LANES), lambda b, i: (0, 0)),
            pl.BlockSpec((1, COLB), lambda b, i: (0, 0)),
            pl.BlockSpec((LANES, LANES), lambda b, i: (0, 0)),
        ],
        out_specs=pl.BlockSpec((None, tq, COLB), lambda b, i: (b, i, 0)),
        out_shape=jax.ShapeDtypeStruct((B, S, COLB), BF16),
        scratch_shapes=[
            pltpu.VMEM((8, tq, LANES), F32),
            pltpu.VMEM((8, tq, LANES), F32),
            pltpu.VMEM((8, tq, LANES), F32),
            pltpu.VMEM((2, 4 * tq, 2 * LANES), BF16),
        ],
        compiler_params=_cparams("parallel", "parallel"),
        name="diff_attn",
    )(lam, proj, proj, proj, pos_feat, slope_feat, out_gain, _seg_matrix(LANES, HEAD_DIM))


def _merge_heads(o_refs, l_refs, d_ref):
    n = len(o_refs)
    pieces = [None] * n
    for pair in range(2):
        ls = [l_refs[2 * g + pair][...] for g in range(n // 2)]
        mx = functools.reduce(jnp.maximum, ls)
        es = [jnp.exp2(l - mx) for l in ls]
        inv = 1.0 / functools.reduce(jnp.add, es)
        for g, e in enumerate(es):
            pieces[2 * g + pair] = (e * inv * o_refs[2 * g + pair][...]).astype(BF16)
    return jnp.concatenate(pieces + [d_ref[...]], axis=1)


def _hgrn_gates(xb, w_ref, lb_ref, q_ref, lf_ref, k_ref, v_ref, g_ref):
    sec = lambda c: jnp.dot(xb, w_ref[:, c * D_MODEL:(c + 1) * D_MODEL], preferred_element_type=F32)
    q_ref[...] = _silu(sec(0)).astype(BF16)
    fl = sec(1)
    log_lb, log1m_lb, one_m_lb = lb_ref[0:1, :], lb_ref[1:2, :], lb_ref[2:3, :]
    t = jnp.exp(-jnp.abs(fl))
    r = 1.0 / (1.0 + t)
    c = log1m_lb + (jnp.minimum(fl, 0.0) + jnp.log(r))
    lf_ref[...] = jnp.maximum(log_lb, c) + jnp.log(1.0 + jnp.exp(-jnp.abs(log_lb - c)))
    k_ref[...] = (one_m_lb * jnp.where(fl >= 0.0, t * r, r)).astype(BF16)
    v_ref[...] = sec(2).astype(BF16)
    g_ref[...] = _sigmoid(sec(3)).astype(BF16)


def _layer0_tail_kernel(h_ref, *refs):
    n = 2 * len(DIL_GROUPS)
    o_refs, l_refs = refs[:n], refs[n:2 * n]
    (d_ref, wo_ref, fng_ref, wg_ref, wu_ref, wd_ref, hng_ref, wh_ref, lb_ref,
     h2_ref, q_ref, lf_ref, k_ref, v_ref, g_ref) = refs[2 * n:]
    mixed = _merge_heads(o_refs, l_refs, d_ref)
    h1 = h_ref[...] + jnp.dot(mixed, wo_ref[...], preferred_element_type=F32)
    xb = _rms_rows(h1, fng_ref[...]).astype(BF16)
    g = jnp.dot(xb, wg_ref[...], preferred_element_type=F32)
    u = jnp.dot(xb, wu_ref[...], preferred_element_type=F32)
    a = (_silu(g) * u).astype(BF16)
    h2 = h1 + jnp.dot(a, wd_ref[...], preferred_element_type=F32)
    h2_ref[...] = h2
    _hgrn_gates(_rms_rows(h2, hng_ref[...]).astype(BF16), wh_ref, lb_ref,
                q_ref, lf_ref, k_ref, v_ref, g_ref)


def _layer0_tail(h2d, outs, lses, diff, w_out, ffn_gain, wg, wu, wd, hgrn_gain, w_in, lb_rows, tm):
    T = h2d.shape[0]
    row = lambda i: (i, 0)
    full = pl.BlockSpec((tm, D_MODEL), row)
    half = pl.BlockSpec((tm, LANES), row)
    held = lambda shape: pl.BlockSpec(shape, lambda i: (0, 0), pipeline_mode=pl.Buffered(1))
    bf = jax.ShapeDtypeStruct((T, D_MODEL), BF16)
    f32 = jax.ShapeDtypeStruct((T, D_MODEL), F32)
    return pl.pallas_call(
        _layer0_tail_kernel,
        grid=(T // tm,),
        in_specs=[full] + [half] * (len(outs) + len(lses))
                 + [pl.BlockSpec((tm, COLB), row), held((D_MODEL, D_MODEL)), held((1, D_MODEL)),
                    held((D_MODEL, D_FF)), held((D_MODEL, D_FF)), held((D_FF, D_MODEL)),
                    held((1, D_MODEL)), held((D_MODEL, 4 * D_MODEL)), held((8, D_MODEL))],
        out_specs=[full] * 6,
        out_shape=[f32, bf, f32, bf, bf, bf],
        compiler_params=_cparams("parallel"),
        name="layer0_tail_hgrn_inproj",
    )(h2d, *outs, *lses, diff, w_out, ffn_gain, wg, wu, wd, hgrn_gain, w_in, lb_rows)


def _hgrn_kernel(q_ref, lf_ref, k_ref, v_ref, g_ref, og_ref, o_ref, st_ref, *, rows):
    @pl.when(pl.program_id(1) == 0)
    def _():
        st_ref[...] = jnp.zeros(st_ref.shape, F32)

    C = HGRN_CHUNK
    nc = rows // C
    ti = lax.broadcasted_iota(I32, (rows, rows), 0)
    si = lax.broadcasted_iota(I32, (rows, rows), 1)
    causal = (si <= ti) & (si >= (ti // C) * C)
    tri = jnp.where(causal, 1.0, 0.0).astype(BF16)
    width = lf_ref.shape[1]
    b = _split_dot_lhs(tri, lf_ref[...])
    chunk_row = lambda i: jnp.concatenate(
        [jnp.broadcast_to(b[c * C + i:c * C + i + 1, :], (C, width)) for c in range(nc)], axis=0)
    b_mid = chunk_row(C // 2)
    b_last = chunk_row(C - 1)
    q = q_ref[...].astype(F32)
    k = k_ref[...].astype(F32)
    v = v_ref[...]
    qs = (q * jnp.exp(b - b_mid)).astype(BF16)
    ks = (k * jnp.exp(b_mid - b)).astype(BF16)
    qi = (q * jnp.exp(b)).astype(BF16)
    kst = (k * jnp.exp(b_last - b)).astype(BF16)
    dec = jnp.exp(b_last)
    nt = (((1,), (1,)), ((), ()))
    for hd in range(HGRN_HEADS):
        lanes = slice(hd * LANES, (hd + 1) * LANES)
        scores = lax.dot_general(qs[:, lanes], ks[:, lanes], nt, preferred_element_type=F32)
        scores = jnp.where(causal, scores, 0.0).astype(BF16)
        o = jnp.dot(scores, v[:, lanes], preferred_element_type=F32)
        st = st_ref[hd]
        inter = []
        for c in range(nc):
            rws = slice(c * C, (c + 1) * C)
            inter.append(lax.dot_general(qi[rws, lanes], st.astype(BF16), nt, preferred_element_type=F32))
            st = dec[c * C:c * C + 1, lanes] * st + lax.dot_general(
                v[rws, lanes], kst[rws, lanes], (((0,), (0,)), ((), ())), preferred_element_type=F32)
        st_ref[hd] = st
        o = o + jnp.concatenate(inter, axis=0)
        o = o * lax.rsqrt(jnp.mean(o * o, axis=-1, keepdims=True) + EPS) * og_ref[...]
        o_ref[:, lanes] = (o * g_ref[:, lanes].astype(F32)).astype(BF16)


def _split_dot_lhs(m, x):
    acc = None
    r = x
    for t in range(3):
        part = r.astype(BF16)
        d = jnp.dot(m, part, preferred_element_type=F32)
        acc = d if acc is None else acc + d
        if t < 2:
            r = r - part.astype(F32)
    return acc


def _hgrn_recurrence(q, lf, k, v, g, out_gain, B, S, rows):
    T = B * S
    nseq = S // rows
    row = pl.BlockSpec((rows, D_MODEL), lambda b, s: (b * nseq + s, 0))
    kern = functools.partial(_hgrn_kernel, rows=rows)
    return pl.pallas_call(
        kern,
        grid=(B, nseq),
        in_specs=[row] * 5 + [pl.BlockSpec((1, LANES), lambda b, s: (0, 0))],
        out_specs=row,
        out_shape=jax.ShapeDtypeStruct((T, D_MODEL), BF16),
        scratch_shapes=[pltpu.VMEM((HGRN_HEADS, LANES, LANES), F32)],
        compiler_params=_cparams("parallel", "arbitrary"),
        name="hgrn_recurrence",
    )(q, lf, k, v, g, out_gain)


def _proj_router_kernel(h_ref, x_ref, wo_ref, ng_ref, whi_ref, wlo_ref,
                        h2_ref, xn_ref, info_ref, cnt_ref, carry_ref, *, tm):
    @pl.when(pl.program_id(0) == 0)
    def _():
        carry_ref[...] = jnp.zeros(carry_ref.shape, F32)

    h2 = h_ref[...] + jnp.dot(x_ref[...], wo_ref[...], preferred_element_type=F32)
    h2_ref[...] = h2
    xn = _rms_rows(h2, ng_ref[...])
    xn_ref[...] = xn
    xhi = xn.astype(BF16)
    xlo = (xn - xhi.astype(F32)).astype(BF16)
    logits = (jnp.dot(xhi, whi_ref[...], preferred_element_type=F32)
              + jnp.dot(xhi, wlo_ref[...], preferred_element_type=F32)
              + jnp.dot(xlo, whi_ref[...], preferred_element_type=F32))
    lane = lax.broadcasted_iota(I32, (tm, LANES), 1)
    lanef = lane.astype(F32)
    logits = jnp.where(lane < N_EXPERTS, logits, -jnp.inf)
    m1 = jnp.max(logits, axis=-1, keepdims=True)
    i1 = jnp.min(jnp.where(logits == m1, lanef, float(LANES)), axis=-1, keepdims=True)
    oh1 = lanef == i1
    rest = jnp.where(oh1, -jnp.inf, logits)
    m2 = jnp.max(rest, axis=-1, keepdims=True)
    i2 = jnp.min(jnp.where(rest == m2, lanef, float(LANES)), axis=-1, keepdims=True)
    oh2 = lanef == i2
    e = jnp.exp(m2 - m1)
    w1 = 1.0 / (1.0 + e)
    w2 = e * w1

    chosen = jnp.where(oh1 | oh2, 1.0, 0.0)
    r = lax.broadcasted_iota(I32, (tm, tm), 0)
    c = lax.broadcasted_iota(I32, (tm, tm), 1)
    before = jnp.where(c < r, 1.0, 0.0).astype(BF16)
    excl = jnp.dot(before, chosen.astype(BF16), preferred_element_type=F32) + carry_ref[0:1, :]
    rank1 = jnp.sum(jnp.where(oh1, excl, 0.0), axis=-1, keepdims=True)
    rank2 = jnp.sum(jnp.where(oh2, excl, 0.0), axis=-1, keepdims=True)
    total = carry_ref[0:1, :] + jnp.sum(chosen, axis=0, keepdims=True)
    carry_ref[...] = jnp.broadcast_to(total, carry_ref.shape)
    cnt_ref[...] = jnp.broadcast_to(total, cnt_ref.shape)

    info = jnp.where(lane == 0, i1, 0.0)
    info = jnp.where(lane == 1, i2, info)
    info = jnp.where(lane == 2, rank1, info)
    info = jnp.where(lane == 3, rank2, info)
    info = jnp.where(lane == 4, w1, info)
    info = jnp.where(lane == 5, w2, info)
    info_ref[...] = info


def _proj_router(h2d, x_bf, w_out, norm_gain, w_hi, w_lo, tm):
    T = h2d.shape[0]
    const = lambda i: (0, 0)
    row = lambda i: (i, 0)
    full = pl.BlockSpec((tm, D_MODEL), row)
    kern = functools.partial(_proj_router_kernel, tm=tm)
    return pl.pallas_call(
        kern,
        grid=(T // tm,),
        in_specs=[full, full, pl.BlockSpec((D_MODEL, D_MODEL), const), pl.BlockSpec((1, D_MODEL), const),
                  pl.BlockSpec((D_MODEL, LANES), const), pl.BlockSpec((D_MODEL, LANES), const)],
        out_specs=[full, full, pl.BlockSpec((tm, LANES), row), pl.BlockSpec((8, LANES), const)],
        out_shape=[
            jax.ShapeDtypeStruct((T, D_MODEL), F32),
            jax.ShapeDtypeStruct((T, D_MODEL), F32),
            jax.ShapeDtypeStruct((T, LANES), F32),
            jax.ShapeDtypeStruct((8, LANES), F32),
        ],
        scratch_shapes=[pltpu.VMEM((8, LANES), F32)],
        compiler_params=_cparams("arbitrary"),
        name="hgrn_outproj_moe_router",
    )(h2d, x_bf, w_out, norm_gain, w_hi, w_lo)


def _dispatch_kernel(zs_ref, pos_ref, xn_ref, xs_hbm, zero_ref, sem, zsem, *, tt, tme):
    @pl.when(pl.program_id(0) == 0)
    def _():
        zero_ref[...] = jnp.zeros(zero_ref.shape, F32)
        for e in range(2 * N_EXPERTS):
            row0 = pl.multiple_of(zs_ref[e], tme)
            fill = pltpu.make_async_copy(zero_ref, xs_hbm.at[pl.ds(row0, tme)], zsem)
            fill.start()
            fill.wait()

    def copy(t, s):
        return pltpu.make_async_copy(
            xn_ref.at[pl.ds(t, 1)], xs_hbm.at[pl.ds(pos_ref[0, 0, 2 * t + s], 1)], sem)

    def start(t, carry):
        copy(t, 0).start()
        copy(t, 1).start()
        return carry

    def wait(t, carry):
        copy(t, 0).wait()
        copy(t, 1).wait()
        return carry

    lax.fori_loop(0, tt, start, 0, unroll=8)
    lax.fori_loop(0, tt, wait, 0, unroll=8)


def _dispatch(xn, pos, last_tile_start, n_rows, tt, tme):
    T = xn.shape[0]
    pos3 = pos.reshape(T // tt, 1, 2 * tt)
    kern = functools.partial(_dispatch_kernel, tt=tt, tme=tme)
    grid_spec = pltpu.PrefetchScalarGridSpec(
        num_scalar_prefetch=1,
        grid=(T // tt,),
        in_specs=[
            pl.BlockSpec((1, 1, 2 * tt), lambda i, zs: (i, 0, 0), memory_space=pltpu.SMEM),
            pl.BlockSpec((tt, D_MODEL), lambda i, zs: (i, 0)),
        ],
        out_specs=pl.BlockSpec(memory_space=pl.ANY),
        scratch_shapes=[pltpu.VMEM((tme, D_MODEL), F32), pltpu.SemaphoreType.DMA(()),
                        pltpu.SemaphoreType.DMA(())],
    )
    return pl.pallas_call(
        kern,
        grid_spec=grid_spec,
        out_shape=jax.ShapeDtypeStruct((n_rows, D_MODEL), F32),
        compiler_params=_cparams("arbitrary"),
        name="moe_dispatch",
    )(last_tile_start, pos3, xn)


def _expert_kernel(te_ref, na_ref, xs_ref, wg_ref, wu_ref, wd_ref, y_ref, *, nchunk):
    del te_ref
    active = pl.program_id(0) < na_ref[0]

    @pl.when(jnp.logical_not(active))
    def _():
        y_ref[...] = jnp.zeros(y_ref.shape, F32)

    @pl.when(active)
    def _():
        xb = xs_ref[...].astype(BF16)
        tf = D_FF_EXPERT // nchunk
        acc = None
        for c in range(nchunk):
            cols = slice(c * tf, (c + 1) * tf)
            g = jnp.dot(xb, wg_ref[0, :, cols], preferred_element_type=F32)
            u = jnp.dot(xb, wu_ref[0, :, cols], preferred_element_type=F32)
            a = (_silu(g) * u).astype(BF16)
            d = jnp.dot(a, wd_ref[0, cols, :], preferred_element_type=F32)
            acc = d if acc is None else acc + d
        y_ref[...] = acc


def _experts(xs, tile_expert, n_active, wg, wu, wd, tme, nchunk):
    n_rows = xs.shape[0]
    n_tiles = n_rows // tme

    def tile(i, na):
        return jnp.minimum(i, na[0] - 1)

    resident = pl.Buffered(1)
    grid_spec = pltpu.PrefetchScalarGridSpec(
        num_scalar_prefetch=2,
        grid=(n_tiles,),
        in_specs=[
            pl.BlockSpec((tme, D_MODEL), lambda i, te, na: (tile(i, na), 0)),
            pl.BlockSpec((1, D_MODEL, D_FF_EXPERT), lambda i, te, na: (te[tile(i, na)], 0, 0),
                         pipeline_mode=resident),
            pl.BlockSpec((1, D_MODEL, D_FF_EXPERT), lambda i, te, na: (te[tile(i, na)], 0, 0),
                         pipeline_mode=resident),
            pl.BlockSpec((1, D_FF_EXPERT, D_MODEL), lambda i, te, na: (te[tile(i, na)], 0, 0),
                         pipeline_mode=resident),
        ],
        out_specs=pl.BlockSpec((tme, D_MODEL), lambda i, te, na: (i, 0)),
    )
    return pl.pallas_call(
        functools.partial(_expert_kernel, nchunk=nchunk),
        grid_spec=grid_spec,
        out_shape=jax.ShapeDtypeStruct((n_rows, D_MODEL), F32),
        compiler_params=_cparams("arbitrary"),
        name="moe_experts",
    )(tile_expert, n_active, xs, wg, wu, wd)


def _combine_kernel(pos_ref, info_ref, h_ref, y_hbm, o_ref, buf_ref, sem, *, tc):
    def copy(t, s):
        return pltpu.make_async_copy(
            y_hbm.at[pl.ds(pos_ref[0, 0, 2 * t + s], 1)], buf_ref.at[s, pl.ds(t, 1)], sem)

    def start(t, carry):
        copy(t, 0).start()
        copy(t, 1).start()
        return carry

    def wait(t, carry):
        copy(t, 0).wait()
        copy(t, 1).wait()
        return carry

    lax.fori_loop(0, tc, start, 0, unroll=8)
    lax.fori_loop(0, tc, wait, 0, unroll=8)
    info = info_ref[...]
    lane = lax.broadcasted_iota(I32, (tc, LANES), 1)
    w1 = jnp.sum(jnp.where(lane == 4, info, 0.0), axis=-1, keepdims=True)
    w2 = jnp.sum(jnp.where(lane == 5, info, 0.0), axis=-1, keepdims=True)
    o_ref[...] = h_ref[...] + (w1 * buf_ref[0] + w2 * buf_ref[1])


def _combine(h2d, info, pos, y, tc):
    T = h2d.shape[0]
    pos3 = pos.reshape(T // tc, 1, 2 * tc)
    kern = functools.partial(_combine_kernel, tc=tc)
    return pl.pallas_call(
        kern,
        grid=(T // tc,),
        in_specs=[
            pl.BlockSpec((1, 1, 2 * tc), lambda i: (i, 0, 0), memory_space=pltpu.SMEM),
            pl.BlockSpec((tc, LANES), lambda i: (i, 0)),
            pl.BlockSpec((tc, D_MODEL), lambda i: (i, 0)),
            pl.BlockSpec(memory_space=pl.ANY),
        ],
        out_specs=pl.BlockSpec((tc, D_MODEL), lambda i: (i, 0)),
        out_shape=jax.ShapeDtypeStruct((T, D_MODEL), F32),
        scratch_shapes=[pltpu.VMEM((2, tc, D_MODEL), F32), pltpu.SemaphoreType.DMA(())],
        compiler_params=_cparams("arbitrary"),
        name="moe_combine",
    )(pos3, info, h2d, y)


def _bf16_pieces(x, n):
    pieces, r = [], x.astype(F32)
    for _ in range(n):
        p = r.astype(BF16)
        pieces.append(p)
        r = r - p.astype(F32)
    return pieces


def _alibi_slopes():
    return [2.0 ** (-8.0 * (h + 1.0) / N_ATTN_HEADS) for h in range(N_ATTN_HEADS)]


def _attn_col_gain(dil_q_gain, dil_k_gain, diff_q_gain, diff_k_gain):
    rep = lambda g, n: jnp.tile(g.astype(F32), n)
    ones = jnp.ones((COLB,), F32)
    dq = rep(dil_q_gain, COLB // HEAD_DIM) * (HEAD_DIM ** -0.5 * LOG2E)
    dk = rep(dil_k_gain, COLB // HEAD_DIM)
    fq = rep(diff_q_gain, COLB // DIFF_QK_DIM) * (DIFF_QK_DIM ** -0.5 * LOG2E)
    fk = rep(diff_k_gain, COLB // DIFF_QK_DIM)
    return jnp.stack([dq, dq, dq, dk, dk, dk, ones, ones, ones, fq, fk, ones])


def _lower_bound_rows(lb_logits, layer):
    sm = jax.nn.softmax(lb_logits.astype(F32), axis=0)
    lb = (jnp.cumsum(sm, axis=0) - sm[0])[layer]
    return jnp.zeros((8, D_MODEL), F32).at[0].set(jnp.log(lb)).at[1].set(jnp.log1p(-lb)).at[2].set(1.0 - lb)


def _layer0_and_hgrn_inproj(h2d, B, S, attn_norm, w_in, dq_g, dk_g, fq_g, fk_g, lq1, lk1, lq2, lk2, out_g,
                            w_out, ffn_norm, w_gate, w_up, w_down, hgrn_norm, hgrn_w_in, lb_logits):
    T = B * S
    tm = min(512, T)
    slopes = _alibi_slopes()
    *cls, dproj = _attn_inproj(h2d, attn_norm.reshape(1, D_MODEL).astype(F32), w_in.astype(BF16),
                               _attn_col_gain(dq_g, dk_g, fq_g, fk_g), tm)
    outs, lses = [], []
    for g, (window, dilation) in enumerate(DIL_GROUPS):
        assert window // dilation == QBLOCK and S % (dilation * QBLOCK) == 0 and tm % dilation == 0
        o, lse = _dilated_group(cls[g], B, S, dilation, tuple(slopes[4 * g:4 * g + 4]))
        outs.extend(o)
        lses.extend(lse)

    lam_init = 0.8 - 0.6 * math.exp(-0.3 * 0)
    lam = (jnp.exp(jnp.sum(lq1.astype(F32) * lk1.astype(F32)))
           - jnp.exp(jnp.sum(lq2.astype(F32) * lk2.astype(F32))) + lam_init).reshape(1)
    sl = jnp.asarray(slopes[N_DIL_HEADS:], F32) * LOG2E
    s_hi, s_lo = _bf16_pieces(sl, 2)
    slope_feat = jnp.zeros((8, LANES), F32)
    slope_feat = slope_feat.at[:N_DIFF_HEADS, 0].set(s_hi.astype(F32)).at[:N_DIFF_HEADS, 1].set(s_hi.astype(F32))
    slope_feat = slope_feat.at[:N_DIFF_HEADS, 2].set(s_lo.astype(F32)).at[:N_DIFF_HEADS, 3].set(s_lo.astype(F32))
    p_hi, p_lo = _bf16_pieces(jnp.arange(S, dtype=F32), 2)
    pos_feat = jnp.zeros((S, LANES), BF16)
    pos_feat = pos_feat.at[:, 0].set(p_hi).at[:, 1].set(p_lo).at[:, 2].set(p_hi).at[:, 3].set(p_lo)
    out_gain = (jnp.tile(out_g.astype(F32), COLB // HEAD_DIM) * (1.0 - lam_init)).reshape(1, COLB)
    diff = _diff_attention(dproj.reshape(B, S, _QKV), lam, slope_feat, pos_feat, out_gain, min(512, S)).reshape(T, COLB)

    h2, *gates = _layer0_tail(
        h2d, outs, lses, diff, w_out.astype(BF16), ffn_norm.reshape(1, D_MODEL).astype(F32),
        w_gate.astype(BF16), w_up.astype(BF16), w_down.astype(BF16),
        hgrn_norm.reshape(1, D_MODEL).astype(F32), hgrn_w_in.astype(BF16),
        _lower_bound_rows(lb_logits, 1), min(256, T))
    return h2, gates


def _layer1_rest(h2d, gates, B, S, out_gain, w_out, moe_norm, w_router, w_gate, w_up, w_down):
    q, lf, k, v, g = gates
    o = _hgrn_recurrence(q, lf, k, v, g, out_gain.reshape(1, LANES).astype(F32), B, S, min(256, S))
    return _moe_block(h2d, o, w_out, moe_norm, w_router, w_gate, w_up, w_down)


def _moe_block(h2d, mixer_out, w_out, moe_norm, w_router, w_gate, w_up, w_down):
    T = h2d.shape[0]
    tme = 512
    w_pad = jnp.zeros((D_MODEL, LANES), F32).at[:, :N_EXPERTS].set(w_router.astype(F32))
    w_hi, w_lo = _bf16_pieces(w_pad, 2)
    h2d, xn, info, counts = _proj_router(h2d, mixer_out, w_out.astype(BF16),
                                         moe_norm.reshape(1, D_MODEL).astype(F32), w_hi, w_lo, min(256, T))
    cnt = counts[0, :N_EXPERTS].astype(I32)
    padded = ((cnt + tme - 1) // tme) * tme
    ends = jnp.cumsum(padded)
    starts = ends - padded
    experts = info[:, 0:2].astype(I32)
    ranks = info[:, 2:4].astype(I32)
    pos = (starts[experts] + ranks).reshape(-1)
    n_rows = 2 * T + N_EXPERTS * tme
    n_tiles = n_rows // tme
    tile_start = jnp.arange(n_tiles, dtype=I32) * tme
    tile_expert = jnp.minimum(jnp.sum((ends[None, :] <= tile_start[:, None]).astype(I32), axis=1), N_EXPERTS - 1)
    n_active = (ends[-1] // tme).astype(I32).reshape(1)
    tail = jnp.minimum(ends[-1] + jnp.arange(N_EXPERTS, dtype=I32) * tme, n_rows - tme)
    zero_tiles = jnp.concatenate([jnp.maximum(ends - tme, 0), tail]).astype(I32)
    xs = _dispatch(xn, pos, zero_tiles, n_rows, min(256, T), tme)
    y = _experts(xs, tile_expert, n_active, w_gate.astype(BF16), w_up.astype(BF16), w_down.astype(BF16),
                 tme, 2)
    return _combine(h2d, info, pos, y, min(256, T))


def kernel(x, attn_norm, attn_w_in, dil_q_gain, dil_k_gain, diff_q_gain, diff_k_gain, diff_lambda_q1, diff_lambda_k1, diff_lambda_q2, diff_lambda_k2, diff_out_gain, attn_w_out, ffn_norm, ffn_w_gate, ffn_w_up, ffn_w_down, hgrn_norm, hgrn_w_in, hgrn_lb_logits, hgrn_out_gain, hgrn_w_out, moe_norm, moe_w_router, moe_w_gate, moe_w_up, moe_w_down):
    B, S, D = x.shape
    assert D == D_MODEL
    h = x.astype(F32).reshape(B * S, D)
    h, gates = _layer0_and_hgrn_inproj(
        h, B, S, attn_norm[0], attn_w_in[0], dil_q_gain[0], dil_k_gain[0], diff_q_gain[0], diff_k_gain[0],
        diff_lambda_q1[0], diff_lambda_k1[0], diff_lambda_q2[0], diff_lambda_k2[0], diff_out_gain[0],
        attn_w_out[0], ffn_norm[0], ffn_w_gate[0], ffn_w_up[0], ffn_w_down[0],
        hgrn_norm[0], hgrn_w_in[0], hgrn_lb_logits)
    h = _layer1_rest(h, gates, B, S, hgrn_out_gain[0], hgrn_w_out[0], moe_norm[0], moe_w_router[0],
                     moe_w_gate[0], moe_w_up[0], moe_w_down[0])
    return h.reshape(B, S, D).astype(x.dtype)
```

```python
import functools
import math

import jax
import jax.numpy as jnp
from jax import lax
from jax.experimental import pallas as pl
from jax.experimental.pallas import tpu as pltpu

F32 = jnp.float32
BF16 = jnp.bfloat16
I32 = jnp.int32

D_MODEL = 1024
HEAD_DIM = 64
N_ATTN_HEADS = 16
N_DIL_HEADS = 12
N_DIFF_HEADS = 4
DIL_GROUPS = ((128, 1), (512, 4), (2048, 16))
DIFF_QK_DIM = 32
ATTN_IN_WIDTH = 3072
QBLOCK = 128
HGRN_HEADS = 8
HGRN_CHUNK = 64
HGRN_ROWS = 256
DEC_ROWS = 8
D_FF = 2816
N_EXPERTS = 8
D_FF_EXPERT = 3584
EPS = 1e-6
NEG_INF = -1e30
LOG2E = 1.4426950408889634
LANES = 128
COLB = 256
VMEM_LIMIT = 56 * 1024 * 1024


def _cparams(*sem):
    return pltpu.CompilerParams(dimension_semantics=sem, vmem_limit_bytes=VMEM_LIMIT)


def _split_dot(x, m, terms):
    acc = None
    r = x
    for t in range(terms):
        part = r.astype(BF16)
        d = jnp.dot(part, m, preferred_element_type=F32)
        acc = d if acc is None else acc + d
        if t + 1 < terms:
            r = r - part.astype(F32)
    return acc


def _seg_matrix(n, seg):
    i = jnp.arange(n)
    return (i[:, None] // seg == i[None, :] // seg).astype(BF16)


def _rms_rows(x, gain_row):
    return x * lax.rsqrt(jnp.mean(x * x, axis=-1, keepdims=True) + EPS) * gain_row


def _sigmoid(x):
    return 1.0 / (1.0 + jnp.exp(-x))


def _silu(x):
    return x * _sigmoid(x)


_ATTN_NORM_SEG = (64, 64, 64, 64, 64, 64, 0, 0, 0, 32, 32, 0)
_QKV = 3 * COLB


def _attn_inproj_kernel(h_ref, ng_ref, w_ref, cg_ref, s64_ref, s32_ref,
                        c0_ref, c1_ref, c2_ref, df_ref, y_ref, *, tm):
    cls_refs = (c0_ref, c1_ref, c2_ref)
    xb = _rms_rows(h_ref[...], ng_ref[...]).astype(BF16)
    blocks = range(len(_ATTN_NORM_SEG))
    ys = [jnp.dot(xb, w_ref[:, c * COLB:(c + 1) * COLB], preferred_element_type=F32) for c in blocks]
    ms = {c: _split_dot(ys[c] * ys[c], s64_ref[...] if seg == 64 else s32_ref[...], 2) * (1.0 / seg)
          for c, seg in enumerate(_ATTN_NORM_SEG) if seg}
    ys = [ys[c] * lax.rsqrt(ms[c] + EPS) * cg_ref[c:c + 1, :] if c in ms else ys[c] for c in blocks]
    strided = []
    for c in blocks:
        part, g = divmod(c, 3)
        if part == 3:
            df_ref[:, g * COLB:(g + 1) * COLB] = ys[c].astype(BF16)
        elif DIL_GROUPS[g][1] == 1:
            cls_refs[g][:, part * COLB:(part + 1) * COLB] = ys[c].astype(BF16)
        else:
            slot = len(strided)
            y_ref[slot, 0] = ys[c][:, :LANES]
            y_ref[slot, 1] = ys[c][:, LANES:]
            strided.append((slot, part, g))
    for slot, part, g in strided:
        d = DIL_GROUPS[g][1]
        for r in range(d):
            col = r * _QKV + part * COLB
            rows = pl.ds(r, tm // d, stride=d)
            cls_refs[g][:, col:col + COLB] = jnp.concatenate(
                [y_ref[slot, 0, rows, :], y_ref[slot, 1, rows, :]], axis=1).astype(BF16)


def _attn_inproj(h2d, norm_gain, w_bf, col_gain, tm):
    T = h2d.shape[0]
    const = lambda i: (0, 0)
    row = lambda i: (i, 0)
    dils = [d for _, d in DIL_GROUPS]
    kern = functools.partial(_attn_inproj_kernel, tm=tm)
    return pl.pallas_call(
        kern,
        grid=(T // tm,),
        in_specs=[
            pl.BlockSpec((tm, D_MODEL), row),
            pl.BlockSpec((1, D_MODEL), const),
            pl.BlockSpec((D_MODEL, ATTN_IN_WIDTH), const),
            pl.BlockSpec((ATTN_IN_WIDTH // COLB, COLB), const),
            pl.BlockSpec((COLB, COLB), const),
            pl.BlockSpec((COLB, COLB), const),
        ],
        out_specs=[pl.BlockSpec((tm // d, d * _QKV), row) for d in dils] + [pl.BlockSpec((tm, _QKV), row)],
        out_shape=[jax.ShapeDtypeStruct((T // d, d * _QKV), BF16) for d in dils]
                  + [jax.ShapeDtypeStruct((T, _QKV), BF16)],
        scratch_shapes=[pltpu.VMEM((3 * sum(d > 1 for d in dils), 2, tm, LANES), F32)],
        compiler_params=_cparams("parallel"),
        name="attn_inproj",
    )(h2d, norm_gain, w_bf, col_gain, _seg_matrix(COLB, 64), _seg_matrix(COLB, 32))


def _dil_kernel(q_ref, kp_ref, kc_ref, vp_ref, vc_ref, o0_ref, o1_ref, l0_ref, l1_ref,
                *, slopes, dilation, ub):
    u = pl.program_id(1)
    r = pl.program_id(2)
    q = q_ref[...]
    kcat = jnp.concatenate([kp_ref[...], kc_ref[...]], axis=0)
    vcat = jnp.concatenate([vp_ref[...], vc_ref[...]], axis=0)
    qi = lax.broadcasted_iota(I32, (QBLOCK, 2 * QBLOCK), 0)
    kj = lax.broadcasted_iota(I32, (QBLOCK, 2 * QBLOCK), 1)
    step = qi + QBLOCK - kj
    in_window = (step >= 0) & (step <= QBLOCK)
    first_window = in_window & ((kj >= QBLOCK) | (u > 0))
    stepf = step.astype(F32)
    bias = [(-slopes[h] * dilation * LOG2E) * stepf for h in range(COLB // HEAD_DIM)]
    mask_bias = [jnp.where(in_window, b, NEG_INF) for b in bias]
    mask_bias_first = [jnp.where(first_window, b, NEG_INF) for b in bias]
    lane = lax.broadcasted_iota(I32, (QBLOCK, LANES), 1)
    low_half = lane < HEAD_DIM
    jobs = [(i, pair, hh) for i in range(ub // QBLOCK) for pair in range(2) for hh in range(2)]
    nt = (((1,), (1,)), ((), ()))

    def masked_q(i, pair, hh):
        qp = q[i * QBLOCK:(i + 1) * QBLOCK, pair * LANES:(pair + 1) * LANES]
        return jnp.where(low_half if hh == 0 else ~low_half, qp, jnp.zeros_like(qp))

    scores = [lax.dot_general(masked_q(i, pair, hh), kcat[i * QBLOCK:(i + 2) * QBLOCK, pair * LANES:(pair + 1) * LANES],
                              nt, preferred_element_type=F32) for i, pair, hh in jobs]
    stats = []
    for (i, pair, hh), s in zip(jobs, scores):
        s = s + (mask_bias_first if i == 0 else mask_bias)[pair * 2 + hh]
        m = jnp.max(s, axis=-1, keepdims=True)
        e = jnp.exp2(s - m)
        stats.append((m, jnp.sum(e, axis=-1, keepdims=True), e.astype(BF16)))
    pvs = [jnp.dot(e, vcat[i * QBLOCK:(i + 2) * QBLOCK, pair * LANES:(pair + 1) * LANES], preferred_element_type=F32)
           for (i, pair, hh), (_, _, e) in zip(jobs, stats)]
    for n in range(0, len(jobs), 2):
        i, pair, _ = jobs[n]
        outs = [pvs[n + hh] / stats[n + hh][1] for hh in range(2)]
        lses = [jnp.broadcast_to(stats[n + hh][0] + jnp.log2(stats[n + hh][1]), (QBLOCK, LANES)) for hh in range(2)]
        if dilation == 1:
            rows = slice(i * QBLOCK, (i + 1) * QBLOCK)
        else:
            rows = pl.ds(r + i * QBLOCK * dilation, QBLOCK, stride=dilation)
        (o0_ref, o1_ref)[pair][rows, :] = jnp.where(low_half, outs[0], outs[1])
        (l0_ref, l1_ref)[pair][rows, :] = jnp.where(low_half, lses[0], lses[1])


def _dilated_group(cls, B, S, dilation, slopes):
    U = S // dilation
    ub = min(512, U)
    sub = ub // QBLOCK
    view = cls.reshape(B, U, dilation * _QKV)
    prev = lambda u: jnp.maximum(u * sub - 1, 0)
    kern = functools.partial(_dil_kernel, slopes=slopes, dilation=dilation, ub=ub)
    out_spec = pl.BlockSpec((None, ub * dilation, LANES), lambda b, u, r: (b, u, 0))
    res = pl.pallas_call(
        kern,
        grid=(B, U // ub, dilation),
        in_specs=[
            pl.BlockSpec((None, ub, COLB), lambda b, u, r: (b, u, 3 * r)),
            pl.BlockSpec((None, QBLOCK, COLB), lambda b, u, r: (b, prev(u), 3 * r + 1)),
            pl.BlockSpec((None, ub, COLB), lambda b, u, r: (b, u, 3 * r + 1)),
            pl.BlockSpec((None, QBLOCK, COLB), lambda b, u, r: (b, prev(u), 3 * r + 2)),
            pl.BlockSpec((None, ub, COLB), lambda b, u, r: (b, u, 3 * r + 2)),
        ],
        out_specs=[out_spec] * 4,
        out_shape=[jax.ShapeDtypeStruct((B, S, LANES), F32)] * 4,
        compiler_params=_cparams("parallel", "parallel", "arbitrary"),
        name=f"dilated_attn_d{dilation}",
    )(view, view, view, view, view)
    res = [a.reshape(B * S, LANES) for a in res]
    return res[:2], res[2:]


def _diff_kernel(lam_ref, q_ref, k_ref, v_ref, pos_ref, sl_ref, og_ref, s64_ref, o_ref,
                 m_ref, l_ref, acc_ref, qa_ref, *, tq):
    i = pl.program_id(1)
    q = q_ref[...]
    lane = lax.broadcasted_iota(I32, (tq, LANES), 1)
    low_half = lane < HEAD_DIM
    for pair in range(2):
        qp = q[:, pair * LANES:(pair + 1) * LANES]
        for hh in range(2):
            feat = jnp.broadcast_to(sl_ref[pair * 2 + hh:pair * 2 + hh + 1, :], (tq, LANES)).astype(BF16)
            for mu in range(2):
                lo = hh * HEAD_DIM + mu * DIFF_QK_DIM
                sel = (lane >= lo) & (lane < lo + DIFF_QK_DIM)
                r = hh * 2 + mu
                qa_ref[pair, r * tq:(r + 1) * tq, :] = jnp.concatenate(
                    [jnp.where(sel, qp, jnp.zeros_like(qp)), feat], axis=1)

    m_ref[...] = jnp.full(m_ref.shape, NEG_INF, F32)
    l_ref[...] = jnp.zeros(l_ref.shape, F32)
    acc_ref[...] = jnp.zeros(acc_ref.shape, F32)

    def scores(j, nk):
        ks = pl.multiple_of(j * tq, tq)
        kblk = k_ref[pl.ds(ks, nk * tq), :]
        pblk = pos_ref[pl.ds(ks, nk * tq), :]
        s_alls = []
        for pair in range(2):
            kaug = jnp.concatenate([kblk[:, pair * LANES:(pair + 1) * LANES], pblk], axis=1)
            s_alls.append(lax.dot_general(qa_ref[pair], kaug, (((1,), (1,)), ((), ())),
                                          preferred_element_type=F32))
        return s_alls, v_ref[pl.ds(ks, nk * tq), :]

    def finish(s_alls, vblk, masked):
        tk = vblk.shape[0]
        ones = jnp.ones((tk, LANES), BF16)
        if masked:
            row = lax.broadcasted_iota(I32, (tq, tk), 0)
            col = lax.broadcasted_iota(I32, (tq, tk), 1)
            causal = col <= row
        probs, alphas = [], []
        for pair in range(2):
            ps = []
            for r in range(4):
                idx = pair * 4 + r
                s = s_alls[pair][r * tq:(r + 1) * tq]
                if masked:
                    s = jnp.where(causal, s, NEG_INF)
                m_prev = m_ref[idx]
                m_next = jnp.maximum(m_prev, jnp.max(s, axis=-1, keepdims=True))
                m_ref[idx] = m_next
                alphas.append(jnp.exp2(m_prev - m_next))
                ps.append(jnp.exp2(s - jnp.concatenate([m_next] * (tk // LANES), axis=1)).astype(BF16))
            probs.append(jnp.concatenate(ps, axis=0))
        for pair in range(2):
            vaug = jnp.concatenate([vblk[:, pair * LANES:(pair + 1) * LANES], ones], axis=1)
            pv = jnp.dot(probs[pair], vaug, preferred_element_type=F32)
            for r in range(4):
                idx = pair * 4 + r
                part = pv[r * tq:(r + 1) * tq]
                acc_ref[idx] = alphas[idx] * acc_ref[idx] + part[:, :LANES]
                l_ref[idx] = alphas[idx] * l_ref[idx] + part[:, LANES:]

    def two_full_blocks(jj, carry):
        finish(*scores(2 * jj, 2), False)
        return carry

    lax.fori_loop(0, i // 2, two_full_blocks, 0)

    @pl.when(i % 2 == 1)
    def _():
        finish(*scores(i - 1, 1), False)

    finish(*scores(i, 1), True)

    lam = lam_ref[0]
    for pair in range(2):
        o = None
        for mu in range(2):
            lo_idx, hi_idx = pair * 4 + mu, pair * 4 + 2 + mu
            term = jnp.where(low_half, acc_ref[lo_idx] / l_ref[lo_idx], acc_ref[hi_idx] / l_ref[hi_idx])
            o = term if mu == 0 else o - lam * term
        ms = _split_dot(o * o, s64_ref[...], 2) * (1.0 / HEAD_DIM)
        o = o * lax.rsqrt(ms + EPS) * og_ref[:, pair * LANES:(pair + 1) * LANES]
        o_ref[:, pair * LANES:(pair + 1) * LANES] = o.astype(BF16)


def _diff_attention(proj, lam, slope_feat, pos_feat, out_gain, tq):
    B, S, W = proj.shape
    kern = functools.partial(_diff_kernel, tq=tq)
    return pl.pallas_call(
        kern,
        grid=(B, S // tq),
        in_specs=[
            pl.BlockSpec(memory_space=pltpu.SMEM),
            pl.BlockSpec((None, tq, COLB), lambda b, i: (b, i, 0)),
            pl.BlockSpec((None, S, COLB), lambda b, i: (b, 0, 1), pipeline_mode=pl.Buffered(1)),
            pl.BlockSpec((None, S, COLB), lambda b, i: (b, 0, 2), pipeline_mode=pl.Buffered(1)),
            pl.BlockSpec((S, LANES), lambda b, i: (0, 0), pipeline_mode=pl.Buffered(1)),
            pl.BlockSpec((8, LANES), lambda b, i: (0, 0)),
            pl.BlockSpec((1, COLB), lambda b, i: (0, 0)),
            pl.BlockSpec((LANES, LANES), lambda b, i: (0, 0)),
        ],
        out_specs=pl.BlockSpec((None, tq, COLB), lambda b, i: (b, i, 0)),
        out_shape=jax.ShapeDtypeStruct((B, S, COLB), BF16),
        scratch_shapes=[
            pltpu.VMEM((8, tq, LANES), F32),
            pltpu.VMEM((8, tq, LANES), F32),
            pltpu.VMEM((8, tq, LANES), F32),
            pltpu.VMEM((2, 4 * tq, 2 * LANES), BF16),
        ],
        compiler_params=_cparams("parallel", "parallel"),
        name="diff_attn",
    )(lam, proj, proj, proj, pos_feat, slope_feat, out_gain, _seg_matrix(LANES, HEAD_DIM))


def _merge_heads(o_refs, l_refs, d_ref):
    n = len(o_refs)
    pieces = [None] * n
    for pair in range(2):
        ls = [l_refs[2 * g + pair][...] for g in range(n // 2)]
        mx = functools.reduce(jnp.maximum, ls)
        es = [jnp.exp2(l - mx) for l in ls]
        inv = 1.0 / functools.reduce(jnp.add, es)
        for g, e in enumerate(es):
            pieces[2 * g + pair] = (e * inv * o_refs[2 * g + pair][...]).astype(BF16)
    return jnp.concatenate(pieces + [d_ref[...]], axis=1)


def _split_dot_lhs(m, x):
    hi = x.astype(BF16)
    lo = (x - hi.astype(F32)).astype(BF16)
    return jnp.dot(m, hi, preferred_element_type=F32) + jnp.dot(m, lo, preferred_element_type=F32)


def _chunk_mask(rows):
    ti = lax.broadcasted_iota(I32, (rows, rows), 0)
    si = lax.broadcasted_iota(I32, (rows, rows), 1)
    return (si <= ti) & (si >= (ti // HGRN_CHUNK) * HGRN_CHUNK)


def _hgrn_gates(xb, w_ref, lb_ref, qs_ref, ks_ref, qi_ref, kst_ref, v_ref, g_ref, dec_ref):
    C = HGRN_CHUNK
    tm = xb.shape[0]
    nc = tm // C
    yq, fl, yv, yg = [jnp.dot(xb, w_ref[:, c * D_MODEL:(c + 1) * D_MODEL], preferred_element_type=F32)
                      for c in range(4)]
    v_ref[...] = yv.astype(BF16)
    g_ref[...] = _sigmoid(yg).astype(BF16)
    q = _silu(yq)
    log_lb, log1m_lb, one_m_lb = lb_ref[0:1, :], lb_ref[1:2, :], lb_ref[2:3, :]
    t = jnp.exp(-jnp.abs(fl))
    r = 1.0 / (1.0 + t)
    c = log1m_lb + (jnp.minimum(fl, 0.0) + jnp.log(r))
    lf = jnp.maximum(log_lb, c) + jnp.log(1.0 + jnp.exp(-jnp.abs(log_lb - c)))
    k = one_m_lb * jnp.where(fl >= 0.0, t * r, r)
    tri = jnp.where(_chunk_mask(tm), 1.0, 0.0).astype(BF16)
    b = _split_dot_lhs(tri, lf)
    chunk_row = lambda i: jnp.concatenate(
        [jnp.broadcast_to(b[n * C + i:n * C + i + 1, :], (C, D_MODEL)) for n in range(nc)], axis=0)
    b_mid = chunk_row(C // 2)
    b_last = chunk_row(C - 1)
    qs_ref[...] = (q * jnp.exp(b - b_mid)).astype(BF16)
    ks_ref[...] = (k * jnp.exp(b_mid - b)).astype(BF16)
    qi_ref[...] = (q * jnp.exp(b)).astype(BF16)
    kst_ref[...] = (k * jnp.exp(b_last - b)).astype(BF16)
    last_rows = jnp.concatenate([b[n * C + C - 1:n * C + C, :] for n in range(nc)], axis=0)
    dec_ref[...] = jnp.concatenate(
        [jnp.exp(last_rows), jnp.zeros((dec_ref.shape[0] - nc, D_MODEL), F32)], axis=0)


def _layer0_tail_kernel(h_ref, *refs):
    n = 2 * len(DIL_GROUPS)
    o_refs, l_refs = refs[:n], refs[n:2 * n]
    (d_ref, wo_ref, fng_ref, wg_ref, wu_ref, wd_ref, hng_ref, wh_ref, lb_ref,
     h2_ref, qs_ref, ks_ref, qi_ref, kst_ref, v_ref, g_ref, dec_ref) = refs[2 * n:]
    mixed = _merge_heads(o_refs, l_refs, d_ref)
    h1 = h_ref[...] + jnp.dot(mixed, wo_ref[...], preferred_element_type=F32)
    xb = _rms_rows(h1, fng_ref[...]).astype(BF16)
    g = jnp.dot(xb, wg_ref[...], preferred_element_type=F32)
    u = jnp.dot(xb, wu_ref[...], preferred_element_type=F32)
    a = (_silu(g) * u).astype(BF16)
    h2 = h1 + jnp.dot(a, wd_ref[...], preferred_element_type=F32)
    h2_ref[...] = h2
    _hgrn_gates(_rms_rows(h2, hng_ref[...]).astype(BF16), wh_ref, lb_ref,
                qs_ref, ks_ref, qi_ref, kst_ref, v_ref, g_ref, dec_ref)


def _layer0_tail(h2d, outs, lses, diff, w_out, ffn_gain, wg, wu, wd, hgrn_gain, w_in, lb_rows, tm):
    T = h2d.shape[0]
    row = lambda i: (i, 0)
    full = pl.BlockSpec((tm, D_MODEL), row)
    half = pl.BlockSpec((tm, LANES), row)
    held = lambda shape: pl.BlockSpec(shape, lambda i: (0, 0), pipeline_mode=pl.Buffered(1))
    bf = jax.ShapeDtypeStruct((T, D_MODEL), BF16)
    f32 = jax.ShapeDtypeStruct((T, D_MODEL), F32)
    return pl.pallas_call(
        _layer0_tail_kernel,
        grid=(T // tm,),
        in_specs=[full] + [half] * (len(outs) + len(lses))
                 + [pl.BlockSpec((tm, COLB), row), held((D_MODEL, D_MODEL)), held((1, D_MODEL)),
                    held((D_MODEL, D_FF)), held((D_MODEL, D_FF)), held((D_FF, D_MODEL)),
                    held((1, D_MODEL)), held((D_MODEL, 4 * D_MODEL)), held((8, D_MODEL))],
        out_specs=[full] * 7 + [pl.BlockSpec((DEC_ROWS, D_MODEL), row)],
        out_shape=[f32] + [bf] * 6 + [jax.ShapeDtypeStruct((T // tm * DEC_ROWS, D_MODEL), F32)],
        compiler_params=_cparams("parallel"),
        name="layer0_tail_hgrn_inproj",
    )(h2d, *outs, *lses, diff, w_out, ffn_gain, wg, wu, wd, hgrn_gain, w_in, lb_rows)


def _hgrn_kernel(qs_ref, ks_ref, qi_ref, kst_ref, v_ref, g_ref, dec_ref, og_ref, o_ref, st_ref, *, rows):
    @pl.when(pl.program_id(1) == 0)
    def _():
        st_ref[...] = jnp.zeros(st_ref.shape, F32)

    C = HGRN_CHUNK
    nc = rows // C
    heads = range(HGRN_HEADS)
    lanes = [slice(hd * LANES, (hd + 1) * LANES) for hd in heads]
    causal = _chunk_mask(rows)
    nt = (((1,), (1,)), ((), ()))
    tn = (((0,), (0,)), ((), ()))
    scores = [lax.dot_general(qs_ref[:, lanes[hd]], ks_ref[:, lanes[hd]], nt, preferred_element_type=F32)
              for hd in heads]
    incs = [[lax.dot_general(v_ref[c * C:(c + 1) * C, lanes[hd]], kst_ref[c * C:(c + 1) * C, lanes[hd]], tn,
                             preferred_element_type=F32) for c in range(nc)] for hd in heads]
    probs = [jnp.where(causal, s, 0.0).astype(BF16) for s in scores]
    outs = [jnp.dot(probs[hd], v_ref[:, lanes[hd]], preferred_element_type=F32) for hd in heads]
    states = []
    for hd in heads:
        st, per_chunk = st_ref[hd], []
        for c in range(nc):
            per_chunk.append(st.astype(BF16))
            st = dec_ref[c:c + 1, lanes[hd]] * st + incs[hd][c]
        st_ref[hd] = st
        states.append(per_chunk)
    for hd in heads:
        inter = [lax.dot_general(qi_ref[c * C:(c + 1) * C, lanes[hd]], states[hd][c], nt,
                                 preferred_element_type=F32) for c in range(nc)]
        o = outs[hd] + jnp.concatenate(inter, axis=0)
        o = o * lax.rsqrt(jnp.mean(o * o, axis=-1, keepdims=True) + EPS) * og_ref[...]
        o_ref[:, lanes[hd]] = (o * g_ref[:, lanes[hd]].astype(F32)).astype(BF16)


def _hgrn_recurrence(gates, out_gain, B, S, rows):
    T = B * S
    nseq = S // rows
    row = pl.BlockSpec((rows, D_MODEL), lambda b, s: (b * nseq + s, 0))
    dec = pl.BlockSpec((DEC_ROWS, D_MODEL), lambda b, s: (b * nseq + s, 0))
    kern = functools.partial(_hgrn_kernel, rows=rows)
    return pl.pallas_call(
        kern,
        grid=(B, nseq),
        in_specs=[row] * 6 + [dec, pl.BlockSpec((1, LANES), lambda b, s: (0, 0))],
        out_specs=row,
        out_shape=jax.ShapeDtypeStruct((T, D_MODEL), BF16),
        scratch_shapes=[pltpu.VMEM((HGRN_HEADS, LANES, LANES), F32)],
        compiler_params=_cparams("parallel", "arbitrary"),
        name="hgrn_recurrence",
    )(*gates, out_gain)


def _proj_router_kernel(h_ref, x_ref, wo_ref, ng_ref, whi_ref, wlo_ref,
                        h2_ref, xn_ref, info_ref, cnt_ref, carry_ref, *, tm):
    @pl.when(pl.program_id(0) == 0)
    def _():
        carry_ref[...] = jnp.zeros(carry_ref.shape, F32)

    h2 = h_ref[...] + jnp.dot(x_ref[...], wo_ref[...], preferred_element_type=F32)
    h2_ref[...] = h2
    xn = _rms_rows(h2, ng_ref[...])
    xn_ref[...] = xn
    xhi = xn.astype(BF16)
    xlo = (xn - xhi.astype(F32)).astype(BF16)
    logits = (jnp.dot(xhi, whi_ref[...], preferred_element_type=F32)
              + jnp.dot(xhi, wlo_ref[...], preferred_element_type=F32)
              + jnp.dot(xlo, whi_ref[...], preferred_element_type=F32))
    lane = lax.broadcasted_iota(I32, (tm, LANES), 1)
    lanef = lane.astype(F32)
    logits = jnp.where(lane < N_EXPERTS, logits, -jnp.inf)
    m1 = jnp.max(logits, axis=-1, keepdims=True)
    i1 = jnp.min(jnp.where(logits == m1, lanef, float(LANES)), axis=-1, keepdims=True)
    oh1 = lanef == i1
    rest = jnp.where(oh1, -jnp.inf, logits)
    m2 = jnp.max(rest, axis=-1, keepdims=True)
    i2 = jnp.min(jnp.where(rest == m2, lanef, float(LANES)), axis=-1, keepdims=True)
    oh2 = lanef == i2
    e = jnp.exp(m2 - m1)
    w1 = 1.0 / (1.0 + e)
    w2 = e * w1

    chosen = jnp.where(oh1 | oh2, 1.0, 0.0)
    r = lax.broadcasted_iota(I32, (tm, tm), 0)
    c = lax.broadcasted_iota(I32, (tm, tm), 1)
    before = jnp.where(c < r, 1.0, 0.0).astype(BF16)
    excl = jnp.dot(before, chosen.astype(BF16), preferred_element_type=F32) + carry_ref[0:1, :]
    rank1 = jnp.sum(jnp.where(oh1, excl, 0.0), axis=-1, keepdims=True)
    rank2 = jnp.sum(jnp.where(oh2, excl, 0.0), axis=-1, keepdims=True)
    total = carry_ref[0:1, :] + jnp.sum(chosen, axis=0, keepdims=True)
    carry_ref[...] = jnp.broadcast_to(total, carry_ref.shape)
    cnt_ref[...] = jnp.broadcast_to(total, cnt_ref.shape)

    info = jnp.where(lane == 0, i1, 0.0)
    info = jnp.where(lane == 1, i2, info)
    info = jnp.where(lane == 2, rank1, info)
    info = jnp.where(lane == 3, rank2, info)
    info = jnp.where(lane == 4, w1, info)
    info = jnp.where(lane == 5, w2, info)
    info_ref[...] = info


def _proj_router(h2d, x_bf, w_out, norm_gain, w_hi, w_lo, tm):
    T = h2d.shape[0]
    const = lambda i: (0, 0)
    row = lambda i: (i, 0)
    full = pl.BlockSpec((tm, D_MODEL), row)
    kern = functools.partial(_proj_router_kernel, tm=tm)
    return pl.pallas_call(
        kern,
        grid=(T // tm,),
        in_specs=[full, full, pl.BlockSpec((D_MODEL, D_MODEL), const), pl.BlockSpec((1, D_MODEL), const),
                  pl.BlockSpec((D_MODEL, LANES), const), pl.BlockSpec((D_MODEL, LANES), const)],
        out_specs=[full, full, pl.BlockSpec((tm, LANES), row), pl.BlockSpec((8, LANES), const)],
        out_shape=[
            jax.ShapeDtypeStruct((T, D_MODEL), F32),
            jax.ShapeDtypeStruct((T, D_MODEL), F32),
            jax.ShapeDtypeStruct((T, LANES), F32),
            jax.ShapeDtypeStruct((8, LANES), F32),
        ],
        scratch_shapes=[pltpu.VMEM((8, LANES), F32)],
        compiler_params=_cparams("arbitrary"),
        name="hgrn_outproj_moe_router",
    )(h2d, x_bf, w_out, norm_gain, w_hi, w_lo)


def _dispatch_kernel(zs_ref, pos_ref, xn_ref, xs_hbm, zero_ref, sem, zsem, *, tt, tme):
    @pl.when(pl.program_id(0) == 0)
    def _():
        zero_ref[...] = jnp.zeros(zero_ref.shape, F32)
        for e in range(2 * N_EXPERTS):
            row0 = pl.multiple_of(zs_ref[e], tme)
            fill = pltpu.make_async_copy(zero_ref, xs_hbm.at[pl.ds(row0, tme)], zsem)
            fill.start()
            fill.wait()

    def copy(t, s):
        return pltpu.make_async_copy(
            xn_ref.at[pl.ds(t, 1)], xs_hbm.at[pl.ds(pos_ref[0, 0, 2 * t + s], 1)], sem)

    def start(t, carry):
        copy(t, 0).start()
        copy(t, 1).start()
        return carry

    def wait(t, carry):
        copy(t, 0).wait()
        copy(t, 1).wait()
        return carry

    lax.fori_loop(0, tt, start, 0, unroll=8)
    lax.fori_loop(0, tt, wait, 0, unroll=8)


def _dispatch(xn, pos, last_tile_start, n_rows, tt, tme):
    T = xn.shape[0]
    pos3 = pos.reshape(T // tt, 1, 2 * tt)
    kern = functools.partial(_dispatch_kernel, tt=tt, tme=tme)
    grid_spec = pltpu.PrefetchScalarGridSpec(
        num_scalar_prefetch=1,
        grid=(T // tt,),
        in_specs=[
            pl.BlockSpec((1, 1, 2 * tt), lambda i, zs: (i, 0, 0), memory_space=pltpu.SMEM),
            pl.BlockSpec((tt, D_MODEL), lambda i, zs: (i, 0)),
        ],
        out_specs=pl.BlockSpec(memory_space=pl.ANY),
        scratch_shapes=[pltpu.VMEM((tme, D_MODEL), F32), pltpu.SemaphoreType.DMA(()),
                        pltpu.SemaphoreType.DMA(())],
    )
    return pl.pallas_call(
        kern,
        grid_spec=grid_spec,
        out_shape=jax.ShapeDtypeStruct((n_rows, D_MODEL), F32),
        compiler_params=_cparams("arbitrary"),
        name="moe_dispatch",
    )(last_tile_start, pos3, xn)


def _expert_kernel(te_ref, na_ref, xs_ref, wg_ref, wu_ref, wd_ref, y_ref, *, nchunk):
    del te_ref
    active = pl.program_id(0) < na_ref[0]

    @pl.when(jnp.logical_not(active))
    def _():
        y_ref[...] = jnp.zeros(y_ref.shape, F32)

    @pl.when(active)
    def _():
        xb = xs_ref[...].astype(BF16)
        tf = D_FF_EXPERT // nchunk
        acc = None
        for c in range(nchunk):
            cols = slice(c * tf, (c + 1) * tf)
            g = jnp.dot(xb, wg_ref[0, :, cols], preferred_element_type=F32)
            u = jnp.dot(xb, wu_ref[0, :, cols], preferred_element_type=F32)
            a = (_silu(g) * u).astype(BF16)
            d = jnp.dot(a, wd_ref[0, cols, :], preferred_element_type=F32)
            acc = d if acc is None else acc + d
        y_ref[...] = acc


def _experts(xs, tile_expert, n_active, wg, wu, wd, tme, nchunk):
    n_rows = xs.shape[0]
    n_tiles = n_rows // tme

    def tile(i, na):
        return jnp.minimum(i, na[0] - 1)

    resident = pl.Buffered(1)
    grid_spec = pltpu.PrefetchScalarGridSpec(
        num_scalar_prefetch=2,
        grid=(n_tiles,),
        in_specs=[
            pl.BlockSpec((tme, D_MODEL), lambda i, te, na: (tile(i, na), 0)),
            pl.BlockSpec((1, D_MODEL, D_FF_EXPERT), lambda i, te, na: (te[tile(i, na)], 0, 0),
                         pipeline_mode=resident),
            pl.BlockSpec((1, D_MODEL, D_FF_EXPERT), lambda i, te, na: (te[tile(i, na)], 0, 0),
                         pipeline_mode=resident),
            pl.BlockSpec((1, D_FF_EXPERT, D_MODEL), lambda i, te, na: (te[tile(i, na)], 0, 0),
                         pipeline_mode=resident),
        ],
        out_specs=pl.BlockSpec((tme, D_MODEL), lambda i, te, na: (i, 0)),
    )
    return pl.pallas_call(
        functools.partial(_expert_kernel, nchunk=nchunk),
        grid_spec=grid_spec,
        out_shape=jax.ShapeDtypeStruct((n_rows, D_MODEL), F32),
        compiler_params=_cparams("arbitrary"),
        name="moe_experts",
    )(tile_expert, n_active, xs, wg, wu, wd)


def _combine_kernel(pos_ref, info_ref, h_ref, y_hbm, o_ref, buf_ref, sem, *, tc):
    def copy(t, s):
        return pltpu.make_async_copy(
            y_hbm.at[pl.ds(pos_ref[0, 0, 2 * t + s], 1)], buf_ref.at[s, pl.ds(t, 1)], sem)

    def start(t, carry):
        copy(t, 0).start()
        copy(t, 1).start()
        return carry

    def wait(t, carry):
        copy(t, 0).wait()
        copy(t, 1).wait()
        return carry

    lax.fori_loop(0, tc, start, 0, unroll=8)
    lax.fori_loop(0, tc, wait, 0, unroll=8)
    info = info_ref[...]
    lane = lax.broadcasted_iota(I32, (tc, LANES), 1)
    w1 = jnp.sum(jnp.where(lane == 4, info, 0.0), axis=-1, keepdims=True)
    w2 = jnp.sum(jnp.where(lane == 5, info, 0.0), axis=-1, keepdims=True)
    o_ref[...] = h_ref[...] + (w1 * buf_ref[0] + w2 * buf_ref[1])


def _combine(h2d, info, pos, y, tc):
    T = h2d.shape[0]
    pos3 = pos.reshape(T // tc, 1, 2 * tc)
    kern = functools.partial(_combine_kernel, tc=tc)
    return pl.pallas_call(
        kern,
        grid=(T // tc,),
        in_specs=[
            pl.BlockSpec((1, 1, 2 * tc), lambda i: (i, 0, 0), memory_space=pltpu.SMEM),
            pl.BlockSpec((tc, LANES), lambda i: (i, 0)),
            pl.BlockSpec((tc, D_MODEL), lambda i: (i, 0)),
            pl.BlockSpec(memory_space=pl.ANY),
        ],
        out_specs=pl.BlockSpec((tc, D_MODEL), lambda i: (i, 0)),
        out_shape=jax.ShapeDtypeStruct((T, D_MODEL), F32),
        scratch_shapes=[pltpu.VMEM((2, tc, D_MODEL), F32), pltpu.SemaphoreType.DMA(())],
        compiler_params=_cparams("arbitrary"),
        name="moe_combine",
    )(pos3, info, h2d, y)


def _bf16_pieces(x, n):
    pieces, r = [], x.astype(F32)
    for _ in range(n):
        p = r.astype(BF16)
        pieces.append(p)
        r = r - p.astype(F32)
    return pieces


def _alibi_slopes():
    return [2.0 ** (-8.0 * (h + 1.0) / N_ATTN_HEADS) for h in range(N_ATTN_HEADS)]


def _attn_col_gain(dil_q_gain, dil_k_gain, diff_q_gain, diff_k_gain):
    rep = lambda g, n: jnp.tile(g.astype(F32), n)
    ones = jnp.ones((COLB,), F32)
    dq = rep(dil_q_gain, COLB // HEAD_DIM) * (HEAD_DIM ** -0.5 * LOG2E)
    dk = rep(dil_k_gain, COLB // HEAD_DIM)
    fq = rep(diff_q_gain, COLB // DIFF_QK_DIM) * (DIFF_QK_DIM ** -0.5 * LOG2E)
    fk = rep(diff_k_gain, COLB // DIFF_QK_DIM)
    return jnp.stack([dq, dq, dq, dk, dk, dk, ones, ones, ones, fq, fk, ones])


def _lower_bound_rows(lb_logits, layer):
    sm = jax.nn.softmax(lb_logits.astype(F32), axis=0)
    lb = (jnp.cumsum(sm, axis=0) - sm[0])[layer]
    return jnp.zeros((8, D_MODEL), F32).at[0].set(jnp.log(lb)).at[1].set(jnp.log1p(-lb)).at[2].set(1.0 - lb)


def _layer0_and_hgrn_inproj(h2d, B, S, attn_norm, w_in, dq_g, dk_g, fq_g, fk_g, lq1, lk1, lq2, lk2, out_g,
                            w_out, ffn_norm, w_gate, w_up, w_down, hgrn_norm, hgrn_w_in, lb_logits):
    T = B * S
    tm = min(512, T)
    slopes = _alibi_slopes()
    *cls, dproj = _attn_inproj(h2d, attn_norm.reshape(1, D_MODEL).astype(F32), w_in.astype(BF16),
                               _attn_col_gain(dq_g, dk_g, fq_g, fk_g), tm)
    outs, lses = [], []
    for g, (window, dilation) in enumerate(DIL_GROUPS):
        assert window // dilation == QBLOCK and S % (dilation * QBLOCK) == 0 and tm % dilation == 0
        o, lse = _dilated_group(cls[g], B, S, dilation, tuple(slopes[4 * g:4 * g + 4]))
        outs.extend(o)
        lses.extend(lse)

    lam_init = 0.8 - 0.6 * math.exp(-0.3 * 0)
    lam = (jnp.exp(jnp.sum(lq1.astype(F32) * lk1.astype(F32)))
           - jnp.exp(jnp.sum(lq2.astype(F32) * lk2.astype(F32))) + lam_init).reshape(1)
    sl = jnp.asarray(slopes[N_DIL_HEADS:], F32) * LOG2E
    s_hi, s_lo = _bf16_pieces(sl, 2)
    slope_feat = jnp.zeros((8, LANES), F32)
    slope_feat = slope_feat.at[:N_DIFF_HEADS, 0].set(s_hi.astype(F32)).at[:N_DIFF_HEADS, 1].set(s_hi.astype(F32))
    slope_feat = slope_feat.at[:N_DIFF_HEADS, 2].set(s_lo.astype(F32)).at[:N_DIFF_HEADS, 3].set(s_lo.astype(F32))
    p_hi, p_lo = _bf16_pieces(jnp.arange(S, dtype=F32), 2)
    pos_feat = jnp.zeros((S, LANES), BF16)
    pos_feat = pos_feat.at[:, 0].set(p_hi).at[:, 1].set(p_lo).at[:, 2].set(p_hi).at[:, 3].set(p_lo)
    out_gain = (jnp.tile(out_g.astype(F32), COLB // HEAD_DIM) * (1.0 - lam_init)).reshape(1, COLB)
    diff = _diff_attention(dproj.reshape(B, S, _QKV), lam, slope_feat, pos_feat, out_gain, min(512, S)).reshape(T, COLB)

    h2, *gates = _layer0_tail(
        h2d, outs, lses, diff, w_out.astype(BF16), ffn_norm.reshape(1, D_MODEL).astype(F32),
        w_gate.astype(BF16), w_up.astype(BF16), w_down.astype(BF16),
        hgrn_norm.reshape(1, D_MODEL).astype(F32), hgrn_w_in.astype(BF16),
        _lower_bound_rows(lb_logits, 1), min(HGRN_ROWS, S))
    return h2, gates


def _layer1_rest(h2d, gates, B, S, out_gain, w_out, moe_norm, w_router, w_gate, w_up, w_down):
    o = _hgrn_recurrence(gates, out_gain.reshape(1, LANES).astype(F32), B, S, min(HGRN_ROWS, S))
    return _moe_block(h2d, o, w_out, moe_norm, w_router, w_gate, w_up, w_down)


def _moe_block(h2d, mixer_out, w_out, moe_norm, w_router, w_gate, w_up, w_down):
    T = h2d.shape[0]
    tme = 512
    w_pad = jnp.zeros((D_MODEL, LANES), F32).at[:, :N_EXPERTS].set(w_router.astype(F32))
    w_hi, w_lo = _bf16_pieces(w_pad, 2)
    h2d, xn, info, counts = _proj_router(h2d, mixer_out, w_out.astype(BF16),
                                         moe_norm.reshape(1, D_MODEL).astype(F32), w_hi, w_lo, min(256, T))
    cnt = counts[0, :N_EXPERTS].astype(I32)
    padded = ((cnt + tme - 1) // tme) * tme
    ends = jnp.cumsum(padded)
    starts = ends - padded
    experts = info[:, 0:2].astype(I32)
    ranks = info[:, 2:4].astype(I32)
    pos = (starts[experts] + ranks).reshape(-1)
    n_rows = 2 * T + N_EXPERTS * tme
    n_tiles = n_rows // tme
    tile_start = jnp.arange(n_tiles, dtype=I32) * tme
    tile_expert = jnp.minimum(jnp.sum((ends[None, :] <= tile_start[:, None]).astype(I32), axis=1), N_EXPERTS - 1)
    n_active = (ends[-1] // tme).astype(I32).reshape(1)
    tail = jnp.minimum(ends[-1] + jnp.arange(N_EXPERTS, dtype=I32) * tme, n_rows - tme)
    zero_tiles = jnp.concatenate([jnp.maximum(ends - tme, 0), tail]).astype(I32)
    xs = _dispatch(xn, pos, zero_tiles, n_rows, min(256, T), tme)
    y = _experts(xs, tile_expert, n_active, w_gate.astype(BF16), w_up.astype(BF16), w_down.astype(BF16),
                 tme, 2)
    return _combine(h2d, info, pos, y, min(256, T))


def kernel(x, attn_norm, attn_w_in, dil_q_gain, dil_k_gain, diff_q_gain, diff_k_gain, diff_lambda_q1, diff_lambda_k1, diff_lambda_q2, diff_lambda_k2, diff_out_gain, attn_w_out, ffn_norm, ffn_w_gate, ffn_w_up, ffn_w_down, hgrn_norm, hgrn_w_in, hgrn_lb_logits, hgrn_out_gain, hgrn_w_out, moe_norm, moe_w_router, moe_w_gate, moe_w_up, moe_w_down):
    B, S, D = x.shape
    assert D == D_MODEL
    h = x.astype(F32).reshape(B * S, D)
    h, gates = _layer0_and_hgrn_inproj(
        h, B, S, attn_norm[0], attn_w_in[0], dil_q_gain[0], dil_k_gain[0], diff_q_gain[0], diff_k_gain[0],
        diff_lambda_q1[0], diff_lambda_k1[0], diff_lambda_q2[0], diff_lambda_k2[0], diff_out_gain[0],
        attn_w_out[0], ffn_norm[0], ffn_w_gate[0], ffn_w_up[0], ffn_w_down[0],
        hgrn_norm[0], hgrn_w_in[0], hgrn_lb_logits)
    h = _layer1_rest(h, gates, B, S, hgrn_out_gain[0], hgrn_w_out[0], moe_norm[0], moe_w_router[0],
                     moe_w_gate[0], moe_w_up[0], moe_w_down[0])
    return h.reshape(B, S, D).astype(x.dtype)
```

```python
import functools
import math

import jax
import jax.numpy as jnp
from jax import lax
from jax.experimental import pallas as pl
from jax.experimental.pallas import tpu as pltpu

F32 = jnp.float32
BF16 = jnp.bfloat16
I32 = jnp.int32

D_MODEL = 1024
HEAD_DIM = 64
N_ATTN_HEADS = 16
N_DIL_HEADS = 12
N_DIFF_HEADS = 4
DIL_GROUPS = ((128, 1), (512, 4), (2048, 16))
DIFF_QK_DIM = 32
ATTN_IN_WIDTH = 3072
QBLOCK = 128
HGRN_HEADS = 8
HGRN_CHUNK = 64
HGRN_ROWS = 256
DEC_ROWS = 8
D_FF = 2816
N_EXPERTS = 8
D_FF_EXPERT = 3584
EPS = 1e-6
NEG_INF = -1e30
LOG2E = 1.4426950408889634
LANES = 128
COLB = 256
VMEM_LIMIT = 56 * 1024 * 1024


def _cparams(*sem):
    return pltpu.CompilerParams(dimension_semantics=sem, vmem_limit_bytes=VMEM_LIMIT)


def _split_dot(x, m, terms):
    acc = None
    r = x
    for t in range(terms):
        part = r.astype(BF16)
        d = jnp.dot(part, m, preferred_element_type=F32)
        acc = d if acc is None else acc + d
        if t + 1 < terms:
            r = r - part.astype(F32)
    return acc


def _seg_matrix(n, seg):
    i = jnp.arange(n)
    return (i[:, None] // seg == i[None, :] // seg).astype(BF16)


def _rms_rows(x, gain_row):
    return x * lax.rsqrt(jnp.mean(x * x, axis=-1, keepdims=True) + EPS) * gain_row


def _sigmoid(x):
    return 1.0 / (1.0 + jnp.exp(-x))


def _silu(x):
    return x * _sigmoid(x)


_ATTN_NORM_SEG = (64, 64, 64, 64, 64, 64, 0, 0, 0, 32, 32, 0)
_QKV = 3 * COLB


def _attn_inproj_kernel(h_ref, ng_ref, w_ref, cg_ref, s64_ref, s32_ref,
                        c0_ref, c1_ref, c2_ref, df_ref, y_ref, *, tm):
    cls_refs = (c0_ref, c1_ref, c2_ref)
    xb = _rms_rows(h_ref[...], ng_ref[...]).astype(BF16)
    blocks = range(len(_ATTN_NORM_SEG))
    ys = [jnp.dot(xb, w_ref[:, c * COLB:(c + 1) * COLB], preferred_element_type=F32) for c in blocks]
    ms = {c: _split_dot(ys[c] * ys[c], s64_ref[...] if seg == 64 else s32_ref[...], 2) * (1.0 / seg)
          for c, seg in enumerate(_ATTN_NORM_SEG) if seg}
    ys = [ys[c] * lax.rsqrt(ms[c] + EPS) * cg_ref[c:c + 1, :] if c in ms else ys[c] for c in blocks]
    strided = []
    for c in blocks:
        part, g = divmod(c, 3)
        if part == 3:
            df_ref[:, g * COLB:(g + 1) * COLB] = ys[c].astype(BF16)
        elif DIL_GROUPS[g][1] == 1:
            cls_refs[g][:, part * COLB:(part + 1) * COLB] = ys[c].astype(BF16)
        else:
            slot = len(strided)
            y_ref[slot, 0] = ys[c][:, :LANES]
            y_ref[slot, 1] = ys[c][:, LANES:]
            strided.append((slot, part, g))
    for slot, part, g in strided:
        d = DIL_GROUPS[g][1]
        for r in range(d):
            col = r * _QKV + part * COLB
            rows = pl.ds(r, tm // d, stride=d)
            cls_refs[g][:, col:col + COLB] = jnp.concatenate(
                [y_ref[slot, 0, rows, :], y_ref[slot, 1, rows, :]], axis=1).astype(BF16)


def _attn_inproj(h2d, norm_gain, w_bf, col_gain, tm):
    T = h2d.shape[0]
    const = lambda i: (0, 0)
    row = lambda i: (i, 0)
    dils = [d for _, d in DIL_GROUPS]
    kern = functools.partial(_attn_inproj_kernel, tm=tm)
    return pl.pallas_call(
        kern,
        grid=(T // tm,),
        in_specs=[
            pl.BlockSpec((tm, D_MODEL), row),
            pl.BlockSpec((1, D_MODEL), const),
            pl.BlockSpec((D_MODEL, ATTN_IN_WIDTH), const),
            pl.BlockSpec((ATTN_IN_WIDTH // COLB, COLB), const),
            pl.BlockSpec((COLB, COLB), const),
            pl.BlockSpec((COLB, COLB), const),
        ],
        out_specs=[pl.BlockSpec((tm // d, d * _QKV), row) for d in dils] + [pl.BlockSpec((tm, _QKV), row)],
        out_shape=[jax.ShapeDtypeStruct((T // d, d * _QKV), BF16) for d in dils]
                  + [jax.ShapeDtypeStruct((T, _QKV), BF16)],
        scratch_shapes=[pltpu.VMEM((3 * sum(d > 1 for d in dils), 2, tm, LANES), F32)],
        compiler_params=_cparams("parallel"),
        name="attn_inproj",
    )(h2d, norm_gain, w_bf, col_gain, _seg_matrix(COLB, 64), _seg_matrix(COLB, 32))


def _dil_kernel(q_ref, kp_ref, kc_ref, vp_ref, vc_ref, o0_ref, o1_ref, l0_ref, l1_ref,
                *, slopes, dilation, ub):
    u = pl.program_id(1)
    r = pl.program_id(2)
    q = q_ref[...]
    kcat = jnp.concatenate([kp_ref[...], kc_ref[...]], axis=0)
    vcat = jnp.concatenate([vp_ref[...], vc_ref[...]], axis=0)
    qi = lax.broadcasted_iota(I32, (QBLOCK, 2 * QBLOCK), 0)
    kj = lax.broadcasted_iota(I32, (QBLOCK, 2 * QBLOCK), 1)
    step = qi + QBLOCK - kj
    in_window = (step >= 0) & (step <= QBLOCK)
    first_window = in_window & ((kj >= QBLOCK) | (u > 0))
    stepf = step.astype(F32)
    bias = [(-slopes[h] * dilation * LOG2E) * stepf for h in range(COLB // HEAD_DIM)]
    mask_bias = [jnp.where(in_window, b, NEG_INF) for b in bias]
    mask_bias_first = [jnp.where(first_window, b, NEG_INF) for b in bias]
    lane = lax.broadcasted_iota(I32, (QBLOCK, LANES), 1)
    low_half = lane < HEAD_DIM
    jobs = [(i, pair, hh) for i in range(ub // QBLOCK) for pair in range(2) for hh in range(2)]
    nt = (((1,), (1,)), ((), ()))

    def masked_q(i, pair, hh):
        qp = q[i * QBLOCK:(i + 1) * QBLOCK, pair * LANES:(pair + 1) * LANES]
        return jnp.where(low_half if hh == 0 else ~low_half, qp, jnp.zeros_like(qp))

    scores = [lax.dot_general(masked_q(i, pair, hh), kcat[i * QBLOCK:(i + 2) * QBLOCK, pair * LANES:(pair + 1) * LANES],
                              nt, preferred_element_type=F32) for i, pair, hh in jobs]
    stats = []
    for (i, pair, hh), s in zip(jobs, scores):
        s = s + (mask_bias_first if i == 0 else mask_bias)[pair * 2 + hh]
        m = jnp.max(s, axis=-1, keepdims=True)
        e = jnp.exp2(s - m)
        stats.append((m, jnp.sum(e, axis=-1, keepdims=True), e.astype(BF16)))
    pvs = [jnp.dot(e, vcat[i * QBLOCK:(i + 2) * QBLOCK, pair * LANES:(pair + 1) * LANES], preferred_element_type=F32)
           for (i, pair, hh), (_, _, e) in zip(jobs, stats)]
    for n in range(0, len(jobs), 2):
        i, pair, _ = jobs[n]
        outs = [pvs[n + hh] / stats[n + hh][1] for hh in range(2)]
        lses = [jnp.broadcast_to(stats[n + hh][0] + jnp.log2(stats[n + hh][1]), (QBLOCK, LANES)) for hh in range(2)]
        if dilation == 1:
            rows = slice(i * QBLOCK, (i + 1) * QBLOCK)
        else:
            rows = pl.ds(r + i * QBLOCK * dilation, QBLOCK, stride=dilation)
        (o0_ref, o1_ref)[pair][rows, :] = jnp.where(low_half, outs[0], outs[1])
        (l0_ref, l1_ref)[pair][rows, :] = jnp.where(low_half, lses[0], lses[1])


def _dilated_group(cls, B, S, dilation, slopes):
    U = S // dilation
    ub = min(512, U)
    sub = ub // QBLOCK
    view = cls.reshape(B, U, dilation * _QKV)
    prev = lambda u: jnp.maximum(u * sub - 1, 0)
    kern = functools.partial(_dil_kernel, slopes=slopes, dilation=dilation, ub=ub)
    out_spec = pl.BlockSpec((None, ub * dilation, LANES), lambda b, u, r: (b, u, 0))
    res = pl.pallas_call(
        kern,
        grid=(B, U // ub, dilation),
        in_specs=[
            pl.BlockSpec((None, ub, COLB), lambda b, u, r: (b, u, 3 * r)),
            pl.BlockSpec((None, QBLOCK, COLB), lambda b, u, r: (b, prev(u), 3 * r + 1)),
            pl.BlockSpec((None, ub, COLB), lambda b, u, r: (b, u, 3 * r + 1)),
            pl.BlockSpec((None, QBLOCK, COLB), lambda b, u, r: (b, prev(u), 3 * r + 2)),
            pl.BlockSpec((None, ub, COLB), lambda b, u, r: (b, u, 3 * r + 2)),
        ],
        out_specs=[out_spec] * 4,
        out_shape=[jax.ShapeDtypeStruct((B, S, LANES), F32)] * 4,
        compiler_params=_cparams("parallel", "parallel", "arbitrary"),
        name=f"dilated_attn_d{dilation}",
    )(view, view, view, view, view)
    res = [a.reshape(B * S, LANES) for a in res]
    return res[:2], res[2:]


def _diff_kernel(lam_ref, q_ref, k_ref, v_ref, pos_ref, sl_ref, og_ref, s64_ref, o_ref,
                 m_ref, l_ref, acc_ref, qa_ref, *, tq):
    i = pl.program_id(1)
    q = q_ref[...]
    lane = lax.broadcasted_iota(I32, (tq, LANES), 1)
    low_half = lane < HEAD_DIM
    for pair in range(2):
        qp = q[:, pair * LANES:(pair + 1) * LANES]
        for hh in range(2):
            feat = jnp.broadcast_to(sl_ref[pair * 2 + hh:pair * 2 + hh + 1, :], (tq, LANES)).astype(BF16)
            for mu in range(2):
                lo = hh * HEAD_DIM + mu * DIFF_QK_DIM
                sel = (lane >= lo) & (lane < lo + DIFF_QK_DIM)
                r = hh * 2 + mu
                qa_ref[pair, r * tq:(r + 1) * tq, :] = jnp.concatenate(
                    [jnp.where(sel, qp, jnp.zeros_like(qp)), feat], axis=1)

    m_ref[...] = jnp.full(m_ref.shape, NEG_INF, F32)
    l_ref[...] = jnp.zeros(l_ref.shape, F32)
    acc_ref[...] = jnp.zeros(acc_ref.shape, F32)

    def scores(j, nk):
        ks = pl.multiple_of(j * tq, tq)
        kblk = k_ref[pl.ds(ks, nk * tq), :]
        pblk = pos_ref[pl.ds(ks, nk * tq), :]
        s_alls = []
        for pair in range(2):
            kaug = jnp.concatenate([kblk[:, pair * LANES:(pair + 1) * LANES], pblk], axis=1)
            s_alls.append(lax.dot_general(qa_ref[pair], kaug, (((1,), (1,)), ((), ())),
                                          preferred_element_type=F32))
        return s_alls, v_ref[pl.ds(ks, nk * tq), :]

    def finish(s_alls, vblk, masked):
        tk = vblk.shape[0]
        ones = jnp.ones((tk, LANES), BF16)
        if masked:
            row = lax.broadcasted_iota(I32, (tq, tk), 0)
            col = lax.broadcasted_iota(I32, (tq, tk), 1)
            causal = col <= row
        probs, alphas = [], []
        for pair in range(2):
            ps = []
            for r in range(4):
                idx = pair * 4 + r
                s = s_alls[pair][r * tq:(r + 1) * tq]
                if masked:
                    s = jnp.where(causal, s, NEG_INF)
                m_prev = m_ref[idx]
                m_next = jnp.maximum(m_prev, jnp.max(s, axis=-1, keepdims=True))
                m_ref[idx] = m_next
                alphas.append(jnp.exp2(m_prev - m_next))
                ps.append(jnp.exp2(s - jnp.concatenate([m_next] * (tk // LANES), axis=1)).astype(BF16))
            probs.append(jnp.concatenate(ps, axis=0))
        for pair in range(2):
            vaug = jnp.concatenate([vblk[:, pair * LANES:(pair + 1) * LANES], ones], axis=1)
            pv = jnp.dot(probs[pair], vaug, preferred_element_type=F32)
            for r in range(4):
                idx = pair * 4 + r
                part = pv[r * tq:(r + 1) * tq]
                acc_ref[idx] = alphas[idx] * acc_ref[idx] + part[:, :LANES]
                l_ref[idx] = alphas[idx] * l_ref[idx] + part[:, LANES:]

    def two_full_blocks(jj, carry):
        finish(*scores(2 * jj, 2), False)
        return carry

    lax.fori_loop(0, i // 2, two_full_blocks, 0)

    @pl.when(i % 2 == 1)
    def _():
        finish(*scores(i - 1, 1), False)

    finish(*scores(i, 1), True)

    lam = lam_ref[0]
    for pair in range(2):
        o = None
        for mu in range(2):
            lo_idx, hi_idx = pair * 4 + mu, pair * 4 + 2 + mu
            term = jnp.where(low_half, acc_ref[lo_idx] / l_ref[lo_idx], acc_ref[hi_idx] / l_ref[hi_idx])
            o = term if mu == 0 else o - lam * term
        ms = _split_dot(o * o, s64_ref[...], 2) * (1.0 / HEAD_DIM)
        o = o * lax.rsqrt(ms + EPS) * og_ref[:, pair * LANES:(pair + 1) * LANES]
        o_ref[:, pair * LANES:(pair + 1) * LANES] = o.astype(BF16)


def _diff_attention(proj, lam, slope_feat, pos_feat, out_gain, tq):
    B, S, W = proj.shape
    kern = functools.partial(_diff_kernel, tq=tq)
    return pl.pallas_call(
        kern,
        grid=(B, S // tq),
        in_specs=[
            pl.BlockSpec(memory_space=pltpu.SMEM),
            pl.BlockSpec((None, tq, COLB), lambda b, i: (b, i, 0)),
            pl.BlockSpec((None, S, COLB), lambda b, i: (b, 0, 1), pipeline_mode=pl.Buffered(1)),
            pl.BlockSpec((None, S, COLB), lambda b, i: (b, 0, 2), pipeline_mode=pl.Buffered(1)),
            pl.BlockSpec((S, LANES), lambda b, i: (0, 0), pipeline_mode=pl.Buffered(1)),
            pl.BlockSpec((8, LANES), lambda b, i: (0, 0)),
            pl.BlockSpec((1, COLB), lambda b, i: (0, 0)),
            pl.BlockSpec((LANES, LANES), lambda b, i: (0, 0)),
        ],
        out_specs=pl.BlockSpec((None, tq, COLB), lambda b, i: (b, i, 0)),
        out_shape=jax.ShapeDtypeStruct((B, S, COLB), BF16),
        scratch_shapes=[
            pltpu.VMEM((8, tq, LANES), F32),
            pltpu.VMEM((8, tq, LANES), F32),
            pltpu.VMEM((8, tq, LANES), F32),
            pltpu.VMEM((2, 4 * tq, 2 * LANES), BF16),
        ],
        compiler_params=_cparams("parallel", "parallel"),
        name="diff_attn",
    )(lam, proj, proj, proj, pos_feat, slope_feat, out_gain, _seg_matrix(LANES, HEAD_DIM))


def _merge_heads(o_refs, l_refs, d_ref):
    n = len(o_refs)
    pieces = [None] * n
    for pair in range(2):
        ls = [l_refs[2 * g + pair][...] for g in range(n // 2)]
        mx = functools.reduce(jnp.maximum, ls)
        es = [jnp.exp2(l - mx) for l in ls]
        inv = 1.0 / functools.reduce(jnp.add, es)
        for g, e in enumerate(es):
            pieces[2 * g + pair] = (e * inv * o_refs[2 * g + pair][...]).astype(BF16)
    return jnp.concatenate(pieces + [d_ref[...]], axis=1)


def _split_dot_lhs(m, x):
    hi = x.astype(BF16)
    lo = (x - hi.astype(F32)).astype(BF16)
    return jnp.dot(m, hi, preferred_element_type=F32) + jnp.dot(m, lo, preferred_element_type=F32)


def _chunk_mask(rows):
    ti = lax.broadcasted_iota(I32, (rows, rows), 0)
    si = lax.broadcasted_iota(I32, (rows, rows), 1)
    return (si <= ti) & (si >= (ti // HGRN_CHUNK) * HGRN_CHUNK)


def _hgrn_gates(xb, w_ref, lb_ref, qs_ref, ks_ref, qi_ref, kst_ref, v_ref, g_ref, dec_ref):
    C = HGRN_CHUNK
    tm = xb.shape[0]
    nc = tm // C
    yq, fl, yv, yg = [jnp.dot(xb, w_ref[:, c * D_MODEL:(c + 1) * D_MODEL], preferred_element_type=F32)
                      for c in range(4)]
    v_ref[...] = yv.astype(BF16)
    g_ref[...] = _sigmoid(yg).astype(BF16)
    q = _silu(yq)
    log_lb, log1m_lb, one_m_lb = lb_ref[0:1, :], lb_ref[1:2, :], lb_ref[2:3, :]
    t = jnp.exp(-jnp.abs(fl))
    r = 1.0 / (1.0 + t)
    c = log1m_lb + (jnp.minimum(fl, 0.0) + jnp.log(r))
    lf = jnp.maximum(log_lb, c) + jnp.log(1.0 + jnp.exp(-jnp.abs(log_lb - c)))
    k = one_m_lb * jnp.where(fl >= 0.0, t * r, r)
    tri = jnp.where(_chunk_mask(tm), 1.0, 0.0).astype(BF16)
    b = _split_dot_lhs(tri, lf)
    chunk_row = lambda i: jnp.concatenate(
        [jnp.broadcast_to(b[n * C + i:n * C + i + 1, :], (C, D_MODEL)) for n in range(nc)], axis=0)
    b_mid = chunk_row(C // 2)
    b_last = chunk_row(C - 1)
    qs_ref[...] = (q * jnp.exp(b - b_mid)).astype(BF16)
    ks_ref[...] = (k * jnp.exp(b_mid - b)).astype(BF16)
    qi_ref[...] = (q * jnp.exp(b)).astype(BF16)
    kst_ref[...] = (k * jnp.exp(b_last - b)).astype(BF16)
    last_rows = jnp.concatenate([b[n * C + C - 1:n * C + C, :] for n in range(nc)], axis=0)
    dec_ref[...] = jnp.concatenate(
        [jnp.exp(last_rows), jnp.zeros((dec_ref.shape[0] - nc, D_MODEL), F32)], axis=0)


def _layer0_tail_kernel(h_ref, *refs):
    n = 2 * len(DIL_GROUPS)
    o_refs, l_refs = refs[:n], refs[n:2 * n]
    (d_ref, wo_ref, fng_ref, wg_ref, wu_ref, wd_ref, hng_ref, wh_ref, lb_ref,
     h2_ref, qs_ref, ks_ref, qi_ref, kst_ref, v_ref, g_ref, dec_ref) = refs[2 * n:]
    mixed = _merge_heads(o_refs, l_refs, d_ref)
    h1 = h_ref[...] + jnp.dot(mixed, wo_ref[...], preferred_element_type=F32)
    xb = _rms_rows(h1, fng_ref[...]).astype(BF16)
    g = jnp.dot(xb, wg_ref[...], preferred_element_type=F32)
    u = jnp.dot(xb, wu_ref[...], preferred_element_type=F32)
    a = (_silu(g) * u).astype(BF16)
    h2 = h1 + jnp.dot(a, wd_ref[...], preferred_element_type=F32)
    h2_ref[...] = h2
    _hgrn_gates(_rms_rows(h2, hng_ref[...]).astype(BF16), wh_ref, lb_ref,
                qs_ref, ks_ref, qi_ref, kst_ref, v_ref, g_ref, dec_ref)


def _layer0_tail(h2d, outs, lses, diff, w_out, ffn_gain, wg, wu, wd, hgrn_gain, w_in, lb_rows, tm):
    T = h2d.shape[0]
    row = lambda i: (i, 0)
    full = pl.BlockSpec((tm, D_MODEL), row)
    half = pl.BlockSpec((tm, LANES), row)
    held = lambda shape: pl.BlockSpec(shape, lambda i: (0, 0), pipeline_mode=pl.Buffered(1))
    bf = jax.ShapeDtypeStruct((T, D_MODEL), BF16)
    f32 = jax.ShapeDtypeStruct((T, D_MODEL), F32)
    return pl.pallas_call(
        _layer0_tail_kernel,
        grid=(T // tm,),
        in_specs=[full] + [half] * (len(outs) + len(lses))
                 + [pl.BlockSpec((tm, COLB), row), held((D_MODEL, D_MODEL)), held((1, D_MODEL)),
                    held((D_MODEL, D_FF)), held((D_MODEL, D_FF)), held((D_FF, D_MODEL)),
                    held((1, D_MODEL)), held((D_MODEL, 4 * D_MODEL)), held((8, D_MODEL))],
        out_specs=[full] * 7 + [pl.BlockSpec((DEC_ROWS, D_MODEL), row)],
        out_shape=[f32] + [bf] * 6 + [jax.ShapeDtypeStruct((T // tm * DEC_ROWS, D_MODEL), F32)],
        compiler_params=_cparams("parallel"),
        name="layer0_tail_hgrn_inproj",
    )(h2d, *outs, *lses, diff, w_out, ffn_gain, wg, wu, wd, hgrn_gain, w_in, lb_rows)


def _hgrn_kernel(qs_ref, ks_ref, qi_ref, kst_ref, v_ref, g_ref, dec_ref, og_ref, o_ref, st_ref, *, rows):
    @pl.when(pl.program_id(1) == 0)
    def _():
        st_ref[...] = jnp.zeros(st_ref.shape, F32)

    C = HGRN_CHUNK
    nc = rows // C
    heads = range(HGRN_HEADS)
    lanes = [slice(hd * LANES, (hd + 1) * LANES) for hd in heads]
    causal = _chunk_mask(rows)
    nt = (((1,), (1,)), ((), ()))
    tn = (((0,), (0,)), ((), ()))
    scores = [lax.dot_general(qs_ref[:, lanes[hd]], ks_ref[:, lanes[hd]], nt, preferred_element_type=F32)
              for hd in heads]
    incs = [[lax.dot_general(v_ref[c * C:(c + 1) * C, lanes[hd]], kst_ref[c * C:(c + 1) * C, lanes[hd]], tn,
                             preferred_element_type=F32) for c in range(nc)] for hd in heads]
    probs = [jnp.where(causal, s, 0.0).astype(BF16) for s in scores]
    outs = [jnp.dot(probs[hd], v_ref[:, lanes[hd]], preferred_element_type=F32) for hd in heads]
    states = []
    for hd in heads:
        st, per_chunk = st_ref[hd], []
        for c in range(nc):
            per_chunk.append(st.astype(BF16))
            st = dec_ref[c:c + 1, lanes[hd]] * st + incs[hd][c]
        st_ref[hd] = st
        states.append(per_chunk)
    for hd in heads:
        inter = [lax.dot_general(qi_ref[c * C:(c + 1) * C, lanes[hd]], states[hd][c], nt,
                                 preferred_element_type=F32) for c in range(nc)]
        o = outs[hd] + jnp.concatenate(inter, axis=0)
        o = o * lax.rsqrt(jnp.mean(o * o, axis=-1, keepdims=True) + EPS) * og_ref[...]
        o_ref[:, lanes[hd]] = (o * g_ref[:, lanes[hd]].astype(F32)).astype(BF16)


def _hgrn_recurrence(gates, out_gain, B, S, rows):
    T = B * S
    nseq = S // rows
    row = pl.BlockSpec((rows, D_MODEL), lambda b, s: (b * nseq + s, 0))
    dec = pl.BlockSpec((DEC_ROWS, D_MODEL), lambda b, s: (b * nseq + s, 0))
    kern = functools.partial(_hgrn_kernel, rows=rows)
    return pl.pallas_call(
        kern,
        grid=(B, nseq),
        in_specs=[row] * 6 + [dec, pl.BlockSpec((1, LANES), lambda b, s: (0, 0))],
        out_specs=row,
        out_shape=jax.ShapeDtypeStruct((T, D_MODEL), BF16),
        scratch_shapes=[pltpu.VMEM((HGRN_HEADS, LANES, LANES), F32)],
        compiler_params=_cparams("parallel", "arbitrary"),
        name="hgrn_recurrence",
    )(*gates, out_gain)


def _proj_router_kernel(h_ref, x_ref, wo_ref, ng_ref, whi_ref, wlo_ref,
                        h2_ref, xn_ref, info_ref, cnt_ref, carry_ref, *, tm):
    @pl.when(pl.program_id(0) == 0)
    def _():
        carry_ref[...] = jnp.zeros(carry_ref.shape, F32)

    half = tm // 2
    halves = [slice(n * half, (n + 1) * half) for n in range(2)]
    h2 = [h_ref[rs, :] + jnp.dot(x_ref[rs, :], wo_ref[...], preferred_element_type=F32) for rs in halves]
    xn = [_rms_rows(v, ng_ref[...]) for v in h2]
    for rs, a, c in zip(halves, h2, xn):
        h2_ref[rs, :] = a
        xn_ref[rs, :] = c
    xhi = [v.astype(BF16) for v in xn]
    xlo = [(v - hi.astype(F32)).astype(BF16) for v, hi in zip(xn, xhi)]
    lane = lax.broadcasted_iota(I32, (half, LANES), 1)
    lanef = lane.astype(F32)
    logits = [jnp.where(lane < N_EXPERTS,
                        jnp.dot(hi, whi_ref[...], preferred_element_type=F32)
                        + jnp.dot(hi, wlo_ref[...], preferred_element_type=F32)
                        + jnp.dot(lo, whi_ref[...], preferred_element_type=F32), -jnp.inf)
              for hi, lo in zip(xhi, xlo)]
    m1 = [jnp.max(l, axis=-1, keepdims=True) for l in logits]
    i1 = [jnp.min(jnp.where(l == m, lanef, float(LANES)), axis=-1, keepdims=True) for l, m in zip(logits, m1)]
    oh1 = [lanef == i for i in i1]
    rest = [jnp.where(o, -jnp.inf, l) for o, l in zip(oh1, logits)]
    m2 = [jnp.max(l, axis=-1, keepdims=True) for l in rest]
    i2 = [jnp.min(jnp.where(l == m, lanef, float(LANES)), axis=-1, keepdims=True) for l, m in zip(rest, m2)]
    oh2 = [lanef == i for i in i2]
    e = [jnp.exp(b - a) for a, b in zip(m1, m2)]
    w1 = [1.0 / (1.0 + v) for v in e]
    w2 = [v * w for v, w in zip(e, w1)]

    chosen = [jnp.where(a | b, 1.0, 0.0) for a, b in zip(oh1, oh2)]
    r = lax.broadcasted_iota(I32, (half, half), 0)
    c = lax.broadcasted_iota(I32, (half, half), 1)
    before = jnp.where(c < r, 1.0, 0.0).astype(BF16)
    within = [jnp.dot(before, v.astype(BF16), preferred_element_type=F32) for v in chosen]
    counts = [jnp.sum(v, axis=0, keepdims=True) for v in chosen]
    base = [carry_ref[0:1, :], carry_ref[0:1, :] + counts[0]]
    total = base[1] + counts[1]
    carry_ref[...] = jnp.broadcast_to(total, carry_ref.shape)
    cnt_ref[...] = jnp.broadcast_to(total, cnt_ref.shape)
    for n, rs in enumerate(halves):
        excl = within[n] + base[n]
        rank1 = jnp.sum(jnp.where(oh1[n], excl, 0.0), axis=-1, keepdims=True)
        rank2 = jnp.sum(jnp.where(oh2[n], excl, 0.0), axis=-1, keepdims=True)
        info = jnp.where(lane == 0, i1[n], 0.0)
        info = jnp.where(lane == 1, i2[n], info)
        info = jnp.where(lane == 2, rank1, info)
        info = jnp.where(lane == 3, rank2, info)
        info = jnp.where(lane == 4, w1[n], info)
        info = jnp.where(lane == 5, w2[n], info)
        info_ref[rs, :] = info


def _proj_router(h2d, x_bf, w_out, norm_gain, w_hi, w_lo, tm):
    T = h2d.shape[0]
    const = lambda i: (0, 0)
    row = lambda i: (i, 0)
    full = pl.BlockSpec((tm, D_MODEL), row)
    kern = functools.partial(_proj_router_kernel, tm=tm)
    return pl.pallas_call(
        kern,
        grid=(T // tm,),
        in_specs=[full, full, pl.BlockSpec((D_MODEL, D_MODEL), const), pl.BlockSpec((1, D_MODEL), const),
                  pl.BlockSpec((D_MODEL, LANES), const), pl.BlockSpec((D_MODEL, LANES), const)],
        out_specs=[full, full, pl.BlockSpec((tm, LANES), row), pl.BlockSpec((8, LANES), const)],
        out_shape=[
            jax.ShapeDtypeStruct((T, D_MODEL), F32),
            jax.ShapeDtypeStruct((T, D_MODEL), F32),
            jax.ShapeDtypeStruct((T, LANES), F32),
            jax.ShapeDtypeStruct((8, LANES), F32),
        ],
        scratch_shapes=[pltpu.VMEM((8, LANES), F32)],
        compiler_params=_cparams("arbitrary"),
        name="hgrn_outproj_moe_router",
    )(h2d, x_bf, w_out, norm_gain, w_hi, w_lo)


def _dispatch_kernel(zs_ref, pos_ref, prev_pos_ref, xn_ref, xs_hbm, zero_ref, rows_ref, sem, zsem, *, tt, tme):
    i = pl.program_id(0)
    slot = i % 2

    @pl.when(i == 0)
    def _():
        zero_ref[...] = jnp.zeros(zero_ref.shape, F32)
        for e in range(2 * N_EXPERTS):
            row0 = pl.multiple_of(zs_ref[e], tme)
            fill = pltpu.make_async_copy(zero_ref, xs_hbm.at[pl.ds(row0, tme)], zsem)
            fill.start()
            fill.wait()

    def copy(pref, s, t, c):
        return pltpu.make_async_copy(
            rows_ref.at[s, pl.ds(t, 1)], xs_hbm.at[pl.ds(pref[0, 0, 2 * t + c], 1)], sem.at[s])

    def start(t, carry):
        copy(pos_ref, slot, t, 0).start()
        copy(pos_ref, slot, t, 1).start()
        return carry

    def wait_for(pref, s):
        def wait(t, carry):
            copy(pref, s, t, 0).wait()
            copy(pref, s, t, 1).wait()
            return carry
        lax.fori_loop(0, tt, wait, 0, unroll=8)

    rows_ref[slot] = xn_ref[...]
    lax.fori_loop(0, tt, start, 0, unroll=8)

    @pl.when(i > 0)
    def _():
        wait_for(prev_pos_ref, 1 - slot)

    @pl.when(i == pl.num_programs(0) - 1)
    def _():
        wait_for(pos_ref, slot)


def _dispatch(xn, pos, last_tile_start, n_rows, tt, tme):
    T = xn.shape[0]
    pos3 = pos.reshape(T // tt, 1, 2 * tt)
    kern = functools.partial(_dispatch_kernel, tt=tt, tme=tme)
    grid_spec = pltpu.PrefetchScalarGridSpec(
        num_scalar_prefetch=1,
        grid=(T // tt,),
        in_specs=[
            pl.BlockSpec((1, 1, 2 * tt), lambda i, zs: (i, 0, 0), memory_space=pltpu.SMEM),
            pl.BlockSpec((1, 1, 2 * tt), lambda i, zs: (jnp.maximum(i - 1, 0), 0, 0), memory_space=pltpu.SMEM),
            pl.BlockSpec((tt, D_MODEL), lambda i, zs: (i, 0)),
        ],
        out_specs=pl.BlockSpec(memory_space=pl.ANY),
        scratch_shapes=[pltpu.VMEM((tme, D_MODEL), F32), pltpu.VMEM((2, tt, D_MODEL), F32),
                        pltpu.SemaphoreType.DMA((2,)), pltpu.SemaphoreType.DMA(())],
    )
    return pl.pallas_call(
        kern,
        grid_spec=grid_spec,
        out_shape=jax.ShapeDtypeStruct((n_rows, D_MODEL), F32),
        compiler_params=_cparams("arbitrary"),
        name="moe_dispatch",
    )(last_tile_start, pos3, pos3, xn)


def _expert_kernel(te_ref, na_ref, xs_ref, wg_ref, wu_ref, wd_ref, y_ref, *, nchunk):
    del te_ref
    active = pl.program_id(0) < na_ref[0]

    @pl.when(jnp.logical_not(active))
    def _():
        y_ref[...] = jnp.zeros(y_ref.shape, F32)

    @pl.when(active)
    def _():
        xb = xs_ref[...].astype(BF16)
        tf = D_FF_EXPERT // nchunk
        acc = None
        for c in range(nchunk):
            cols = slice(c * tf, (c + 1) * tf)
            g = jnp.dot(xb, wg_ref[0, :, cols], preferred_element_type=F32)
            u = jnp.dot(xb, wu_ref[0, :, cols], preferred_element_type=F32)
            a = (_silu(g) * u).astype(BF16)
            d = jnp.dot(a, wd_ref[0, cols, :], preferred_element_type=F32)
            acc = d if acc is None else acc + d
        y_ref[...] = acc


def _experts(xs, tile_expert, n_active, wg, wu, wd, tme, nchunk):
    n_rows = xs.shape[0]
    n_tiles = n_rows // tme

    def tile(i, na):
        return jnp.minimum(i, na[0] - 1)

    resident = pl.Buffered(1)
    grid_spec = pltpu.PrefetchScalarGridSpec(
        num_scalar_prefetch=2,
        grid=(n_tiles,),
        in_specs=[
            pl.BlockSpec((tme, D_MODEL), lambda i, te, na: (tile(i, na), 0)),
            pl.BlockSpec((1, D_MODEL, D_FF_EXPERT), lambda i, te, na: (te[tile(i, na)], 0, 0),
                         pipeline_mode=resident),
            pl.BlockSpec((1, D_MODEL, D_FF_EXPERT), lambda i, te, na: (te[tile(i, na)], 0, 0)),
            pl.BlockSpec((1, D_FF_EXPERT, D_MODEL), lambda i, te, na: (te[tile(i, na)], 0, 0)),
        ],
        out_specs=pl.BlockSpec((tme, D_MODEL), lambda i, te, na: (i, 0)),
    )
    return pl.pallas_call(
        functools.partial(_expert_kernel, nchunk=nchunk),
        grid_spec=grid_spec,
        out_shape=jax.ShapeDtypeStruct((n_rows, D_MODEL), F32),
        compiler_params=_cparams("arbitrary"),
        name="moe_experts",
    )(tile_expert, n_active, xs, wg, wu, wd)


def _combine_kernel(pos_ref, next_pos_ref, info_ref, h_ref, y_hbm, o_ref, buf_ref, sem, *, tc):
    i = pl.program_id(0)
    slot = i % 2

    def copy(pref, s, t, c):
        return pltpu.make_async_copy(
            y_hbm.at[pl.ds(pref[0, 0, 2 * t + c], 1)], buf_ref.at[s, c, pl.ds(t, 1)], sem.at[s])

    def gather(pref, s):
        def start(t, carry):
            copy(pref, s, t, 0).start()
            copy(pref, s, t, 1).start()
            return carry
        lax.fori_loop(0, tc, start, 0, unroll=8)

    def wait(t, carry):
        copy(pos_ref, slot, t, 0).wait()
        copy(pos_ref, slot, t, 1).wait()
        return carry

    @pl.when(i == 0)
    def _():
        gather(pos_ref, slot)

    @pl.when(i + 1 < pl.num_programs(0))
    def _():
        gather(next_pos_ref, 1 - slot)

    lax.fori_loop(0, tc, wait, 0, unroll=8)
    info = info_ref[...]
    lane = lax.broadcasted_iota(I32, (tc, LANES), 1)
    w1 = jnp.sum(jnp.where(lane == 4, info, 0.0), axis=-1, keepdims=True)
    w2 = jnp.sum(jnp.where(lane == 5, info, 0.0), axis=-1, keepdims=True)
    o_ref[...] = h_ref[...] + (w1 * buf_ref[slot, 0] + w2 * buf_ref[slot, 1])


def _combine(h2d, info, pos, y, tc):
    T = h2d.shape[0]
    n = T // tc
    pos3 = pos.reshape(n, 1, 2 * tc)
    kern = functools.partial(_combine_kernel, tc=tc)
    return pl.pallas_call(
        kern,
        grid=(n,),
        in_specs=[
            pl.BlockSpec((1, 1, 2 * tc), lambda i: (i, 0, 0), memory_space=pltpu.SMEM),
            pl.BlockSpec((1, 1, 2 * tc), lambda i: (jnp.minimum(i + 1, n - 1), 0, 0), memory_space=pltpu.SMEM),
            pl.BlockSpec((tc, LANES), lambda i: (i, 0)),
            pl.BlockSpec((tc, D_MODEL), lambda i: (i, 0)),
            pl.BlockSpec(memory_space=pl.ANY),
        ],
        out_specs=pl.BlockSpec((tc, D_MODEL), lambda i: (i, 0)),
        out_shape=jax.ShapeDtypeStruct((T, D_MODEL), F32),
        scratch_shapes=[pltpu.VMEM((2, 2, tc, D_MODEL), F32), pltpu.SemaphoreType.DMA((2,))],
        compiler_params=_cparams("arbitrary"),
        name="moe_combine",
    )(pos3, pos3, info, h2d, y)


def _bf16_pieces(x, n):
    pieces, r = [], x.astype(F32)
    for _ in range(n):
        p = r.astype(BF16)
        pieces.append(p)
        r = r - p.astype(F32)
    return pieces


def _alibi_slopes():
    return [2.0 ** (-8.0 * (h + 1.0) / N_ATTN_HEADS) for h in range(N_ATTN_HEADS)]


def _attn_col_gain(dil_q_gain, dil_k_gain, diff_q_gain, diff_k_gain):
    rep = lambda g, n: jnp.tile(g.astype(F32), n)
    ones = jnp.ones((COLB,), F32)
    dq = rep(dil_q_gain, COLB // HEAD_DIM) * (HEAD_DIM ** -0.5 * LOG2E)
    dk = rep(dil_k_gain, COLB // HEAD_DIM)
    fq = rep(diff_q_gain, COLB // DIFF_QK_DIM) * (DIFF_QK_DIM ** -0.5 * LOG2E)
    fk = rep(diff_k_gain, COLB // DIFF_QK_DIM)
    return jnp.stack([dq, dq, dq, dk, dk, dk, ones, ones, ones, fq, fk, ones])


def _lower_bound_rows(lb_logits, layer):
    sm = jax.nn.softmax(lb_logits.astype(F32), axis=0)
    lb = (jnp.cumsum(sm, axis=0) - sm[0])[layer]
    return jnp.zeros((8, D_MODEL), F32).at[0].set(jnp.log(lb)).at[1].set(jnp.log1p(-lb)).at[2].set(1.0 - lb)


def _layer0_and_hgrn_inproj(h2d, B, S, attn_norm, w_in, dq_g, dk_g, fq_g, fk_g, lq1, lk1, lq2, lk2, out_g,
                            w_out, ffn_norm, w_gate, w_up, w_down, hgrn_norm, hgrn_w_in, lb_logits):
    T = B * S
    tm = min(512, T)
    slopes = _alibi_slopes()
    *cls, dproj = _attn_inproj(h2d, attn_norm.reshape(1, D_MODEL).astype(F32), w_in.astype(BF16),
                               _attn_col_gain(dq_g, dk_g, fq_g, fk_g), tm)
    outs, lses = [], []
    for g, (window, dilation) in enumerate(DIL_GROUPS):
        assert window // dilation == QBLOCK and S % (dilation * QBLOCK) == 0 and tm % dilation == 0
        o, lse = _dilated_group(cls[g], B, S, dilation, tuple(slopes[4 * g:4 * g + 4]))
        outs.extend(o)
        lses.extend(lse)

    lam_init = 0.8 - 0.6 * math.exp(-0.3 * 0)
    lam = (jnp.exp(jnp.sum(lq1.astype(F32) * lk1.astype(F32)))
           - jnp.exp(jnp.sum(lq2.astype(F32) * lk2.astype(F32))) + lam_init).reshape(1)
    sl = jnp.asarray(slopes[N_DIL_HEADS:], F32) * LOG2E
    s_hi, s_lo = _bf16_pieces(sl, 2)
    slope_feat = jnp.zeros((8, LANES), F32)
    slope_feat = slope_feat.at[:N_DIFF_HEADS, 0].set(s_hi.astype(F32)).at[:N_DIFF_HEADS, 1].set(s_hi.astype(F32))
    slope_feat = slope_feat.at[:N_DIFF_HEADS, 2].set(s_lo.astype(F32)).at[:N_DIFF_HEADS, 3].set(s_lo.astype(F32))
    p_hi, p_lo = _bf16_pieces(jnp.arange(S, dtype=F32), 2)
    pos_feat = jnp.zeros((S, LANES), BF16)
    pos_feat = pos_feat.at[:, 0].set(p_hi).at[:, 1].set(p_lo).at[:, 2].set(p_hi).at[:, 3].set(p_lo)
    out_gain = (jnp.tile(out_g.astype(F32), COLB // HEAD_DIM) * (1.0 - lam_init)).reshape(1, COLB)
    diff = _diff_attention(dproj.reshape(B, S, _QKV), lam, slope_feat, pos_feat, out_gain, min(512, S)).reshape(T, COLB)

    h2, *gates = _layer0_tail(
        h2d, outs, lses, diff, w_out.astype(BF16), ffn_norm.reshape(1, D_MODEL).astype(F32),
        w_gate.astype(BF16), w_up.astype(BF16), w_down.astype(BF16),
        hgrn_norm.reshape(1, D_MODEL).astype(F32), hgrn_w_in.astype(BF16),
        _lower_bound_rows(lb_logits, 1), min(HGRN_ROWS, S))
    return h2, gates


def _layer1_rest(h2d, gates, B, S, out_gain, w_out, moe_norm, w_router, w_gate, w_up, w_down):
    o = _hgrn_recurrence(gates, out_gain.reshape(1, LANES).astype(F32), B, S, min(HGRN_ROWS, S))
    return _moe_block(h2d, o, w_out, moe_norm, w_router, w_gate, w_up, w_down)


def _moe_block(h2d, mixer_out, w_out, moe_norm, w_router, w_gate, w_up, w_down):
    T = h2d.shape[0]
    tme = 512
    w_pad = jnp.zeros((D_MODEL, LANES), F32).at[:, :N_EXPERTS].set(w_router.astype(F32))
    w_hi, w_lo = _bf16_pieces(w_pad, 2)
    h2d, xn, info, counts = _proj_router(h2d, mixer_out, w_out.astype(BF16),
                                         moe_norm.reshape(1, D_MODEL).astype(F32), w_hi, w_lo, min(512, T))
    cnt = counts[0, :N_EXPERTS].astype(I32)
    padded = ((cnt + tme - 1) // tme) * tme
    ends = jnp.cumsum(padded)
    starts = ends - padded
    experts = info[:, 0:2].astype(I32)
    ranks = info[:, 2:4].astype(I32)
    pos = (starts[experts] + ranks).reshape(-1)
    n_rows = 2 * T + N_EXPERTS * tme
    n_tiles = n_rows // tme
    tile_start = jnp.arange(n_tiles, dtype=I32) * tme
    tile_expert = jnp.minimum(jnp.sum((ends[None, :] <= tile_start[:, None]).astype(I32), axis=1), N_EXPERTS - 1)
    n_active = (ends[-1] // tme).astype(I32).reshape(1)
    tail = jnp.minimum(ends[-1] + jnp.arange(N_EXPERTS, dtype=I32) * tme, n_rows - tme)
    zero_tiles = jnp.concatenate([jnp.maximum(ends - tme, 0), tail]).astype(I32)
    xs = _dispatch(xn, pos, zero_tiles, n_rows, min(256, T), tme)
    y = _experts(xs, tile_expert, n_active, w_gate.astype(BF16), w_up.astype(BF16), w_down.astype(BF16),
                 tme, 2)
    return _combine(h2d, info, pos, y, min(256, T))


def kernel(x, attn_norm, attn_w_in, dil_q_gain, dil_k_gain, diff_q_gain, diff_k_gain, diff_lambda_q1, diff_lambda_k1, diff_lambda_q2, diff_lambda_k2, diff_out_gain, attn_w_out, ffn_norm, ffn_w_gate, ffn_w_up, ffn_w_down, hgrn_norm, hgrn_w_in, hgrn_lb_logits, hgrn_out_gain, hgrn_w_out, moe_norm, moe_w_router, moe_w_gate, moe_w_up, moe_w_down):
    B, S, D = x.shape
    assert D == D_MODEL
    h = x.astype(F32).reshape(B * S, D)
    h, gates = _layer0_and_hgrn_inproj(
        h, B, S, attn_norm[0], attn_w_in[0], dil_q_gain[0], dil_k_gain[0], diff_q_gain[0], diff_k_gain[0],
        diff_lambda_q1[0], diff_lambda_k1[0], diff_lambda_q2[0], diff_lambda_k2[0], diff_out_gain[0],
        attn_w_out[0], ffn_norm[0], ffn_w_gate[0], ffn_w_up[0], ffn_w_down[0],
        hgrn_norm[0], hgrn_w_in[0], hgrn_lb_logits)
    h = _layer1_rest(h, gates, B, S, hgrn_out_gain[0], hgrn_w_out[0], moe_norm[0], moe_w_router[0],
                     moe_w_gate[0], moe_w_up[0], moe_w_down[0])
    return h.reshape(B, S, D).astype(x.dtype)
```

```python
import functools
import math

import numpy as np
import jax
import jax.numpy as jnp
from jax import lax
from jax.experimental import pallas as pl
from jax.experimental.pallas import tpu as pltpu

F32 = jnp.float32
BF16 = jnp.bfloat16
I32 = jnp.int32

D_MODEL = 1024
HEAD_DIM = 64
N_ATTN_HEADS = 16
N_DIL_HEADS = 12
N_DIFF_HEADS = 4
DIL_GROUPS = ((128, 1), (512, 4), (2048, 16))
DIFF_QK_DIM = 32
ATTN_IN_WIDTH = 3072
QBLOCK = 128
HGRN_HEADS = 8
HGRN_CHUNK = 64
DEC_ROWS = 8
D_FF = 2816
N_EXPERTS = 8
D_FF_EXPERT = 3584
EPS = 1e-6
NEG_INF = -1e30
LOG2E = 1.4426950408889634
LANES = 128
COLB = 256
VMEM_LIMIT = 56 * 1024 * 1024

TM_INPROJ = 512
UB_DILATED = 512
TQ_DIFF = 512
HGRN_ROWS = 256
TM_ROUTER = 512
TOKENS_PER_DMA_STEP = 256
TME = 512
EXPERT_FF_CHUNKS = 2


def _cparams(*sem):
    return pltpu.CompilerParams(dimension_semantics=sem, vmem_limit_bytes=VMEM_LIMIT)


def _split_dot(x, m, terms):
    acc = None
    r = x
    for t in range(terms):
        part = r.astype(BF16)
        d = jnp.dot(part, m, preferred_element_type=F32)
        acc = d if acc is None else acc + d
        if t + 1 < terms:
            r = r - part.astype(F32)
    return acc


def _seg_matrix(n, seg):
    i = np.arange(n)
    return jnp.asarray((i[:, None] // seg == i[None, :] // seg).astype(BF16))


def _rms_rows(x, gain_row):
    return x * lax.rsqrt(jnp.mean(x * x, axis=-1, keepdims=True) + EPS) * gain_row


def _sigmoid(x):
    return 1.0 / (1.0 + jnp.exp(-x))


def _silu(x):
    return x * _sigmoid(x)


_ATTN_NORM_SEG = (64, 64, 64, 64, 64, 64, 0, 0, 0, 32, 32, 0)
_QKV = 3 * COLB


def _attn_inproj_kernel(h_ref, ng_ref, w_ref, cg_ref, s64_ref, s32_ref,
                        c0_ref, c1_ref, c2_ref, df_ref, y_ref, *, tm):
    cls_refs = (c0_ref, c1_ref, c2_ref)
    xb = _rms_rows(h_ref[...], ng_ref[...]).astype(BF16)
    blocks = range(len(_ATTN_NORM_SEG))
    ys = [jnp.dot(xb, w_ref[:, c * COLB:(c + 1) * COLB], preferred_element_type=F32) for c in blocks]
    ms = {c: _split_dot(ys[c] * ys[c], s64_ref[...] if seg == 64 else s32_ref[...], 2) * (1.0 / seg)
          for c, seg in enumerate(_ATTN_NORM_SEG) if seg}
    ys = [ys[c] * lax.rsqrt(ms[c] + EPS) * cg_ref[c:c + 1, :] if c in ms else ys[c] for c in blocks]
    strided = []
    for c in blocks:
        part, g = divmod(c, 3)
        if part == 3:
            df_ref[:, g * COLB:(g + 1) * COLB] = ys[c].astype(BF16)
        elif DIL_GROUPS[g][1] == 1:
            cls_refs[g][:, part * COLB:(part + 1) * COLB] = ys[c].astype(BF16)
        else:
            slot = len(strided)
            y_ref[slot, 0] = ys[c][:, :LANES]
            y_ref[slot, 1] = ys[c][:, LANES:]
            strided.append((slot, part, g))
    for slot, part, g in strided:
        d = DIL_GROUPS[g][1]
        for r in range(d):
            col = r * _QKV + part * COLB
            rows = pl.ds(r, tm // d, stride=d)
            cls_refs[g][:, col:col + COLB] = jnp.concatenate(
                [y_ref[slot, 0, rows, :], y_ref[slot, 1, rows, :]], axis=1).astype(BF16)


def _attn_inproj(h2d, norm_gain, w_bf, col_gain, tm):
    T = h2d.shape[0]
    const = lambda i: (0, 0)
    row = lambda i: (i, 0)
    dils = [d for _, d in DIL_GROUPS]
    kern = functools.partial(_attn_inproj_kernel, tm=tm)
    return pl.pallas_call(
        kern,
        grid=(T // tm,),
        in_specs=[
            pl.BlockSpec((tm, D_MODEL), row),
            pl.BlockSpec((1, D_MODEL), const),
            pl.BlockSpec((D_MODEL, ATTN_IN_WIDTH), const),
            pl.BlockSpec((ATTN_IN_WIDTH // COLB, COLB), const),
            pl.BlockSpec((COLB, COLB), const),
            pl.BlockSpec((COLB, COLB), const),
        ],
        out_specs=[pl.BlockSpec((tm // d, d * _QKV), row) for d in dils] + [pl.BlockSpec((tm, _QKV), row)],
        out_shape=[jax.ShapeDtypeStruct((T // d, d * _QKV), BF16) for d in dils]
                  + [jax.ShapeDtypeStruct((T, _QKV), BF16)],
        scratch_shapes=[pltpu.VMEM((3 * sum(d > 1 for d in dils), 2, tm, LANES), F32)],
        compiler_params=_cparams("parallel"),
        name="attn_inproj",
    )(h2d, norm_gain, w_bf, col_gain, _seg_matrix(COLB, 64), _seg_matrix(COLB, 32))


def _dil_kernel(q_ref, kp_ref, kc_ref, vp_ref, vc_ref, o0_ref, o1_ref, l0_ref, l1_ref,
                *, slopes, dilation, ub):
    u = pl.program_id(1)
    r = pl.program_id(2)
    q = q_ref[...]
    kcat = jnp.concatenate([kp_ref[...], kc_ref[...]], axis=0)
    vcat = jnp.concatenate([vp_ref[...], vc_ref[...]], axis=0)
    qi = lax.broadcasted_iota(I32, (QBLOCK, 2 * QBLOCK), 0)
    kj = lax.broadcasted_iota(I32, (QBLOCK, 2 * QBLOCK), 1)
    step = qi + QBLOCK - kj
    in_window = (step >= 0) & (step <= QBLOCK)
    first_window = in_window & ((kj >= QBLOCK) | (u > 0))
    stepf = step.astype(F32)
    bias = [(-slopes[h] * dilation * LOG2E) * stepf for h in range(COLB // HEAD_DIM)]
    mask_bias = [jnp.where(in_window, b, NEG_INF) for b in bias]
    mask_bias_first = [jnp.where(first_window, b, NEG_INF) for b in bias]
    lane = lax.broadcasted_iota(I32, (QBLOCK, LANES), 1)
    low_half = lane < HEAD_DIM
    jobs = [(i, pair, hh) for i in range(ub // QBLOCK) for pair in range(2) for hh in range(2)]
    nt = (((1,), (1,)), ((), ()))

    def masked_q(i, pair, hh):
        qp = q[i * QBLOCK:(i + 1) * QBLOCK, pair * LANES:(pair + 1) * LANES]
        return jnp.where(low_half if hh == 0 else ~low_half, qp, jnp.zeros_like(qp))

    scores = [lax.dot_general(masked_q(i, pair, hh), kcat[i * QBLOCK:(i + 2) * QBLOCK, pair * LANES:(pair + 1) * LANES],
                              nt, preferred_element_type=F32) for i, pair, hh in jobs]
    stats = []
    for (i, pair, hh), s in zip(jobs, scores):
        s = s + (mask_bias_first if i == 0 else mask_bias)[pair * 2 + hh]
        m = jnp.max(s, axis=-1, keepdims=True)
        e = jnp.exp2(s - m)
        stats.append((m, jnp.sum(e, axis=-1, keepdims=True), e.astype(BF16)))
    pvs = [jnp.dot(e, vcat[i * QBLOCK:(i + 2) * QBLOCK, pair * LANES:(pair + 1) * LANES], preferred_element_type=F32)
           for (i, pair, hh), (_, _, e) in zip(jobs, stats)]
    for n in range(0, len(jobs), 2):
        i, pair, _ = jobs[n]
        outs = [pvs[n + hh] / stats[n + hh][1] for hh in range(2)]
        lses = [jnp.broadcast_to(stats[n + hh][0] + jnp.log2(stats[n + hh][1]), (QBLOCK, LANES)) for hh in range(2)]
        if dilation == 1:
            rows = slice(i * QBLOCK, (i + 1) * QBLOCK)
        else:
            rows = pl.ds(r + i * QBLOCK * dilation, QBLOCK, stride=dilation)
        (o0_ref, o1_ref)[pair][rows, :] = jnp.where(low_half, outs[0], outs[1])
        (l0_ref, l1_ref)[pair][rows, :] = jnp.where(low_half, lses[0], lses[1])


def _dilated_group(cls, B, S, dilation, slopes):
    U = S // dilation
    ub = min(UB_DILATED, U)
    sub = ub // QBLOCK
    view = cls.reshape(B, U, dilation * _QKV)
    prev = lambda u: jnp.maximum(u * sub - 1, 0)
    kern = functools.partial(_dil_kernel, slopes=slopes, dilation=dilation, ub=ub)
    out_spec = pl.BlockSpec((None, ub * dilation, LANES), lambda b, u, r: (b, u, 0))
    res = pl.pallas_call(
        kern,
        grid=(B, U // ub, dilation),
        in_specs=[
            pl.BlockSpec((None, ub, COLB), lambda b, u, r: (b, u, 3 * r)),
            pl.BlockSpec((None, QBLOCK, COLB), lambda b, u, r: (b, prev(u), 3 * r + 1)),
            pl.BlockSpec((None, ub, COLB), lambda b, u, r: (b, u, 3 * r + 1)),
            pl.BlockSpec((None, QBLOCK, COLB), lambda b, u, r: (b, prev(u), 3 * r + 2)),
            pl.BlockSpec((None, ub, COLB), lambda b, u, r: (b, u, 3 * r + 2)),
        ],
        out_specs=[out_spec] * 4,
        out_shape=[jax.ShapeDtypeStruct((B, S, LANES), F32)] * 4,
        compiler_params=_cparams("parallel", "parallel", "arbitrary"),
        name=f"dilated_attn_d{dilation}",
    )(view, view, view, view, view)
    res = [a.reshape(B * S, LANES) for a in res]
    return res[:2], res[2:]


def _diff_kernel(lam_ref, q_ref, k_ref, v_ref, pos_ref, sl_ref, og_ref, s64_ref, o_ref,
                 m_ref, l_ref, acc_ref, qa_ref, *, tq):
    i = pl.program_id(1)
    q = q_ref[...]
    lane = lax.broadcasted_iota(I32, (tq, LANES), 1)
    low_half = lane < HEAD_DIM
    for pair in range(2):
        qp = q[:, pair * LANES:(pair + 1) * LANES]
        for hh in range(2):
            feat = jnp.broadcast_to(sl_ref[pair * 2 + hh:pair * 2 + hh + 1, :], (tq, LANES)).astype(BF16)
            for mu in range(2):
                lo = hh * HEAD_DIM + mu * DIFF_QK_DIM
                sel = (lane >= lo) & (lane < lo + DIFF_QK_DIM)
                r = hh * 2 + mu
                qa_ref[pair, r * tq:(r + 1) * tq, :] = jnp.concatenate(
                    [jnp.where(sel, qp, jnp.zeros_like(qp)), feat], axis=1)

    m_ref[...] = jnp.full(m_ref.shape, NEG_INF, F32)
    l_ref[...] = jnp.zeros(l_ref.shape, F32)
    acc_ref[...] = jnp.zeros(acc_ref.shape, F32)

    def scores(j, nk):
        ks = pl.multiple_of(j * tq, tq)
        kblk = k_ref[pl.ds(ks, nk * tq), :]
        pblk = pos_ref[pl.ds(ks, nk * tq), :]
        s_alls = []
        for pair in range(2):
            kaug = jnp.concatenate([kblk[:, pair * LANES:(pair + 1) * LANES], pblk], axis=1)
            s_alls.append(lax.dot_general(qa_ref[pair], kaug, (((1,), (1,)), ((), ())),
                                          preferred_element_type=F32))
        return s_alls, v_ref[pl.ds(ks, nk * tq), :]

    def finish(s_alls, vblk, masked):
        tk = vblk.shape[0]
        ones = jnp.ones((tk, LANES), BF16)
        if masked:
            row = lax.broadcasted_iota(I32, (tq, tk), 0)
            col = lax.broadcasted_iota(I32, (tq, tk), 1)
            causal = col <= row
        ss = [s_alls[idx // 4][(idx % 4) * tq:(idx % 4 + 1) * tq] for idx in range(8)]
        if masked:
            ss = [jnp.where(causal, s, NEG_INF) for s in ss]
        m_prev = [m_ref[idx] for idx in range(8)]
        m_next = [jnp.maximum(m, jnp.max(s, axis=-1, keepdims=True)) for m, s in zip(m_prev, ss)]
        for idx in range(8):
            m_ref[idx] = m_next[idx]
        alphas = [jnp.exp2(a - b) for a, b in zip(m_prev, m_next)]
        ps = [jnp.exp2(s - jnp.concatenate([m] * (tk // LANES), axis=1)).astype(BF16) for s, m in zip(ss, m_next)]
        pvs = []
        for pair in range(2):
            vaug = jnp.concatenate([vblk[:, pair * LANES:(pair + 1) * LANES], ones], axis=1)
            pvs.append(jnp.dot(jnp.concatenate(ps[pair * 4:pair * 4 + 4], axis=0), vaug,
                               preferred_element_type=F32))
        for idx in range(8):
            part = pvs[idx // 4][(idx % 4) * tq:(idx % 4 + 1) * tq]
            acc_ref[idx] = alphas[idx] * acc_ref[idx] + part[:, :LANES]
            l_ref[idx] = alphas[idx] * l_ref[idx] + part[:, LANES:]

    def two_full_blocks(jj, carry):
        finish(*scores(2 * jj, 2), False)
        return carry

    lax.fori_loop(0, i // 2, two_full_blocks, 0)

    @pl.when(i % 2 == 1)
    def _():
        finish(*scores(i - 1, 1), False)

    finish(*scores(i, 1), True)

    lam = lam_ref[0]
    for pair in range(2):
        o = None
        for mu in range(2):
            lo_idx, hi_idx = pair * 4 + mu, pair * 4 + 2 + mu
            term = jnp.where(low_half, acc_ref[lo_idx] / l_ref[lo_idx], acc_ref[hi_idx] / l_ref[hi_idx])
            o = term if mu == 0 else o - lam * term
        ms = _split_dot(o * o, s64_ref[...], 2) * (1.0 / HEAD_DIM)
        o = o * lax.rsqrt(ms + EPS) * og_ref[:, pair * LANES:(pair + 1) * LANES]
        o_ref[:, pair * LANES:(pair + 1) * LANES] = o.astype(BF16)


def _diff_attention(proj, lam, slope_feat, pos_feat, out_gain, tq):
    B, S, W = proj.shape
    kern = functools.partial(_diff_kernel, tq=tq)
    return pl.pallas_call(
        kern,
        grid=(B, S // tq),
        in_specs=[
            pl.BlockSpec(memory_space=pltpu.SMEM),
            pl.BlockSpec((None, tq, COLB), lambda b, i: (b, i, 0)),
            pl.BlockSpec((None, S, COLB), lambda b, i: (b, 0, 1), pipeline_mode=pl.Buffered(1)),
            pl.BlockSpec((None, S, COLB), lambda b, i: (b, 0, 2), pipeline_mode=pl.Buffered(1)),
            pl.BlockSpec((S, LANES), lambda b, i: (0, 0), pipeline_mode=pl.Buffered(1)),
            pl.BlockSpec((8, LANES), lambda b, i: (0, 0)),
            pl.BlockSpec((1, COLB), lambda b, i: (0, 0)),
            pl.BlockSpec((LANES, LANES), lambda b, i: (0, 0)),
        ],
        out_specs=pl.BlockSpec((None, tq, COLB), lambda b, i: (b, i, 0)),
        out_shape=jax.ShapeDtypeStruct((B, S, COLB), BF16),
        scratch_shapes=[
            pltpu.VMEM((8, tq, LANES), F32),
            pltpu.VMEM((8, tq, LANES), F32),
            pltpu.VMEM((8, tq, LANES), F32),
            pltpu.VMEM((2, 4 * tq, 2 * LANES), BF16),
        ],
        compiler_params=_cparams("parallel", "parallel"),
        name="diff_attn",
    )(lam, proj, proj, proj, pos_feat, slope_feat, out_gain, _seg_matrix(LANES, HEAD_DIM))


def _merge_heads(o_refs, l_refs, d_ref):
    n = len(o_refs)
    pieces = [None] * n
    for pair in range(2):
        ls = [l_refs[2 * g + pair][...] for g in range(n // 2)]
        mx = functools.reduce(jnp.maximum, ls)
        es = [jnp.exp2(l - mx) for l in ls]
        inv = 1.0 / functools.reduce(jnp.add, es)
        for g, e in enumerate(es):
            pieces[2 * g + pair] = (e * inv * o_refs[2 * g + pair][...]).astype(BF16)
    return jnp.concatenate(pieces + [d_ref[...]], axis=1)


def _split_dot_lhs(m, x):
    hi = x.astype(BF16)
    lo = (x - hi.astype(F32)).astype(BF16)
    return jnp.dot(m, hi, preferred_element_type=F32) + jnp.dot(m, lo, preferred_element_type=F32)


def _chunk_mask(rows):
    ti = lax.broadcasted_iota(I32, (rows, rows), 0)
    si = lax.broadcasted_iota(I32, (rows, rows), 1)
    return (si <= ti) & (si >= (ti // HGRN_CHUNK) * HGRN_CHUNK)


def _hgrn_gates(xb, w_ref, lb_ref, qs_ref, ks_ref, qi_ref, kst_ref, v_ref, g_ref, dec_ref):
    C = HGRN_CHUNK
    tm = xb.shape[0]
    nc = tm // C
    yq, fl, yv, yg = [jnp.dot(xb, w_ref[:, c * D_MODEL:(c + 1) * D_MODEL], preferred_element_type=F32)
                      for c in range(4)]
    v_ref[...] = yv.astype(BF16)
    g_ref[...] = _sigmoid(yg).astype(BF16)
    q = _silu(yq)
    log_lb, log1m_lb, one_m_lb = lb_ref[0:1, :], lb_ref[1:2, :], lb_ref[2:3, :]
    t = jnp.exp(-jnp.abs(fl))
    r = 1.0 / (1.0 + t)
    c = log1m_lb + (jnp.minimum(fl, 0.0) + jnp.log(r))
    lf = jnp.maximum(log_lb, c) + jnp.log(1.0 + jnp.exp(-jnp.abs(log_lb - c)))
    k = one_m_lb * jnp.where(fl >= 0.0, t * r, r)
    tri = jnp.where(_chunk_mask(tm), 1.0, 0.0).astype(BF16)
    b = _split_dot_lhs(tri, lf)
    chunk_row = lambda i: jnp.concatenate(
        [jnp.broadcast_to(b[n * C + i:n * C + i + 1, :], (C, D_MODEL)) for n in range(nc)], axis=0)
    b_mid = chunk_row(C // 2)
    b_last = chunk_row(C - 1)
    qs_ref[...] = (q * jnp.exp(b - b_mid)).astype(BF16)
    ks_ref[...] = (k * jnp.exp(b_mid - b)).astype(BF16)
    qi_ref[...] = (q * jnp.exp(b)).astype(BF16)
    kst_ref[...] = (k * jnp.exp(b_last - b)).astype(BF16)
    last_rows = jnp.concatenate([b[n * C + C - 1:n * C + C, :] for n in range(nc)], axis=0)
    dec_ref[...] = jnp.concatenate(
        [jnp.exp(last_rows), jnp.zeros((dec_ref.shape[0] - nc, D_MODEL), F32)], axis=0)


def _layer0_tail_kernel(h_ref, *refs):
    n = 2 * len(DIL_GROUPS)
    o_refs, l_refs = refs[:n], refs[n:2 * n]
    (d_ref, wo_ref, fng_ref, wg_ref, wu_ref, wd_ref, hng_ref, wh_ref, lb_ref,
     h2_ref, qs_ref, ks_ref, qi_ref, kst_ref, v_ref, g_ref, dec_ref) = refs[2 * n:]
    mixed = _merge_heads(o_refs, l_refs, d_ref)
    h1 = h_ref[...] + jnp.dot(mixed, wo_ref[...], preferred_element_type=F32)
    xb = _rms_rows(h1, fng_ref[...]).astype(BF16)
    g = jnp.dot(xb, wg_ref[...], preferred_element_type=F32)
    u = jnp.dot(xb, wu_ref[...], preferred_element_type=F32)
    a = (_silu(g) * u).astype(BF16)
    h2 = h1 + jnp.dot(a, wd_ref[...], preferred_element_type=F32)
    h2_ref[...] = h2
    _hgrn_gates(_rms_rows(h2, hng_ref[...]).astype(BF16), wh_ref, lb_ref,
                qs_ref, ks_ref, qi_ref, kst_ref, v_ref, g_ref, dec_ref)


def _layer0_tail(h2d, outs, lses, diff, w_out, ffn_gain, wg, wu, wd, hgrn_gain, w_in, lb_rows, tm):
    T = h2d.shape[0]
    row = lambda i: (i, 0)
    full = pl.BlockSpec((tm, D_MODEL), row)
    half = pl.BlockSpec((tm, LANES), row)
    held = lambda shape: pl.BlockSpec(shape, lambda i: (0, 0), pipeline_mode=pl.Buffered(1))
    bf = jax.ShapeDtypeStruct((T, D_MODEL), BF16)
    f32 = jax.ShapeDtypeStruct((T, D_MODEL), F32)
    return pl.pallas_call(
        _layer0_tail_kernel,
        grid=(T // tm,),
        in_specs=[full] + [half] * (len(outs) + len(lses))
                 + [pl.BlockSpec((tm, COLB), row), held((D_MODEL, D_MODEL)), held((1, D_MODEL)),
                    held((D_MODEL, D_FF)), held((D_MODEL, D_FF)), held((D_FF, D_MODEL)),
                    held((1, D_MODEL)), held((D_MODEL, 4 * D_MODEL)), held((8, D_MODEL))],
        out_specs=[full] * 7 + [pl.BlockSpec((DEC_ROWS, D_MODEL), row)],
        out_shape=[f32] + [bf] * 6 + [jax.ShapeDtypeStruct((T // tm * DEC_ROWS, D_MODEL), F32)],
        compiler_params=_cparams("parallel"),
        name="layer0_tail_hgrn_inproj",
    )(h2d, *outs, *lses, diff, w_out, ffn_gain, wg, wu, wd, hgrn_gain, w_in, lb_rows)


def _hgrn_kernel(qs_ref, ks_ref, qi_ref, kst_ref, v_ref, g_ref, dec_ref, og_ref, o_ref, st_ref, *, rows):
    @pl.when(pl.program_id(1) == 0)
    def _():
        st_ref[...] = jnp.zeros(st_ref.shape, F32)

    C = HGRN_CHUNK
    nc = rows // C
    heads = range(HGRN_HEADS)
    lanes = [slice(hd * LANES, (hd + 1) * LANES) for hd in heads]
    causal = _chunk_mask(rows)
    nt = (((1,), (1,)), ((), ()))
    tn = (((0,), (0,)), ((), ()))
    scores = [lax.dot_general(qs_ref[:, lanes[hd]], ks_ref[:, lanes[hd]], nt, preferred_element_type=F32)
              for hd in heads]
    incs = [[lax.dot_general(v_ref[c * C:(c + 1) * C, lanes[hd]], kst_ref[c * C:(c + 1) * C, lanes[hd]], tn,
                             preferred_element_type=F32) for c in range(nc)] for hd in heads]
    probs = [jnp.where(causal, s, 0.0).astype(BF16) for s in scores]
    outs = [jnp.dot(probs[hd], v_ref[:, lanes[hd]], preferred_element_type=F32) for hd in heads]
    states = []
    for hd in heads:
        st, per_chunk = st_ref[hd], []
        for c in range(nc):
            per_chunk.append(st.astype(BF16))
            st = dec_ref[c:c + 1, lanes[hd]] * st + incs[hd][c]
        st_ref[hd] = st
        states.append(per_chunk)
    for hd in heads:
        inter = [lax.dot_general(qi_ref[c * C:(c + 1) * C, lanes[hd]], states[hd][c], nt,
                                 preferred_element_type=F32) for c in range(nc)]
        o = outs[hd] + jnp.concatenate(inter, axis=0)
        o = o * lax.rsqrt(jnp.mean(o * o, axis=-1, keepdims=True) + EPS) * og_ref[...]
        o_ref[:, lanes[hd]] = (o * g_ref[:, lanes[hd]].astype(F32)).astype(BF16)


def _hgrn_recurrence(gates, out_gain, B, S, rows):
    T = B * S
    nseq = S // rows
    row = pl.BlockSpec((rows, D_MODEL), lambda b, s: (b * nseq + s, 0))
    dec = pl.BlockSpec((DEC_ROWS, D_MODEL), lambda b, s: (b * nseq + s, 0))
    kern = functools.partial(_hgrn_kernel, rows=rows)
    return pl.pallas_call(
        kern,
        grid=(B, nseq),
        in_specs=[row] * 6 + [dec, pl.BlockSpec((1, LANES), lambda b, s: (0, 0))],
        out_specs=row,
        out_shape=jax.ShapeDtypeStruct((T, D_MODEL), BF16),
        scratch_shapes=[pltpu.VMEM((HGRN_HEADS, LANES, LANES), F32)],
        compiler_params=_cparams("parallel", "arbitrary"),
        name="hgrn_recurrence",
    )(*gates, out_gain)


def _proj_router_kernel(h_ref, x_ref, wo_ref, ng_ref, whi_ref, wlo_ref,
                        h2_ref, xn_ref, info_ref, cnt_ref, carry_ref, *, tm):
    @pl.when(pl.program_id(0) == 0)
    def _():
        carry_ref[...] = jnp.zeros(carry_ref.shape, F32)

    half = tm // 2
    halves = [slice(n * half, (n + 1) * half) for n in range(2)]
    h2 = [h_ref[rs, :] + jnp.dot(x_ref[rs, :], wo_ref[...], preferred_element_type=F32) for rs in halves]
    xn = [_rms_rows(v, ng_ref[...]) for v in h2]
    for rs, a, c in zip(halves, h2, xn):
        h2_ref[rs, :] = a
        xn_ref[rs, :] = c
    xhi = [v.astype(BF16) for v in xn]
    xlo = [(v - hi.astype(F32)).astype(BF16) for v, hi in zip(xn, xhi)]
    lane = lax.broadcasted_iota(I32, (half, LANES), 1)
    lanef = lane.astype(F32)
    logits = [jnp.where(lane < N_EXPERTS,
                        jnp.dot(hi, whi_ref[...], preferred_element_type=F32)
                        + jnp.dot(hi, wlo_ref[...], preferred_element_type=F32)
                        + jnp.dot(lo, whi_ref[...], preferred_element_type=F32), -jnp.inf)
              for hi, lo in zip(xhi, xlo)]
    m1 = [jnp.max(l, axis=-1, keepdims=True) for l in logits]
    i1 = [jnp.min(jnp.where(l == m, lanef, float(LANES)), axis=-1, keepdims=True) for l, m in zip(logits, m1)]
    oh1 = [lanef == i for i in i1]
    rest = [jnp.where(o, -jnp.inf, l) for o, l in zip(oh1, logits)]
    m2 = [jnp.max(l, axis=-1, keepdims=True) for l in rest]
    i2 = [jnp.min(jnp.where(l == m, lanef, float(LANES)), axis=-1, keepdims=True) for l, m in zip(rest, m2)]
    oh2 = [lanef == i for i in i2]
    e = [jnp.exp(b - a) for a, b in zip(m1, m2)]
    w1 = [1.0 / (1.0 + v) for v in e]
    w2 = [v * w for v, w in zip(e, w1)]

    chosen = [jnp.where(a | b, 1.0, 0.0) for a, b in zip(oh1, oh2)]
    r = lax.broadcasted_iota(I32, (half, half), 0)
    c = lax.broadcasted_iota(I32, (half, half), 1)
    before = jnp.where(c < r, 1.0, 0.0).astype(BF16)
    within = [jnp.dot(before, v.astype(BF16), preferred_element_type=F32) for v in chosen]
    counts = [jnp.sum(v, axis=0, keepdims=True) for v in chosen]
    base = [carry_ref[0:1, :], carry_ref[0:1, :] + counts[0]]
    total = base[1] + counts[1]
    carry_ref[...] = jnp.broadcast_to(total, carry_ref.shape)
    cnt_ref[...] = jnp.broadcast_to(total, cnt_ref.shape)
    for n, rs in enumerate(halves):
        excl = within[n] + base[n]
        rank1 = jnp.sum(jnp.where(oh1[n], excl, 0.0), axis=-1, keepdims=True)
        rank2 = jnp.sum(jnp.where(oh2[n], excl, 0.0), axis=-1, keepdims=True)
        info = jnp.where(lane == 0, i1[n], 0.0)
        info = jnp.where(lane == 1, i2[n], info)
        info = jnp.where(lane == 2, rank1, info)
        info = jnp.where(lane == 3, rank2, info)
        info = jnp.where(lane == 4, w1[n], info)
        info = jnp.where(lane == 5, w2[n], info)
        info_ref[rs, :] = info


def _proj_router(h2d, x_bf, w_out, norm_gain, w_hi, w_lo, tm):
    T = h2d.shape[0]
    const = lambda i: (0, 0)
    row = lambda i: (i, 0)
    full = pl.BlockSpec((tm, D_MODEL), row)
    kern = functools.partial(_proj_router_kernel, tm=tm)
    return pl.pallas_call(
        kern,
        grid=(T // tm,),
        in_specs=[full, full, pl.BlockSpec((D_MODEL, D_MODEL), const), pl.BlockSpec((1, D_MODEL), const),
                  pl.BlockSpec((D_MODEL, LANES), const), pl.BlockSpec((D_MODEL, LANES), const)],
        out_specs=[full, full, pl.BlockSpec((tm, LANES), row), pl.BlockSpec((8, LANES), const)],
        out_shape=[
            jax.ShapeDtypeStruct((T, D_MODEL), F32),
            jax.ShapeDtypeStruct((T, D_MODEL), F32),
            jax.ShapeDtypeStruct((T, LANES), F32),
            jax.ShapeDtypeStruct((8, LANES), F32),
        ],
        scratch_shapes=[pltpu.VMEM((8, LANES), F32)],
        compiler_params=_cparams("arbitrary"),
        name="hgrn_outproj_moe_router",
    )(h2d, x_bf, w_out, norm_gain, w_hi, w_lo)


def _dispatch_kernel(zs_ref, pos_ref, prev_pos_ref, xn_ref, xs_hbm, zero_ref, rows_ref, sem, zsem, *, tt, tme):
    i = pl.program_id(0)
    slot = i % 2

    @pl.when(i == 0)
    def _():
        zero_ref[...] = jnp.zeros(zero_ref.shape, F32)
        for e in range(2 * N_EXPERTS):
            row0 = pl.multiple_of(zs_ref[e], tme)
            fill = pltpu.make_async_copy(zero_ref, xs_hbm.at[pl.ds(row0, tme)], zsem)
            fill.start()
            fill.wait()

    def copy(pref, s, t, c):
        return pltpu.make_async_copy(
            rows_ref.at[s, pl.ds(t, 1)], xs_hbm.at[pl.ds(pref[0, 0, 2 * t + c], 1)], sem.at[s])

    def start(t, carry):
        copy(pos_ref, slot, t, 0).start()
        copy(pos_ref, slot, t, 1).start()
        return carry

    def wait_for(pref, s):
        def wait(t, carry):
            copy(pref, s, t, 0).wait()
            copy(pref, s, t, 1).wait()
            return carry
        lax.fori_loop(0, tt, wait, 0, unroll=8)

    rows_ref[slot] = xn_ref[...]
    lax.fori_loop(0, tt, start, 0, unroll=8)

    @pl.when(i > 0)
    def _():
        wait_for(prev_pos_ref, 1 - slot)

    @pl.when(i == pl.num_programs(0) - 1)
    def _():
        wait_for(pos_ref, slot)


def _dispatch(xn, pos, last_tile_start, n_rows, tt, tme):
    T = xn.shape[0]
    pos3 = pos.reshape(T // tt, 1, 2 * tt)
    kern = functools.partial(_dispatch_kernel, tt=tt, tme=tme)
    grid_spec = pltpu.PrefetchScalarGridSpec(
        num_scalar_prefetch=1,
        grid=(T // tt,),
        in_specs=[
            pl.BlockSpec((1, 1, 2 * tt), lambda i, zs: (i, 0, 0), memory_space=pltpu.SMEM),
            pl.BlockSpec((1, 1, 2 * tt), lambda i, zs: (jnp.maximum(i - 1, 0), 0, 0), memory_space=pltpu.SMEM),
            pl.BlockSpec((tt, D_MODEL), lambda i, zs: (i, 0)),
        ],
        out_specs=pl.BlockSpec(memory_space=pl.ANY),
        scratch_shapes=[pltpu.VMEM((tme, D_MODEL), F32), pltpu.VMEM((2, tt, D_MODEL), F32),
                        pltpu.SemaphoreType.DMA((2,)), pltpu.SemaphoreType.DMA(())],
    )
    return pl.pallas_call(
        kern,
        grid_spec=grid_spec,
        out_shape=jax.ShapeDtypeStruct((n_rows, D_MODEL), F32),
        compiler_params=_cparams("arbitrary"),
        name="moe_dispatch",
    )(last_tile_start, pos3, pos3, xn)


def _expert_kernel(te_ref, na_ref, xs_ref, wg_ref, wu_ref, wd_ref, y_ref, *, nchunk):
    del te_ref
    active = pl.program_id(0) < na_ref[0]

    @pl.when(jnp.logical_not(active))
    def _():
        y_ref[...] = jnp.zeros(y_ref.shape, F32)

    @pl.when(active)
    def _():
        xb = xs_ref[...].astype(BF16)
        tf = D_FF_EXPERT // nchunk
        acc = None
        for c in range(nchunk):
            cols = slice(c * tf, (c + 1) * tf)
            g = jnp.dot(xb, wg_ref[0, :, cols], preferred_element_type=F32)
            u = jnp.dot(xb, wu_ref[0, :, cols], preferred_element_type=F32)
            a = (_silu(g) * u).astype(BF16)
            d = jnp.dot(a, wd_ref[0, cols, :], preferred_element_type=F32)
            acc = d if acc is None else acc + d
        y_ref[...] = acc


def _experts(xs, tile_expert, n_active, wg, wu, wd, tme, nchunk):
    n_rows = xs.shape[0]
    n_tiles = n_rows // tme

    def tile(i, na):
        return jnp.minimum(i, na[0] - 1)

    resident = pl.Buffered(1)
    grid_spec = pltpu.PrefetchScalarGridSpec(
        num_scalar_prefetch=2,
        grid=(n_tiles,),
        in_specs=[
            pl.BlockSpec((tme, D_MODEL), lambda i, te, na: (tile(i, na), 0)),
            pl.BlockSpec((1, D_MODEL, D_FF_EXPERT), lambda i, te, na: (te[tile(i, na)], 0, 0),
                         pipeline_mode=resident),
            pl.BlockSpec((1, D_MODEL, D_FF_EXPERT), lambda i, te, na: (te[tile(i, na)], 0, 0)),
            pl.BlockSpec((1, D_FF_EXPERT, D_MODEL), lambda i, te, na: (te[tile(i, na)], 0, 0)),
        ],
        out_specs=pl.BlockSpec((tme, D_MODEL), lambda i, te, na: (i, 0)),
    )
    return pl.pallas_call(
        functools.partial(_expert_kernel, nchunk=nchunk),
        grid_spec=grid_spec,
        out_shape=jax.ShapeDtypeStruct((n_rows, D_MODEL), F32),
        compiler_params=_cparams("arbitrary"),
        name="moe_experts",
    )(tile_expert, n_active, xs, wg, wu, wd)


def _combine_kernel(pos_ref, next_pos_ref, info_ref, h_ref, y_hbm, o_ref, buf_ref, sem, *, tc):
    i = pl.program_id(0)
    slot = i % 2

    def copy(pref, s, t, c):
        return pltpu.make_async_copy(
            y_hbm.at[pl.ds(pref[0, 0, 2 * t + c], 1)], buf_ref.at[s, c, pl.ds(t, 1)], sem.at[s])

    def gather(pref, s):
        def start(t, carry):
            copy(pref, s, t, 0).start()
            copy(pref, s, t, 1).start()
            return carry
        lax.fori_loop(0, tc, start, 0, unroll=8)

    def wait(t, carry):
        copy(pos_ref, slot, t, 0).wait()
        copy(pos_ref, slot, t, 1).wait()
        return carry

    @pl.when(i == 0)
    def _():
        gather(pos_ref, slot)

    @pl.when(i + 1 < pl.num_programs(0))
    def _():
        gather(next_pos_ref, 1 - slot)

    lax.fori_loop(0, tc, wait, 0, unroll=8)
    info = info_ref[...]
    lane = lax.broadcasted_iota(I32, (tc, LANES), 1)
    w1 = jnp.sum(jnp.where(lane == 4, info, 0.0), axis=-1, keepdims=True)
    w2 = jnp.sum(jnp.where(lane == 5, info, 0.0), axis=-1, keepdims=True)
    o_ref[...] = h_ref[...] + (w1 * buf_ref[slot, 0] + w2 * buf_ref[slot, 1])


def _combine(h2d, info, pos, y, tc):
    T = h2d.shape[0]
    n = T // tc
    pos3 = pos.reshape(n, 1, 2 * tc)
    kern = functools.partial(_combine_kernel, tc=tc)
    return pl.pallas_call(
        kern,
        grid=(n,),
        in_specs=[
            pl.BlockSpec((1, 1, 2 * tc), lambda i: (i, 0, 0), memory_space=pltpu.SMEM),
            pl.BlockSpec((1, 1, 2 * tc), lambda i: (jnp.minimum(i + 1, n - 1), 0, 0), memory_space=pltpu.SMEM),
            pl.BlockSpec((tc, LANES), lambda i: (i, 0)),
            pl.BlockSpec((tc, D_MODEL), lambda i: (i, 0)),
            pl.BlockSpec(memory_space=pl.ANY),
        ],
        out_specs=pl.BlockSpec((tc, D_MODEL), lambda i: (i, 0)),
        out_shape=jax.ShapeDtypeStruct((T, D_MODEL), F32),
        scratch_shapes=[pltpu.VMEM((2, 2, tc, D_MODEL), F32), pltpu.SemaphoreType.DMA((2,))],
        compiler_params=_cparams("arbitrary"),
        name="moe_combine",
    )(pos3, pos3, info, h2d, y)


def _bf16_pieces(x, n):
    pieces, r = [], x.astype(F32)
    for _ in range(n):
        p = r.astype(BF16)
        pieces.append(p)
        r = r - p.astype(F32)
    return pieces


def _alibi_slopes():
    return [2.0 ** (-8.0 * (h + 1.0) / N_ATTN_HEADS) for h in range(N_ATTN_HEADS)]


def _alibi_features(slopes, S):
    def pieces(x):
        hi = x.astype(BF16)
        return hi.astype(np.float32), (x - hi.astype(np.float32)).astype(BF16).astype(np.float32)

    s_hi, s_lo = pieces(np.asarray(slopes, np.float32) * np.float32(LOG2E))
    slope_feat = np.zeros((8, LANES), np.float32)
    slope_feat[:len(slopes), 0], slope_feat[:len(slopes), 1] = s_hi, s_hi
    slope_feat[:len(slopes), 2], slope_feat[:len(slopes), 3] = s_lo, s_lo
    p_hi, p_lo = pieces(np.arange(S, dtype=np.float32))
    pos_feat = np.zeros((S, LANES), np.float32)
    pos_feat[:, 0], pos_feat[:, 1], pos_feat[:, 2], pos_feat[:, 3] = p_hi, p_lo, p_hi, p_lo
    return jnp.asarray(slope_feat), jnp.asarray(pos_feat.astype(BF16))


def _attn_col_gain(dil_q_gain, dil_k_gain, diff_q_gain, diff_k_gain):
    rep = lambda g, n: jnp.tile(g.astype(F32), n)
    ones = jnp.ones((COLB,), F32)
    dq = rep(dil_q_gain, COLB // HEAD_DIM) * (HEAD_DIM ** -0.5 * LOG2E)
    dk = rep(dil_k_gain, COLB // HEAD_DIM)
    fq = rep(diff_q_gain, COLB // DIFF_QK_DIM) * (DIFF_QK_DIM ** -0.5 * LOG2E)
    fk = rep(diff_k_gain, COLB // DIFF_QK_DIM)
    return jnp.stack([dq, dq, dq, dk, dk, dk, ones, ones, ones, fq, fk, ones])


def _lower_bound_rows(lb_logits, layer):
    sm = jax.nn.softmax(lb_logits.astype(F32), axis=0)
    lb = (jnp.cumsum(sm, axis=0) - sm[0])[layer]
    return jnp.zeros((8, D_MODEL), F32).at[0].set(jnp.log(lb)).at[1].set(jnp.log1p(-lb)).at[2].set(1.0 - lb)


def _layer0_and_hgrn_inproj(h2d, B, S, attn_norm, w_in, dq_g, dk_g, fq_g, fk_g, lq1, lk1, lq2, lk2, out_g,
                            w_out, ffn_norm, w_gate, w_up, w_down, hgrn_norm, hgrn_w_in, lb_logits):
    T = B * S
    tm = min(TM_INPROJ, T)
    slopes = _alibi_slopes()
    *cls, dproj = _attn_inproj(h2d, attn_norm.reshape(1, D_MODEL).astype(F32), w_in.astype(BF16),
                               _attn_col_gain(dq_g, dk_g, fq_g, fk_g), tm)
    outs, lses = [], []
    for g, (window, dilation) in enumerate(DIL_GROUPS):
        assert window // dilation == QBLOCK and S % (dilation * QBLOCK) == 0 and tm % dilation == 0
        o, lse = _dilated_group(cls[g], B, S, dilation, tuple(slopes[4 * g:4 * g + 4]))
        outs.extend(o)
        lses.extend(lse)

    lam_init = 0.8 - 0.6 * math.exp(-0.3 * 0)
    lam = (jnp.exp(jnp.sum(lq1.astype(F32) * lk1.astype(F32)))
           - jnp.exp(jnp.sum(lq2.astype(F32) * lk2.astype(F32))) + lam_init).reshape(1)
    slope_feat, pos_feat = _alibi_features(slopes[N_DIL_HEADS:], S)
    out_gain = (jnp.tile(out_g.astype(F32), COLB // HEAD_DIM) * (1.0 - lam_init)).reshape(1, COLB)
    diff = _diff_attention(dproj.reshape(B, S, _QKV), lam, slope_feat, pos_feat, out_gain, min(TQ_DIFF, S)).reshape(T, COLB)

    h2, *gates = _layer0_tail(
        h2d, outs, lses, diff, w_out.astype(BF16), ffn_norm.reshape(1, D_MODEL).astype(F32),
        w_gate.astype(BF16), w_up.astype(BF16), w_down.astype(BF16),
        hgrn_norm.reshape(1, D_MODEL).astype(F32), hgrn_w_in.astype(BF16),
        _lower_bound_rows(lb_logits, 1), min(HGRN_ROWS, S))
    return h2, gates


def _layer1_rest(h2d, gates, B, S, out_gain, w_out, moe_norm, w_router, w_gate, w_up, w_down):
    o = _hgrn_recurrence(gates, out_gain.reshape(1, LANES).astype(F32), B, S, min(HGRN_ROWS, S))
    return _moe_block(h2d, o, w_out, moe_norm, w_router, w_gate, w_up, w_down)


def _moe_block(h2d, mixer_out, w_out, moe_norm, w_router, w_gate, w_up, w_down):
    T = h2d.shape[0]
    tme = TME
    w_pad = jnp.zeros((D_MODEL, LANES), F32).at[:, :N_EXPERTS].set(w_router.astype(F32))
    w_hi, w_lo = _bf16_pieces(w_pad, 2)
    h2d, xn, info, counts = _proj_router(h2d, mixer_out, w_out.astype(BF16),
                                         moe_norm.reshape(1, D_MODEL).astype(F32), w_hi, w_lo, min(TM_ROUTER, T))
    cnt = counts[0, :N_EXPERTS].astype(I32)
    padded = ((cnt + tme - 1) // tme) * tme
    ends = jnp.cumsum(padded)
    starts = ends - padded
    experts = info[:, 0:2].astype(I32)
    ranks = info[:, 2:4].astype(I32)
    pos = (starts[experts] + ranks).reshape(-1)
    n_rows = 2 * T + N_EXPERTS * tme
    n_tiles = n_rows // tme
    tile_start = jnp.arange(n_tiles, dtype=I32) * tme
    tile_expert = jnp.minimum(jnp.sum((ends[None, :] <= tile_start[:, None]).astype(I32), axis=1), N_EXPERTS - 1)
    n_active = (ends[-1] // tme).astype(I32).reshape(1)
    tail = jnp.minimum(ends[-1] + jnp.arange(N_EXPERTS, dtype=I32) * tme, n_rows - tme)
    zero_tiles = jnp.concatenate([jnp.maximum(ends - tme, 0), tail]).astype(I32)
    xs = _dispatch(xn, pos, zero_tiles, n_rows, min(TOKENS_PER_DMA_STEP, T), tme)
    y = _experts(xs, tile_expert, n_active, w_gate.astype(BF16), w_up.astype(BF16), w_down.astype(BF16),
                 tme, EXPERT_FF_CHUNKS)
    return _combine(h2d, info, pos, y, min(TOKENS_PER_DMA_STEP, T))


def kernel(x, attn_norm, attn_w_in, dil_q_gain, dil_k_gain, diff_q_gain, diff_k_gain, diff_lambda_q1, diff_lambda_k1, diff_lambda_q2, diff_lambda_k2, diff_out_gain, attn_w_out, ffn_norm, ffn_w_gate, ffn_w_up, ffn_w_down, hgrn_norm, hgrn_w_in, hgrn_lb_logits, hgrn_out_gain, hgrn_w_out, moe_norm, moe_w_router, moe_w_gate, moe_w_up, moe_w_down):
    B, S, D = x.shape
    assert D == D_MODEL
    h = x.astype(F32).reshape(B * S, D)
    h, gates = _layer0_and_hgrn_inproj(
        h, B, S, attn_norm[0], attn_w_in[0], dil_q_gain[0], dil_k_gain[0], diff_q_gain[0], diff_k_gain[0],
        diff_lambda_q1[0], diff_lambda_k1[0], diff_lambda_q2[0], diff_lambda_k2[0], diff_out_gain[0],
        attn_w_out[0], ffn_norm[0], ffn_w_gate[0], ffn_w_up[0], ffn_w_down[0],
        hgrn_norm[0], hgrn_w_in[0], hgrn_lb_logits)
    h = _layer1_rest(h, gates, B, S, hgrn_out_gain[0], hgrn_w_out[0], moe_norm[0], moe_w_router[0],
                     moe_w_gate[0], moe_w_up[0], moe_w_down[0])
    return h.reshape(B, S, D).astype(x.dtype)
```

```python
import functools
import math

import numpy as np
import jax
import jax.numpy as jnp
from jax import lax
from jax.experimental import pallas as pl
from jax.experimental.pallas import tpu as pltpu

F32 = jnp.float32
BF16 = jnp.bfloat16
I32 = jnp.int32

D_MODEL = 1024
HEAD_DIM = 64
N_ATTN_HEADS = 16
N_DIL_HEADS = 12
N_DIFF_HEADS = 4
DIL_GROUPS = ((128, 1), (512, 4), (2048, 16))
DIFF_QK_DIM = 32
ATTN_IN_WIDTH = 3072
QBLOCK = 128
HGRN_HEADS = 8
HGRN_CHUNK = 64
DEC_ROWS = 8
D_FF = 2816
N_EXPERTS = 8
D_FF_EXPERT = 3584
EPS = 1e-6
NEG_INF = -1e30
LOG2E = 1.4426950408889634
LANES = 128
COLB = 256
VMEM_LIMIT = 56 * 1024 * 1024

TM_INPROJ = 512
UB_DILATED = 512
TQ_DIFF = 512
HGRN_ROWS = 256
TM_ROUTER = 512
TOKENS_PER_DMA_STEP = 256
TME = 512
EXPERT_FF_CHUNKS = 2


def _cparams(*sem):
    return pltpu.CompilerParams(dimension_semantics=sem, vmem_limit_bytes=VMEM_LIMIT)


def _bf16_hi_lo(x):
    bits = lax.bitcast_convert_type(x, jnp.uint32)
    hi = lax.bitcast_convert_type(bits & jnp.uint32(0xFFFF0000), F32)
    return hi.astype(BF16), (x - hi).astype(BF16)


def _split_dot(x, m):
    hi, lo = _bf16_hi_lo(x)
    return jnp.dot(hi, m, preferred_element_type=F32) + jnp.dot(lo, m, preferred_element_type=F32)


def _seg_matrix(n, seg):
    i = np.arange(n)
    return jnp.asarray((i[:, None] // seg == i[None, :] // seg).astype(BF16))


def _rms_rows(x, gain_row):
    return x * lax.rsqrt(jnp.mean(x * x, axis=-1, keepdims=True) + EPS) * gain_row


def _sigmoid(x):
    return 1.0 / (1.0 + jnp.exp(-x))


def _silu(x):
    return x * _sigmoid(x)


_ATTN_NORM_SEG = (64, 64, 64, 64, 64, 64, 0, 0, 0, 32, 32, 0)
_QKV = 3 * COLB


def _attn_inproj_kernel(h_ref, ng_ref, w_ref, cg_ref, s64_ref, s32_ref,
                        c0_ref, c1_ref, c2_ref, df_ref, y_ref, *, tm):
    cls_refs = (c0_ref, c1_ref, c2_ref)
    xb = _rms_rows(h_ref[...], ng_ref[...]).astype(BF16)
    blocks = range(len(_ATTN_NORM_SEG))
    ys = [jnp.dot(xb, w_ref[:, c * COLB:(c + 1) * COLB], preferred_element_type=F32) for c in blocks]
    ms = {c: _split_dot(ys[c] * ys[c], s64_ref[...] if seg == 64 else s32_ref[...]) * (1.0 / seg)
          for c, seg in enumerate(_ATTN_NORM_SEG) if seg}
    ys = [ys[c] * lax.rsqrt(ms[c] + EPS) * cg_ref[c:c + 1, :] if c in ms else ys[c] for c in blocks]
    strided = []
    for c in blocks:
        part, g = divmod(c, 3)
        if part == 3:
            df_ref[:, g * COLB:(g + 1) * COLB] = ys[c].astype(BF16)
        elif DIL_GROUPS[g][1] == 1:
            cls_refs[g][:, part * COLB:(part + 1) * COLB] = ys[c].astype(BF16)
        else:
            slot = len(strided)
            y_ref[slot, 0] = ys[c][:, :LANES]
            y_ref[slot, 1] = ys[c][:, LANES:]
            strided.append((slot, part, g))
    for slot, part, g in strided:
        d = DIL_GROUPS[g][1]
        for r in range(d):
            col = r * _QKV + part * COLB
            rows = pl.ds(r, tm // d, stride=d)
            cls_refs[g][:, col:col + COLB] = jnp.concatenate(
                [y_ref[slot, 0, rows, :], y_ref[slot, 1, rows, :]], axis=1).astype(BF16)


def _attn_inproj(h2d, norm_gain, w_bf, col_gain, tm):
    T = h2d.shape[0]
    const = lambda i: (0, 0)
    row = lambda i: (i, 0)
    dils = [d for _, d in DIL_GROUPS]
    kern = functools.partial(_attn_inproj_kernel, tm=tm)
    return pl.pallas_call(
        kern,
        grid=(T // tm,),
        in_specs=[
            pl.BlockSpec((tm, D_MODEL), row),
            pl.BlockSpec((1, D_MODEL), const),
            pl.BlockSpec((D_MODEL, ATTN_IN_WIDTH), const),
            pl.BlockSpec((ATTN_IN_WIDTH // COLB, COLB), const),
            pl.BlockSpec((COLB, COLB), const),
            pl.BlockSpec((COLB, COLB), const),
        ],
        out_specs=[pl.BlockSpec((tm // d, d * _QKV), row) for d in dils] + [pl.BlockSpec((tm, _QKV), row)],
        out_shape=[jax.ShapeDtypeStruct((T // d, d * _QKV), BF16) for d in dils]
                  + [jax.ShapeDtypeStruct((T, _QKV), BF16)],
        scratch_shapes=[pltpu.VMEM((3 * sum(d > 1 for d in dils), 2, tm, LANES), F32)],
        compiler_params=_cparams("parallel"),
        name="attn_inproj",
    )(h2d, norm_gain, w_bf, col_gain, _seg_matrix(COLB, 64), _seg_matrix(COLB, 32))


def _dil_kernel(q_ref, kp_ref, kc_ref, vp_ref, vc_ref, o0_ref, o1_ref, l0_ref, l1_ref,
                *, slopes, dilation, ub):
    u = pl.program_id(1)
    r = pl.program_id(2)
    q = q_ref[...]
    kcat = jnp.concatenate([kp_ref[...], kc_ref[...]], axis=0)
    vcat = jnp.concatenate([vp_ref[...], vc_ref[...]], axis=0)
    qi = lax.broadcasted_iota(I32, (QBLOCK, 2 * QBLOCK), 0)
    kj = lax.broadcasted_iota(I32, (QBLOCK, 2 * QBLOCK), 1)
    step = qi + QBLOCK - kj
    in_window = (step >= 0) & (step <= QBLOCK)
    first_window = in_window & ((kj >= QBLOCK) | (u > 0))
    stepf = step.astype(F32)
    bias = [(-slopes[h] * dilation * LOG2E) * stepf for h in range(COLB // HEAD_DIM)]
    mask_bias = [jnp.where(in_window, b, NEG_INF) for b in bias]
    mask_bias_first = [jnp.where(first_window, b, NEG_INF) for b in bias]
    lane = lax.broadcasted_iota(I32, (QBLOCK, LANES), 1)
    low_half = lane < HEAD_DIM
    jobs = [(i, pair, hh) for i in range(ub // QBLOCK) for pair in range(2) for hh in range(2)]
    nt = (((1,), (1,)), ((), ()))

    def masked_q(i, pair, hh):
        qp = q[i * QBLOCK:(i + 1) * QBLOCK, pair * LANES:(pair + 1) * LANES]
        return jnp.where(low_half if hh == 0 else ~low_half, qp, jnp.zeros_like(qp))

    scores = [lax.dot_general(masked_q(i, pair, hh), kcat[i * QBLOCK:(i + 2) * QBLOCK, pair * LANES:(pair + 1) * LANES],
                              nt, preferred_element_type=F32) for i, pair, hh in jobs]
    stats = []
    for (i, pair, hh), s in zip(jobs, scores):
        s = s + (mask_bias_first if i == 0 else mask_bias)[pair * 2 + hh]
        m = jnp.max(s, axis=-1, keepdims=True)
        e = jnp.exp2(s - m)
        stats.append((m, jnp.sum(e, axis=-1, keepdims=True), e.astype(BF16)))
    pvs = [jnp.dot(e, vcat[i * QBLOCK:(i + 2) * QBLOCK, pair * LANES:(pair + 1) * LANES], preferred_element_type=F32)
           for (i, pair, hh), (_, _, e) in zip(jobs, stats)]
    for n in range(0, len(jobs), 2):
        i, pair, _ = jobs[n]
        outs = [pvs[n + hh] / stats[n + hh][1] for hh in range(2)]
        lses = [jnp.broadcast_to(stats[n + hh][0] + jnp.log2(stats[n + hh][1]), (QBLOCK, LANES)) for hh in range(2)]
        if dilation == 1:
            rows = slice(i * QBLOCK, (i + 1) * QBLOCK)
        else:
            rows = pl.ds(r + i * QBLOCK * dilation, QBLOCK, stride=dilation)
        (o0_ref, o1_ref)[pair][rows, :] = jnp.where(low_half, outs[0], outs[1])
        (l0_ref, l1_ref)[pair][rows, :] = jnp.where(low_half, lses[0], lses[1])


def _dilated_group(cls, B, S, dilation, slopes):
    U = S // dilation
    ub = min(UB_DILATED, U)
    sub = ub // QBLOCK
    view = cls.reshape(B, U, dilation * _QKV)
    prev = lambda u: jnp.maximum(u * sub - 1, 0)
    kern = functools.partial(_dil_kernel, slopes=slopes, dilation=dilation, ub=ub)
    out_spec = pl.BlockSpec((None, ub * dilation, LANES), lambda b, u, r: (b, u, 0))
    res = pl.pallas_call(
        kern,
        grid=(B, U // ub, dilation),
        in_specs=[
            pl.BlockSpec((None, ub, COLB), lambda b, u, r: (b, u, 3 * r)),
            pl.BlockSpec((None, QBLOCK, COLB), lambda b, u, r: (b, prev(u), 3 * r + 1)),
            pl.BlockSpec((None, ub, COLB), lambda b, u, r: (b, u, 3 * r + 1)),
            pl.BlockSpec((None, QBLOCK, COLB), lambda b, u, r: (b, prev(u), 3 * r + 2)),
            pl.BlockSpec((None, ub, COLB), lambda b, u, r: (b, u, 3 * r + 2)),
        ],
        out_specs=[out_spec] * 4,
        out_shape=[jax.ShapeDtypeStruct((B, S, LANES), F32)] * 4,
        compiler_params=_cparams("parallel", "parallel", "arbitrary"),
        name=f"dilated_attn_d{dilation}",
    )(view, view, view, view, view)
    res = [a.reshape(B * S, LANES) for a in res]
    return res[:2], res[2:]


def _diff_kernel(lam_ref, q_ref, k_ref, v_ref, pos_ref, sl_ref, og_ref, s64_ref, o_ref,
                 m_ref, l_ref, acc_ref, qa_ref, *, tq):
    i = pl.program_id(1)
    q = q_ref[...]
    lane = lax.broadcasted_iota(I32, (tq, LANES), 1)
    low_half = lane < HEAD_DIM
    for pair in range(2):
        qp = q[:, pair * LANES:(pair + 1) * LANES]
        for hh in range(2):
            feat = jnp.broadcast_to(sl_ref[pair * 2 + hh:pair * 2 + hh + 1, :], (tq, LANES)).astype(BF16)
            for mu in range(2):
                lo = hh * HEAD_DIM + mu * DIFF_QK_DIM
                sel = (lane >= lo) & (lane < lo + DIFF_QK_DIM)
                r = hh * 2 + mu
                qa_ref[pair, r * tq:(r + 1) * tq, :] = jnp.concatenate(
                    [jnp.where(sel, qp, jnp.zeros_like(qp)), feat], axis=1)

    m_ref[...] = jnp.full(m_ref.shape, NEG_INF, F32)
    l_ref[...] = jnp.zeros(l_ref.shape, F32)
    acc_ref[...] = jnp.zeros(acc_ref.shape, F32)

    def scores(j, nk):
        ks = pl.multiple_of(j * tq, tq)
        kblk = k_ref[pl.ds(ks, nk * tq), :]
        pblk = pos_ref[pl.ds(ks, nk * tq), :]
        s_alls = []
        for pair in range(2):
            kaug = jnp.concatenate([kblk[:, pair * LANES:(pair + 1) * LANES], pblk], axis=1)
            s_alls.append(lax.dot_general(qa_ref[pair], kaug, (((1,), (1,)), ((), ())),
                                          preferred_element_type=F32))
        return s_alls, v_ref[pl.ds(ks, nk * tq), :]

    def finish(s_alls, vblk, masked):
        tk = vblk.shape[0]
        ones = jnp.ones((tk, LANES), BF16)
        if masked:
            row = lax.broadcasted_iota(I32, (tq, tk), 0)
            col = lax.broadcasted_iota(I32, (tq, tk), 1)
            causal = col <= row
        ss = [s_alls[idx // 4][(idx % 4) * tq:(idx % 4 + 1) * tq] for idx in range(8)]
        if masked:
            ss = [jnp.where(causal, s, NEG_INF) for s in ss]
        m_prev = [m_ref[idx] for idx in range(8)]
        m_next = [jnp.maximum(m, jnp.max(s, axis=-1, keepdims=True)) for m, s in zip(m_prev, ss)]
        for idx in range(8):
            m_ref[idx] = m_next[idx]
        alphas = [jnp.exp2(a - b) for a, b in zip(m_prev, m_next)]
        ps = [jnp.exp2(s - jnp.concatenate([m] * (tk // LANES), axis=1)).astype(BF16) for s, m in zip(ss, m_next)]
        pvs = []
        for pair in range(2):
            vaug = jnp.concatenate([vblk[:, pair * LANES:(pair + 1) * LANES], ones], axis=1)
            pvs.append(jnp.dot(jnp.concatenate(ps[pair * 4:pair * 4 + 4], axis=0), vaug,
                               preferred_element_type=F32))
        for idx in range(8):
            part = pvs[idx // 4][(idx % 4) * tq:(idx % 4 + 1) * tq]
            acc_ref[idx] = alphas[idx] * acc_ref[idx] + part[:, :LANES]
            l_ref[idx] = alphas[idx] * l_ref[idx] + part[:, LANES:]

    def two_full_blocks(jj, carry):
        finish(*scores(2 * jj, 2), False)
        return carry

    lax.fori_loop(0, i // 2, two_full_blocks, 0)

    @pl.when(i % 2 == 1)
    def _():
        finish(*scores(i - 1, 1), False)

    finish(*scores(i, 1), True)

    lam = lam_ref[0]
    for pair in range(2):
        o = None
        for mu in range(2):
            lo_idx, hi_idx = pair * 4 + mu, pair * 4 + 2 + mu
            term = jnp.where(low_half, acc_ref[lo_idx] / l_ref[lo_idx], acc_ref[hi_idx] / l_ref[hi_idx])
            o = term if mu == 0 else o - lam * term
        ms = _split_dot(o * o, s64_ref[...]) * (1.0 / HEAD_DIM)
        o = o * lax.rsqrt(ms + EPS) * og_ref[:, pair * LANES:(pair + 1) * LANES]
        o_ref[:, pair * LANES:(pair + 1) * LANES] = o.astype(BF16)


def _diff_attention(proj, lam, slope_feat, pos_feat, out_gain, tq):
    B, S, W = proj.shape
    kern = functools.partial(_diff_kernel, tq=tq)
    return pl.pallas_call(
        kern,
        grid=(B, S // tq),
        in_specs=[
            pl.BlockSpec(memory_space=pltpu.SMEM),
            pl.BlockSpec((None, tq, COLB), lambda b, i: (b, i, 0)),
            pl.BlockSpec((None, S, COLB), lambda b, i: (b, 0, 1), pipeline_mode=pl.Buffered(1)),
            pl.BlockSpec((None, S, COLB), lambda b, i: (b, 0, 2), pipeline_mode=pl.Buffered(1)),
            pl.BlockSpec((S, LANES), lambda b, i: (0, 0), pipeline_mode=pl.Buffered(1)),
            pl.BlockSpec((8, LANES), lambda b, i: (0, 0)),
            pl.BlockSpec((1, COLB), lambda b, i: (0, 0)),
            pl.BlockSpec((LANES, LANES), lambda b, i: (0, 0)),
        ],
        out_specs=pl.BlockSpec((None, tq, COLB), lambda b, i: (b, i, 0)),
        out_shape=jax.ShapeDtypeStruct((B, S, COLB), BF16),
        scratch_shapes=[
            pltpu.VMEM((8, tq, LANES), F32),
            pltpu.VMEM((8, tq, LANES), F32),
            pltpu.VMEM((8, tq, LANES), F32),
            pltpu.VMEM((2, 4 * tq, 2 * LANES), BF16),
        ],
        compiler_params=_cparams("parallel", "parallel"),
        name="diff_attn",
    )(lam, proj, proj, proj, pos_feat, slope_feat, out_gain, _seg_matrix(LANES, HEAD_DIM))


def _merge_heads(o_refs, l_refs, d_ref):
    n = len(o_refs)
    pieces = [None] * n
    for pair in range(2):
        ls = [l_refs[2 * g + pair][...] for g in range(n // 2)]
        mx = functools.reduce(jnp.maximum, ls)
        es = [jnp.exp2(l - mx) for l in ls]
        inv = 1.0 / functools.reduce(jnp.add, es)
        for g, e in enumerate(es):
            pieces[2 * g + pair] = (e * inv * o_refs[2 * g + pair][...]).astype(BF16)
    return jnp.concatenate(pieces + [d_ref[...]], axis=1)


def _split_dot_lhs(m, x):
    hi, lo = _bf16_hi_lo(x)
    return jnp.dot(m, hi, preferred_element_type=F32) + jnp.dot(m, lo, preferred_element_type=F32)


def _chunk_mask(rows):
    ti = lax.broadcasted_iota(I32, (rows, rows), 0)
    si = lax.broadcasted_iota(I32, (rows, rows), 1)
    return (si <= ti) & (si >= (ti // HGRN_CHUNK) * HGRN_CHUNK)


def _hgrn_gates(xb, w_ref, lb_ref, qs_ref, ks_ref, qi_ref, kst_ref, v_ref, g_ref, dec_ref):
    C = HGRN_CHUNK
    tm = xb.shape[0]
    nc = tm // C
    yq, fl, yv, yg = [jnp.dot(xb, w_ref[:, c * D_MODEL:(c + 1) * D_MODEL], preferred_element_type=F32)
                      for c in range(4)]
    v_ref[...] = yv.astype(BF16)
    g_ref[...] = _sigmoid(yg).astype(BF16)
    q = _silu(yq)
    log_lb, log1m_lb, one_m_lb = lb_ref[0:1, :], lb_ref[1:2, :], lb_ref[2:3, :]
    t = jnp.exp(-jnp.abs(fl))
    r = 1.0 / (1.0 + t)
    c = log1m_lb + (jnp.minimum(fl, 0.0) + jnp.log(r))
    lf = jnp.maximum(log_lb, c) + jnp.log(1.0 + jnp.exp(-jnp.abs(log_lb - c)))
    k = one_m_lb * jnp.where(fl >= 0.0, t * r, r)
    tri = jnp.where(_chunk_mask(tm), 1.0, 0.0).astype(BF16)
    b = _split_dot_lhs(tri, lf)
    chunk_row = lambda i: jnp.concatenate(
        [jnp.broadcast_to(b[n * C + i:n * C + i + 1, :], (C, D_MODEL)) for n in range(nc)], axis=0)
    b_mid = chunk_row(C // 2)
    b_last = chunk_row(C - 1)
    qs_ref[...] = (q * jnp.exp(b - b_mid)).astype(BF16)
    ks_ref[...] = (k * jnp.exp(b_mid - b)).astype(BF16)
    qi_ref[...] = (q * jnp.exp(b)).astype(BF16)
    kst_ref[...] = (k * jnp.exp(b_last - b)).astype(BF16)
    last_rows = jnp.concatenate([b[n * C + C - 1:n * C + C, :] for n in range(nc)], axis=0)
    dec_ref[...] = jnp.concatenate(
        [jnp.exp(last_rows), jnp.zeros((dec_ref.shape[0] - nc, D_MODEL), F32)], axis=0)


def _layer0_tail_kernel(h_ref, *refs):
    n = 2 * len(DIL_GROUPS)
    o_refs, l_refs = refs[:n], refs[n:2 * n]
    (d_ref, wo_ref, fng_ref, wg_ref, wu_ref, wd_ref, hng_ref, wh_ref, lb_ref,
     h2_ref, qs_ref, ks_ref, qi_ref, kst_ref, v_ref, g_ref, dec_ref) = refs[2 * n:]
    mixed = _merge_heads(o_refs, l_refs, d_ref)
    h1 = h_ref[...] + jnp.dot(mixed, wo_ref[...], preferred_element_type=F32)
    xb = _rms_rows(h1, fng_ref[...]).astype(BF16)
    g = jnp.dot(xb, wg_ref[...], preferred_element_type=F32)
    u = jnp.dot(xb, wu_ref[...], preferred_element_type=F32)
    a = (_silu(g) * u).astype(BF16)
    h2 = h1 + jnp.dot(a, wd_ref[...], preferred_element_type=F32)
    h2_ref[...] = h2
    _hgrn_gates(_rms_rows(h2, hng_ref[...]).astype(BF16), wh_ref, lb_ref,
                qs_ref, ks_ref, qi_ref, kst_ref, v_ref, g_ref, dec_ref)


def _layer0_tail(h2d, outs, lses, diff, w_out, ffn_gain, wg, wu, wd, hgrn_gain, w_in, lb_rows, tm):
    T = h2d.shape[0]
    row = lambda i: (i, 0)
    full = pl.BlockSpec((tm, D_MODEL), row)
    half = pl.BlockSpec((tm, LANES), row)
    held = lambda shape: pl.BlockSpec(shape, lambda i: (0, 0), pipeline_mode=pl.Buffered(1))
    bf = jax.ShapeDtypeStruct((T, D_MODEL), BF16)
    f32 = jax.ShapeDtypeStruct((T, D_MODEL), F32)
    return pl.pallas_call(
        _layer0_tail_kernel,
        grid=(T // tm,),
        in_specs=[full] + [half] * (len(outs) + len(lses))
                 + [pl.BlockSpec((tm, COLB), row), held((D_MODEL, D_MODEL)), held((1, D_MODEL)),
                    held((D_MODEL, D_FF)), held((D_MODEL, D_FF)), held((D_FF, D_MODEL)),
                    held((1, D_MODEL)), held((D_MODEL, 4 * D_MODEL)), held((8, D_MODEL))],
        out_specs=[full] * 7 + [pl.BlockSpec((DEC_ROWS, D_MODEL), row)],
        out_shape=[f32] + [bf] * 6 + [jax.ShapeDtypeStruct((T // tm * DEC_ROWS, D_MODEL), F32)],
        compiler_params=_cparams("parallel"),
        name="layer0_tail_hgrn_inproj",
    )(h2d, *outs, *lses, diff, w_out, ffn_gain, wg, wu, wd, hgrn_gain, w_in, lb_rows)


def _hgrn_kernel(qs_ref, ks_ref, qi_ref, kst_ref, v_ref, g_ref, dec_ref, og_ref, o_ref, st_ref, *, rows):
    @pl.when(pl.program_id(1) == 0)
    def _():
        st_ref[...] = jnp.zeros(st_ref.shape, F32)

    C = HGRN_CHUNK
    nc = rows // C
    heads = range(HGRN_HEADS)
    lanes = [slice(hd * LANES, (hd + 1) * LANES) for hd in heads]
    causal = _chunk_mask(rows)
    nt = (((1,), (1,)), ((), ()))
    tn = (((0,), (0,)), ((), ()))
    scores = [lax.dot_general(qs_ref[:, lanes[hd]], ks_ref[:, lanes[hd]], nt, preferred_element_type=F32)
              for hd in heads]
    incs = [[lax.dot_general(v_ref[c * C:(c + 1) * C, lanes[hd]], kst_ref[c * C:(c + 1) * C, lanes[hd]], tn,
                             preferred_element_type=F32) for c in range(nc)] for hd in heads]
    probs = [jnp.where(causal, s, 0.0).astype(BF16) for s in scores]
    outs = [jnp.dot(probs[hd], v_ref[:, lanes[hd]], preferred_element_type=F32) for hd in heads]
    states = []
    for hd in heads:
        st, per_chunk = st_ref[hd], []
        for c in range(nc):
            per_chunk.append(st.astype(BF16))
            st = dec_ref[c:c + 1, lanes[hd]] * st + incs[hd][c]
        st_ref[hd] = st
        states.append(per_chunk)
    for hd in heads:
        inter = [lax.dot_general(qi_ref[c * C:(c + 1) * C, lanes[hd]], states[hd][c], nt,
                                 preferred_element_type=F32) for c in range(nc)]
        o = outs[hd] + jnp.concatenate(inter, axis=0)
        o = o * lax.rsqrt(jnp.mean(o * o, axis=-1, keepdims=True) + EPS) * og_ref[...]
        o_ref[:, lanes[hd]] = (o * g_ref[:, lanes[hd]].astype(F32)).astype(BF16)


def _hgrn_recurrence(gates, out_gain, B, S, rows):
    T = B * S
    nseq = S // rows
    row = pl.BlockSpec((rows, D_MODEL), lambda b, s: (b * nseq + s, 0))
    dec = pl.BlockSpec((DEC_ROWS, D_MODEL), lambda b, s: (b * nseq + s, 0))
    kern = functools.partial(_hgrn_kernel, rows=rows)
    return pl.pallas_call(
        kern,
        grid=(B, nseq),
        in_specs=[row] * 6 + [dec, pl.BlockSpec((1, LANES), lambda b, s: (0, 0))],
        out_specs=row,
        out_shape=jax.ShapeDtypeStruct((T, D_MODEL), BF16),
        scratch_shapes=[pltpu.VMEM((HGRN_HEADS, LANES, LANES), F32)],
        compiler_params=_cparams("parallel", "arbitrary"),
        name="hgrn_recurrence",
    )(*gates, out_gain)


def _proj_router_kernel(h_ref, x_ref, wo_ref, ng_ref, whi_ref, wlo_ref,
                        h2_ref, xn_ref, info_ref, cnt_ref, carry_ref, *, tm):
    @pl.when(pl.program_id(0) == 0)
    def _():
        carry_ref[...] = jnp.zeros(carry_ref.shape, F32)

    half = tm // 2
    halves = [slice(n * half, (n + 1) * half) for n in range(2)]
    h2 = [h_ref[rs, :] + jnp.dot(x_ref[rs, :], wo_ref[...], preferred_element_type=F32) for rs in halves]
    xn = [_rms_rows(v, ng_ref[...]) for v in h2]
    for rs, a, c in zip(halves, h2, xn):
        h2_ref[rs, :] = a
        xn_ref[rs, :] = c
    xhi, xlo = zip(*[_bf16_hi_lo(v) for v in xn])
    lane = lax.broadcasted_iota(I32, (half, LANES), 1)
    lanef = lane.astype(F32)
    logits = [jnp.where(lane < N_EXPERTS,
                        jnp.dot(hi, whi_ref[...], preferred_element_type=F32)
                        + jnp.dot(hi, wlo_ref[...], preferred_element_type=F32)
                        + jnp.dot(lo, whi_ref[...], preferred_element_type=F32), -jnp.inf)
              for hi, lo in zip(xhi, xlo)]
    m1 = [jnp.max(l, axis=-1, keepdims=True) for l in logits]
    i1 = [jnp.min(jnp.where(l == m, lanef, float(LANES)), axis=-1, keepdims=True) for l, m in zip(logits, m1)]
    oh1 = [lanef == i for i in i1]
    rest = [jnp.where(o, -jnp.inf, l) for o, l in zip(oh1, logits)]
    m2 = [jnp.max(l, axis=-1, keepdims=True) for l in rest]
    i2 = [jnp.min(jnp.where(l == m, lanef, float(LANES)), axis=-1, keepdims=True) for l, m in zip(rest, m2)]
    oh2 = [lanef == i for i in i2]
    e = [jnp.exp(b - a) for a, b in zip(m1, m2)]
    w1 = [1.0 / (1.0 + v) for v in e]
    w2 = [v * w for v, w in zip(e, w1)]

    chosen = [jnp.where(a | b, 1.0, 0.0) for a, b in zip(oh1, oh2)]
    r = lax.broadcasted_iota(I32, (half, half), 0)
    c = lax.broadcasted_iota(I32, (half, half), 1)
    before = jnp.where(c < r, 1.0, 0.0).astype(BF16)
    within = [jnp.dot(before, v.astype(BF16), preferred_element_type=F32) for v in chosen]
    counts = [jnp.sum(v, axis=0, keepdims=True) for v in chosen]
    base = [carry_ref[0:1, :], carry_ref[0:1, :] + counts[0]]
    total = base[1] + counts[1]
    carry_ref[...] = jnp.broadcast_to(total, carry_ref.shape)
    cnt_ref[...] = jnp.broadcast_to(total, cnt_ref.shape)
    for n, rs in enumerate(halves):
        excl = within[n] + base[n]
        rank1 = jnp.sum(jnp.where(oh1[n], excl, 0.0), axis=-1, keepdims=True)
        rank2 = jnp.sum(jnp.where(oh2[n], excl, 0.0), axis=-1, keepdims=True)
        info = jnp.where(lane == 0, i1[n], 0.0)
        info = jnp.where(lane == 1, i2[n], info)
        info = jnp.where(lane == 2, rank1, info)
        info = jnp.where(lane == 3, rank2, info)
        info = jnp.where(lane == 4, w1[n], info)
        info = jnp.where(lane == 5, w2[n], info)
        info_ref[rs, :] = info


def _proj_router(h2d, x_bf, w_out, norm_gain, w_hi, w_lo, tm):
    T = h2d.shape[0]
    const = lambda i: (0, 0)
    row = lambda i: (i, 0)
    full = pl.BlockSpec((tm, D_MODEL), row)
    kern = functools.partial(_proj_router_kernel, tm=tm)
    return pl.pallas_call(
        kern,
        grid=(T // tm,),
        in_specs=[full, full, pl.BlockSpec((D_MODEL, D_MODEL), const), pl.BlockSpec((1, D_MODEL), const),
                  pl.BlockSpec((D_MODEL, LANES), const), pl.BlockSpec((D_MODEL, LANES), const)],
        out_specs=[full, full, pl.BlockSpec((tm, LANES), row), pl.BlockSpec((8, LANES), const)],
        out_shape=[
            jax.ShapeDtypeStruct((T, D_MODEL), F32),
            jax.ShapeDtypeStruct((T, D_MODEL), F32),
            jax.ShapeDtypeStruct((T, LANES), F32),
            jax.ShapeDtypeStruct((8, LANES), F32),
        ],
        scratch_shapes=[pltpu.VMEM((8, LANES), F32)],
        compiler_params=_cparams("arbitrary"),
        name="hgrn_outproj_moe_router",
    )(h2d, x_bf, w_out, norm_gain, w_hi, w_lo)


def _dispatch_kernel(zs_ref, pos_ref, prev_pos_ref, xn_ref, xs_hbm, zero_ref, rows_ref, sem, zsem, *, tt, tme):
    i = pl.program_id(0)
    slot = i % 2

    @pl.when(i == 0)
    def _():
        zero_ref[...] = jnp.zeros(zero_ref.shape, F32)
        for e in range(2 * N_EXPERTS):
            row0 = pl.multiple_of(zs_ref[e], tme)
            fill = pltpu.make_async_copy(zero_ref, xs_hbm.at[pl.ds(row0, tme)], zsem)
            fill.start()
            fill.wait()

    def copy(pref, s, t, c):
        return pltpu.make_async_copy(
            rows_ref.at[s, pl.ds(t, 1)], xs_hbm.at[pl.ds(pref[0, 0, 2 * t + c], 1)], sem.at[s])

    def start(t, carry):
        copy(pos_ref, slot, t, 0).start()
        copy(pos_ref, slot, t, 1).start()
        return carry

    def wait_for(pref, s):
        def wait(t, carry):
            copy(pref, s, t, 0).wait()
            copy(pref, s, t, 1).wait()
            return carry
        lax.fori_loop(0, tt, wait, 0, unroll=8)

    rows_ref[slot] = xn_ref[...]
    lax.fori_loop(0, tt, start, 0, unroll=8)

    @pl.when(i > 0)
    def _():
        wait_for(prev_pos_ref, 1 - slot)

    @pl.when(i == pl.num_programs(0) - 1)
    def _():
        wait_for(pos_ref, slot)


def _dispatch(xn, pos, last_tile_start, n_rows, tt, tme):
    T = xn.shape[0]
    pos3 = pos.reshape(T // tt, 1, 2 * tt)
    kern = functools.partial(_dispatch_kernel, tt=tt, tme=tme)
    grid_spec = pltpu.PrefetchScalarGridSpec(
        num_scalar_prefetch=1,
        grid=(T // tt,),
        in_specs=[
            pl.BlockSpec((1, 1, 2 * tt), lambda i, zs: (i, 0, 0), memory_space=pltpu.SMEM),
            pl.BlockSpec((1, 1, 2 * tt), lambda i, zs: (jnp.maximum(i - 1, 0), 0, 0), memory_space=pltpu.SMEM),
            pl.BlockSpec((tt, D_MODEL), lambda i, zs: (i, 0)),
        ],
        out_specs=pl.BlockSpec(memory_space=pl.ANY),
        scratch_shapes=[pltpu.VMEM((tme, D_MODEL), F32), pltpu.VMEM((2, tt, D_MODEL), F32),
                        pltpu.SemaphoreType.DMA((2,)), pltpu.SemaphoreType.DMA(())],
    )
    return pl.pallas_call(
        kern,
        grid_spec=grid_spec,
        out_shape=jax.ShapeDtypeStruct((n_rows, D_MODEL), F32),
        compiler_params=_cparams("arbitrary"),
        name="moe_dispatch",
    )(last_tile_start, pos3, pos3, xn)


def _expert_kernel(te_ref, na_ref, xs_ref, wg_ref, wu_ref, wd_ref, y_ref, *, nchunk):
    del te_ref
    active = pl.program_id(0) < na_ref[0]

    @pl.when(jnp.logical_not(active))
    def _():
        y_ref[...] = jnp.zeros(y_ref.shape, F32)

    @pl.when(active)
    def _():
        xb = xs_ref[...].astype(BF16)
        tf = D_FF_EXPERT // nchunk
        acc = None
        for c in range(nchunk):
            cols = slice(c * tf, (c + 1) * tf)
            g = jnp.dot(xb, wg_ref[0, :, cols], preferred_element_type=F32)
            u = jnp.dot(xb, wu_ref[0, :, cols], preferred_element_type=F32)
            a = (_silu(g) * u).astype(BF16)
            d = jnp.dot(a, wd_ref[0, cols, :], preferred_element_type=F32)
            acc = d if acc is None else acc + d
        y_ref[...] = acc


def _experts(xs, tile_expert, n_active, wg, wu, wd, tme, nchunk):
    n_rows = xs.shape[0]
    n_tiles = n_rows // tme

    def tile(i, na):
        return jnp.minimum(i, na[0] - 1)

    resident = pl.Buffered(1)
    grid_spec = pltpu.PrefetchScalarGridSpec(
        num_scalar_prefetch=2,
        grid=(n_tiles,),
        in_specs=[
            pl.BlockSpec((tme, D_MODEL), lambda i, te, na: (tile(i, na), 0)),
            pl.BlockSpec((1, D_MODEL, D_FF_EXPERT), lambda i, te, na: (te[tile(i, na)], 0, 0),
                         pipeline_mode=resident),
            pl.BlockSpec((1, D_MODEL, D_FF_EXPERT), lambda i, te, na: (te[tile(i, na)], 0, 0)),
            pl.BlockSpec((1, D_FF_EXPERT, D_MODEL), lambda i, te, na: (te[tile(i, na)], 0, 0)),
        ],
        out_specs=pl.BlockSpec((tme, D_MODEL), lambda i, te, na: (i, 0)),
    )
    return pl.pallas_call(
        functools.partial(_expert_kernel, nchunk=nchunk),
        grid_spec=grid_spec,
        out_shape=jax.ShapeDtypeStruct((n_rows, D_MODEL), F32),
        compiler_params=_cparams("arbitrary"),
        name="moe_experts",
    )(tile_expert, n_active, xs, wg, wu, wd)


def _combine_kernel(pos_ref, next_pos_ref, info_ref, h_ref, y_hbm, o_ref, buf_ref, sem, *, tc):
    i = pl.program_id(0)
    slot = i % 2

    def copy(pref, s, t, c):
        return pltpu.make_async_copy(
            y_hbm.at[pl.ds(pref[0, 0, 2 * t + c], 1)], buf_ref.at[s, c, pl.ds(t, 1)], sem.at[s])

    def gather(pref, s):
        def start(t, carry):
            copy(pref, s, t, 0).start()
            copy(pref, s, t, 1).start()
            return carry
        lax.fori_loop(0, tc, start, 0, unroll=8)

    def wait(t, carry):
        copy(pos_ref, slot, t, 0).wait()
        copy(pos_ref, slot, t, 1).wait()
        return carry

    @pl.when(i == 0)
    def _():
        gather(pos_ref, slot)

    @pl.when(i + 1 < pl.num_programs(0))
    def _():
        gather(next_pos_ref, 1 - slot)

    lax.fori_loop(0, tc, wait, 0, unroll=8)
    info = info_ref[...]
    lane = lax.broadcasted_iota(I32, (tc, LANES), 1)
    w1 = jnp.sum(jnp.where(lane == 4, info, 0.0), axis=-1, keepdims=True)
    w2 = jnp.sum(jnp.where(lane == 5, info, 0.0), axis=-1, keepdims=True)
    o_ref[...] = h_ref[...] + (w1 * buf_ref[slot, 0] + w2 * buf_ref[slot, 1])


def _combine(h2d, info, pos, y, tc):
    T = h2d.shape[0]
    n = T // tc
    pos3 = pos.reshape(n, 1, 2 * tc)
    kern = functools.partial(_combine_kernel, tc=tc)
    return pl.pallas_call(
        kern,
        grid=(n,),
        in_specs=[
            pl.BlockSpec((1, 1, 2 * tc), lambda i: (i, 0, 0), memory_space=pltpu.SMEM),
            pl.BlockSpec((1, 1, 2 * tc), lambda i: (jnp.minimum(i + 1, n - 1), 0, 0), memory_space=pltpu.SMEM),
            pl.BlockSpec((tc, LANES), lambda i: (i, 0)),
            pl.BlockSpec((tc, D_MODEL), lambda i: (i, 0)),
            pl.BlockSpec(memory_space=pl.ANY),
        ],
        out_specs=pl.BlockSpec((tc, D_MODEL), lambda i: (i, 0)),
        out_shape=jax.ShapeDtypeStruct((T, D_MODEL), F32),
        scratch_shapes=[pltpu.VMEM((2, 2, tc, D_MODEL), F32), pltpu.SemaphoreType.DMA((2,))],
        compiler_params=_cparams("arbitrary"),
        name="moe_combine",
    )(pos3, pos3, info, h2d, y)


def _alibi_slopes():
    return [2.0 ** (-8.0 * (h + 1.0) / N_ATTN_HEADS) for h in range(N_ATTN_HEADS)]


def _alibi_features(slopes, S):
    def pieces(x):
        hi = x.astype(BF16)
        return hi.astype(np.float32), (x - hi.astype(np.float32)).astype(BF16).astype(np.float32)

    s_hi, s_lo = pieces(np.asarray(slopes, np.float32) * np.float32(LOG2E))
    slope_feat = np.zeros((8, LANES), np.float32)
    slope_feat[:len(slopes), 0], slope_feat[:len(slopes), 1] = s_hi, s_hi
    slope_feat[:len(slopes), 2], slope_feat[:len(slopes), 3] = s_lo, s_lo
    p_hi, p_lo = pieces(np.arange(S, dtype=np.float32))
    pos_feat = np.zeros((S, LANES), np.float32)
    pos_feat[:, 0], pos_feat[:, 1], pos_feat[:, 2], pos_feat[:, 3] = p_hi, p_lo, p_hi, p_lo
    return jnp.asarray(slope_feat), jnp.asarray(pos_feat.astype(BF16))


def _attn_col_gain(dil_q_gain, dil_k_gain, diff_q_gain, diff_k_gain):
    rep = lambda g, n: jnp.tile(g.astype(F32), n)
    ones = jnp.ones((COLB,), F32)
    dq = rep(dil_q_gain, COLB // HEAD_DIM) * (HEAD_DIM ** -0.5 * LOG2E)
    dk = rep(dil_k_gain, COLB // HEAD_DIM)
    fq = rep(diff_q_gain, COLB // DIFF_QK_DIM) * (DIFF_QK_DIM ** -0.5 * LOG2E)
    fk = rep(diff_k_gain, COLB // DIFF_QK_DIM)
    return jnp.stack([dq, dq, dq, dk, dk, dk, ones, ones, ones, fq, fk, ones])


def _lower_bound_rows(lb_logits, layer):
    sm = jax.nn.softmax(lb_logits.astype(F32), axis=0)
    lb = (jnp.cumsum(sm, axis=0) - sm[0])[layer]
    return jnp.zeros((8, D_MODEL), F32).at[0].set(jnp.log(lb)).at[1].set(jnp.log1p(-lb)).at[2].set(1.0 - lb)


def _layer0_and_hgrn_inproj(h2d, B, S, attn_norm, w_in, dq_g, dk_g, fq_g, fk_g, lq1, lk1, lq2, lk2, out_g,
                            w_out, ffn_norm, w_gate, w_up, w_down, hgrn_norm, hgrn_w_in, lb_logits):
    T = B * S
    tm = min(TM_INPROJ, T)
    slopes = _alibi_slopes()
    *cls, dproj = _attn_inproj(h2d, attn_norm.reshape(1, D_MODEL).astype(F32), w_in.astype(BF16),
                               _attn_col_gain(dq_g, dk_g, fq_g, fk_g), tm)
    outs, lses = [], []
    for g, (window, dilation) in enumerate(DIL_GROUPS):
        assert window // dilation == QBLOCK and S % (dilation * QBLOCK) == 0 and tm % dilation == 0
        o, lse = _dilated_group(cls[g], B, S, dilation, tuple(slopes[4 * g:4 * g + 4]))
        outs.extend(o)
        lses.extend(lse)

    lam_init = 0.8 - 0.6 * math.exp(-0.3 * 0)
    lam = (jnp.exp(jnp.sum(lq1.astype(F32) * lk1.astype(F32)))
           - jnp.exp(jnp.sum(lq2.astype(F32) * lk2.astype(F32))) + lam_init).reshape(1)
    slope_feat, pos_feat = _alibi_features(slopes[N_DIL_HEADS:], S)
    out_gain = (jnp.tile(out_g.astype(F32), COLB // HEAD_DIM) * (1.0 - lam_init)).reshape(1, COLB)
    diff = _diff_attention(dproj.reshape(B, S, _QKV), lam, slope_feat, pos_feat, out_gain, min(TQ_DIFF, S)).reshape(T, COLB)

    h2, *gates = _layer0_tail(
        h2d, outs, lses, diff, w_out.astype(BF16), ffn_norm.reshape(1, D_MODEL).astype(F32),
        w_gate.astype(BF16), w_up.astype(BF16), w_down.astype(BF16),
        hgrn_norm.reshape(1, D_MODEL).astype(F32), hgrn_w_in.astype(BF16),
        _lower_bound_rows(lb_logits, 1), min(HGRN_ROWS, S))
    return h2, gates


def _layer1_rest(h2d, gates, B, S, out_gain, w_out, moe_norm, w_router, w_gate, w_up, w_down):
    o = _hgrn_recurrence(gates, out_gain.reshape(1, LANES).astype(F32), B, S, min(HGRN_ROWS, S))
    return _moe_block(h2d, o, w_out, moe_norm, w_router, w_gate, w_up, w_down)


def _moe_block(h2d, mixer_out, w_out, moe_norm, w_router, w_gate, w_up, w_down):
    T = h2d.shape[0]
    tme = TME
    w_pad = jnp.zeros((D_MODEL, LANES), F32).at[:, :N_EXPERTS].set(w_router.astype(F32))
    w_hi, w_lo = _bf16_hi_lo(w_pad)
    h2d, xn, info, counts = _proj_router(h2d, mixer_out, w_out.astype(BF16),
                                         moe_norm.reshape(1, D_MODEL).astype(F32), w_hi, w_lo, min(TM_ROUTER, T))
    cnt = counts[0, :N_EXPERTS].astype(I32)
    padded = ((cnt + tme - 1) // tme) * tme
    ends = jnp.cumsum(padded)
    starts = ends - padded
    experts = info[:, 0:2].astype(I32)
    ranks = info[:, 2:4].astype(I32)
    pos = (starts[experts] + ranks).reshape(-1)
    n_rows = 2 * T + N_EXPERTS * tme
    n_tiles = n_rows // tme
    tile_start = jnp.arange(n_tiles, dtype=I32) * tme
    tile_expert = jnp.minimum(jnp.sum((ends[None, :] <= tile_start[:, None]).astype(I32), axis=1), N_EXPERTS - 1)
    n_active = (ends[-1] // tme).astype(I32).reshape(1)
    tail = jnp.minimum(ends[-1] + jnp.arange(N_EXPERTS, dtype=I32) * tme, n_rows - tme)
    zero_tiles = jnp.concatenate([jnp.maximum(ends - tme, 0), tail]).astype(I32)
    xs = _dispatch(xn, pos, zero_tiles, n_rows, min(TOKENS_PER_DMA_STEP, T), tme)
    y = _experts(xs, tile_expert, n_active, w_gate.astype(BF16), w_up.astype(BF16), w_down.astype(BF16),
                 tme, EXPERT_FF_CHUNKS)
    return _combine(h2d, info, pos, y, min(TOKENS_PER_DMA_STEP, T))


def kernel(x, attn_norm, attn_w_in, dil_q_gain, dil_k_gain, diff_q_gain, diff_k_gain, diff_lambda_q1, diff_lambda_k1, diff_lambda_q2, diff_lambda_k2, diff_out_gain, attn_w_out, ffn_norm, ffn_w_gate, ffn_w_up, ffn_w_down, hgrn_norm, hgrn_w_in, hgrn_lb_logits, hgrn_out_gain, hgrn_w_out, moe_norm, moe_w_router, moe_w_gate, moe_w_up, moe_w_down):
    B, S, D = x.shape
    assert D == D_MODEL
    h = x.astype(F32).reshape(B * S, D)
    h, gates = _layer0_and_hgrn_inproj(
        h, B, S, attn_norm[0], attn_w_in[0], dil_q_gain[0], dil_k_gain[0], diff_q_gain[0], diff_k_gain[0],
        diff_lambda_q1[0], diff_lambda_k1[0], diff_lambda_q2[0], diff_lambda_k2[0], diff_out_gain[0],
        attn_w_out[0], ffn_norm[0], ffn_w_gate[0], ffn_w_up[0], ffn_w_down[0],
        hgrn_norm[0], hgrn_w_in[0], hgrn_lb_logits)
    h = _layer1_rest(h, gates, B, S, hgrn_out_gain[0], hgrn_w_out[0], moe_norm[0], moe_w_router[0],
                     moe_w_gate[0], moe_w_up[0], moe_w_down[0])
    return h.reshape(B, S, D).astype(x.dtype)
```

```python
import functools
import math

import numpy as np
import jax
import jax.numpy as jnp
from jax import lax
from jax.experimental import pallas as pl
from jax.experimental.pallas import tpu as pltpu

F32 = jnp.float32
BF16 = jnp.bfloat16
I32 = jnp.int32

D_MODEL = 1024
HEAD_DIM = 64
N_ATTN_HEADS = 16
N_DIL_HEADS = 12
N_DIFF_HEADS = 4
DIL_GROUPS = ((128, 1), (512, 4), (2048, 16))
DIFF_QK_DIM = 32
ATTN_IN_WIDTH = 3072
QBLOCK = 128
HGRN_HEADS = 8
HGRN_CHUNK = 64
DEC_ROWS = 8
D_FF = 2816
N_EXPERTS = 8
D_FF_EXPERT = 3584
EPS = 1e-6
NEG_INF = -1e30
LOG2E = 1.4426950408889634
LANES = 128
COLB = 256
VMEM_LIMIT = 56 * 1024 * 1024

TM_INPROJ = 512
UB_DILATED = 512
TQ_DIFF = 512
HGRN_ROWS = 256
TM_ROUTER = 512
TOKENS_PER_DMA_STEP = 512
TME = 512
EXPERT_FF_CHUNKS = 2


def _cparams(*sem):
    return pltpu.CompilerParams(dimension_semantics=sem, vmem_limit_bytes=VMEM_LIMIT)


def _bf16_hi_lo(x):
    bits = lax.bitcast_convert_type(x, jnp.uint32)
    hi = lax.bitcast_convert_type(bits & jnp.uint32(0xFFFF0000), F32)
    return hi.astype(BF16), (x - hi).astype(BF16)


def _split_dot(x, m):
    hi, lo = _bf16_hi_lo(x)
    return jnp.dot(hi, m, preferred_element_type=F32) + jnp.dot(lo, m, preferred_element_type=F32)


def _seg_matrix(n, seg):
    i = np.arange(n)
    return jnp.asarray((i[:, None] // seg == i[None, :] // seg).astype(BF16))


def _rms_rows(x, gain_row):
    return x * lax.rsqrt(jnp.mean(x * x, axis=-1, keepdims=True) + EPS) * gain_row


def _sigmoid(x):
    return 1.0 / (1.0 + jnp.exp(-x))


def _silu(x):
    return x * _sigmoid(x)


_ATTN_NORM_SEG = (64, 64, 64, 64, 64, 64, 0, 0, 0, 32, 32, 0)
_QKV = 3 * COLB


def _attn_inproj_kernel(h_ref, ng_ref, w_ref, cg_ref, s64_ref, s32_ref,
                        c0_ref, c1_ref, c2_ref, df_ref, y_ref, *, tm):
    cls_refs = (c0_ref, c1_ref, c2_ref)
    xb = _rms_rows(h_ref[...], ng_ref[...]).astype(BF16)
    blocks = range(len(_ATTN_NORM_SEG))
    ys = [jnp.dot(xb, w_ref[:, c * COLB:(c + 1) * COLB], preferred_element_type=F32) for c in blocks]
    ms = {c: _split_dot(ys[c] * ys[c], s64_ref[...] if seg == 64 else s32_ref[...]) * (1.0 / seg)
          for c, seg in enumerate(_ATTN_NORM_SEG) if seg}
    ys = [ys[c] * lax.rsqrt(ms[c] + EPS) * cg_ref[c:c + 1, :] if c in ms else ys[c] for c in blocks]
    strided = []
    for c in blocks:
        part, g = divmod(c, 3)
        if part == 3:
            df_ref[:, g * COLB:(g + 1) * COLB] = ys[c].astype(BF16)
        elif DIL_GROUPS[g][1] == 1:
            cls_refs[g][:, part * COLB:(part + 1) * COLB] = ys[c].astype(BF16)
        else:
            slot = len(strided)
            y_ref[slot, 0] = ys[c][:, :LANES]
            y_ref[slot, 1] = ys[c][:, LANES:]
            strided.append((slot, part, g))
    for slot, part, g in strided:
        d = DIL_GROUPS[g][1]
        for r in range(d):
            col = r * _QKV + part * COLB
            rows = pl.ds(r, tm // d, stride=d)
            cls_refs[g][:, col:col + COLB] = jnp.concatenate(
                [y_ref[slot, 0, rows, :], y_ref[slot, 1, rows, :]], axis=1).astype(BF16)


def _attn_inproj(h2d, norm_gain, w_bf, col_gain, tm):
    T = h2d.shape[0]
    const = lambda i: (0, 0)
    row = lambda i: (i, 0)
    dils = [d for _, d in DIL_GROUPS]
    kern = functools.partial(_attn_inproj_kernel, tm=tm)
    return pl.pallas_call(
        kern,
        grid=(T // tm,),
        in_specs=[
            pl.BlockSpec((tm, D_MODEL), row),
            pl.BlockSpec((1, D_MODEL), const),
            pl.BlockSpec((D_MODEL, ATTN_IN_WIDTH), const),
            pl.BlockSpec((ATTN_IN_WIDTH // COLB, COLB), const),
            pl.BlockSpec((COLB, COLB), const),
            pl.BlockSpec((COLB, COLB), const),
        ],
        out_specs=[pl.BlockSpec((tm // d, d * _QKV), row) for d in dils] + [pl.BlockSpec((tm, _QKV), row)],
        out_shape=[jax.ShapeDtypeStruct((T // d, d * _QKV), BF16) for d in dils]
                  + [jax.ShapeDtypeStruct((T, _QKV), BF16)],
        scratch_shapes=[pltpu.VMEM((3 * sum(d > 1 for d in dils), 2, tm, LANES), F32)],
        compiler_params=_cparams("parallel"),
        name="attn_inproj",
    )(h2d, norm_gain, w_bf, col_gain, _seg_matrix(COLB, 64), _seg_matrix(COLB, 32))


def _dil_kernel(q_ref, kp_ref, kc_ref, vp_ref, vc_ref, o0_ref, o1_ref, l0_ref, l1_ref,
                *, slopes, dilation, ub):
    u = pl.program_id(1)
    r = pl.program_id(2)
    q = q_ref[...]
    kcat = jnp.concatenate([kp_ref[...], kc_ref[...]], axis=0)
    vcat = jnp.concatenate([vp_ref[...], vc_ref[...]], axis=0)
    qi = lax.broadcasted_iota(I32, (QBLOCK, 2 * QBLOCK), 0)
    kj = lax.broadcasted_iota(I32, (QBLOCK, 2 * QBLOCK), 1)
    step = qi + QBLOCK - kj
    in_window = (step >= 0) & (step <= QBLOCK)
    first_window = in_window & ((kj >= QBLOCK) | (u > 0))
    stepf = step.astype(F32)
    bias = [(-slopes[h] * dilation * LOG2E) * stepf for h in range(COLB // HEAD_DIM)]
    mask_bias = [jnp.where(in_window, b, NEG_INF) for b in bias]
    mask_bias_first = [jnp.where(first_window, b, NEG_INF) for b in bias]
    lane = lax.broadcasted_iota(I32, (QBLOCK, LANES), 1)
    low_half = lane < HEAD_DIM
    jobs = [(i, pair, hh) for i in range(ub // QBLOCK) for pair in range(2) for hh in range(2)]
    nt = (((1,), (1,)), ((), ()))

    def masked_q(i, pair, hh):
        qp = q[i * QBLOCK:(i + 1) * QBLOCK, pair * LANES:(pair + 1) * LANES]
        return jnp.where(low_half if hh == 0 else ~low_half, qp, jnp.zeros_like(qp))

    scores = [lax.dot_general(masked_q(i, pair, hh), kcat[i * QBLOCK:(i + 2) * QBLOCK, pair * LANES:(pair + 1) * LANES],
                              nt, preferred_element_type=F32) for i, pair, hh in jobs]
    stats = []
    for (i, pair, hh), s in zip(jobs, scores):
        s = s + (mask_bias_first if i == 0 else mask_bias)[pair * 2 + hh]
        m = jnp.max(s, axis=-1, keepdims=True)
        e = jnp.exp2(s - m)
        stats.append((m, jnp.sum(e, axis=-1, keepdims=True), e.astype(BF16)))
    pvs = [jnp.dot(e, vcat[i * QBLOCK:(i + 2) * QBLOCK, pair * LANES:(pair + 1) * LANES], preferred_element_type=F32)
           for (i, pair, hh), (_, _, e) in zip(jobs, stats)]
    for n in range(0, len(jobs), 2):
        i, pair, _ = jobs[n]
        outs = [pvs[n + hh] / stats[n + hh][1] for hh in range(2)]
        lses = [jnp.broadcast_to(stats[n + hh][0] + jnp.log2(stats[n + hh][1]), (QBLOCK, LANES)) for hh in range(2)]
        if dilation == 1:
            rows = slice(i * QBLOCK, (i + 1) * QBLOCK)
        else:
            rows = pl.ds(r + i * QBLOCK * dilation, QBLOCK, stride=dilation)
        (o0_ref, o1_ref)[pair][rows, :] = jnp.where(low_half, outs[0], outs[1])
        (l0_ref, l1_ref)[pair][rows, :] = jnp.where(low_half, lses[0], lses[1])


def _dilated_group(cls, B, S, dilation, slopes):
    U = S // dilation
    ub = min(UB_DILATED, U)
    sub = ub // QBLOCK
    view = cls.reshape(B, U, dilation * _QKV)
    prev = lambda u: jnp.maximum(u * sub - 1, 0)
    kern = functools.partial(_dil_kernel, slopes=slopes, dilation=dilation, ub=ub)
    out_spec = pl.BlockSpec((None, ub * dilation, LANES), lambda b, u, r: (b, u, 0))
    res = pl.pallas_call(
        kern,
        grid=(B, U // ub, dilation),
        in_specs=[
            pl.BlockSpec((None, ub, COLB), lambda b, u, r: (b, u, 3 * r)),
            pl.BlockSpec((None, QBLOCK, COLB), lambda b, u, r: (b, prev(u), 3 * r + 1)),
            pl.BlockSpec((None, ub, COLB), lambda b, u, r: (b, u, 3 * r + 1)),
            pl.BlockSpec((None, QBLOCK, COLB), lambda b, u, r: (b, prev(u), 3 * r + 2)),
            pl.BlockSpec((None, ub, COLB), lambda b, u, r: (b, u, 3 * r + 2)),
        ],
        out_specs=[out_spec] * 4,
        out_shape=[jax.ShapeDtypeStruct((B, S, LANES), F32)] * 4,
        compiler_params=_cparams("parallel", "parallel", "arbitrary"),
        name=f"dilated_attn_d{dilation}",
    )(view, view, view, view, view)
    res = [a.reshape(B * S, LANES) for a in res]
    return res[:2], res[2:]


def _diff_kernel(lam_ref, q_ref, k_ref, v_ref, pos_ref, sl_ref, og_ref, s64_ref, o_ref,
                 m_ref, l_ref, acc_ref, qa_ref, *, tq):
    i = pl.program_id(1)
    q = q_ref[...]
    lane = lax.broadcasted_iota(I32, (tq, LANES), 1)
    low_half = lane < HEAD_DIM
    for pair in range(2):
        qp = q[:, pair * LANES:(pair + 1) * LANES]
        for hh in range(2):
            feat = jnp.broadcast_to(sl_ref[pair * 2 + hh:pair * 2 + hh + 1, :], (tq, LANES)).astype(BF16)
            for mu in range(2):
                lo = hh * HEAD_DIM + mu * DIFF_QK_DIM
                sel = (lane >= lo) & (lane < lo + DIFF_QK_DIM)
                r = hh * 2 + mu
                qa_ref[pair, r * tq:(r + 1) * tq, :] = jnp.concatenate(
                    [jnp.where(sel, qp, jnp.zeros_like(qp)), feat], axis=1)

    m_ref[...] = jnp.full(m_ref.shape, NEG_INF, F32)
    l_ref[...] = jnp.zeros(l_ref.shape, F32)
    acc_ref[...] = jnp.zeros(acc_ref.shape, F32)

    def scores(j, nk):
        ks = pl.multiple_of(j * tq, tq)
        kblk = k_ref[pl.ds(ks, nk * tq), :]
        pblk = pos_ref[pl.ds(ks, nk * tq), :]
        s_alls = []
        for pair in range(2):
            kaug = jnp.concatenate([kblk[:, pair * LANES:(pair + 1) * LANES], pblk], axis=1)
            s_alls.append(lax.dot_general(qa_ref[pair], kaug, (((1,), (1,)), ((), ())),
                                          preferred_element_type=F32))
        return s_alls, v_ref[pl.ds(ks, nk * tq), :]

    def finish(s_alls, vblk, masked):
        tk = vblk.shape[0]
        ones = jnp.ones((tk, LANES), BF16)
        if masked:
            row = lax.broadcasted_iota(I32, (tq, tk), 0)
            col = lax.broadcasted_iota(I32, (tq, tk), 1)
            causal = col <= row
        ss = [s_alls[idx // 4][(idx % 4) * tq:(idx % 4 + 1) * tq] for idx in range(8)]
        if masked:
            ss = [jnp.where(causal, s, NEG_INF) for s in ss]
        m_prev = [m_ref[idx] for idx in range(8)]
        m_next = [jnp.maximum(m, jnp.max(s, axis=-1, keepdims=True)) for m, s in zip(m_prev, ss)]
        for idx in range(8):
            m_ref[idx] = m_next[idx]
        alphas = [jnp.exp2(a - b) for a, b in zip(m_prev, m_next)]
        ps = [jnp.exp2(s - jnp.concatenate([m] * (tk // LANES), axis=1)).astype(BF16) for s, m in zip(ss, m_next)]
        pvs = []
        for pair in range(2):
            vaug = jnp.concatenate([vblk[:, pair * LANES:(pair + 1) * LANES], ones], axis=1)
            pvs.append(jnp.dot(jnp.concatenate(ps[pair * 4:pair * 4 + 4], axis=0), vaug,
                               preferred_element_type=F32))
        for idx in range(8):
            part = pvs[idx // 4][(idx % 4) * tq:(idx % 4 + 1) * tq]
            acc_ref[idx] = alphas[idx] * acc_ref[idx] + part[:, :LANES]
            l_ref[idx] = alphas[idx] * l_ref[idx] + part[:, LANES:]

    def two_full_blocks(jj, carry):
        finish(*scores(2 * jj, 2), False)
        return carry

    lax.fori_loop(0, i // 2, two_full_blocks, 0)

    @pl.when(i % 2 == 1)
    def _():
        finish(*scores(i - 1, 1), False)

    finish(*scores(i, 1), True)

    lam = lam_ref[0]
    for pair in range(2):
        o = None
        for mu in range(2):
            lo_idx, hi_idx = pair * 4 + mu, pair * 4 + 2 + mu
            term = jnp.where(low_half, acc_ref[lo_idx] / l_ref[lo_idx], acc_ref[hi_idx] / l_ref[hi_idx])
            o = term if mu == 0 else o - lam * term
        ms = _split_dot(o * o, s64_ref[...]) * (1.0 / HEAD_DIM)
        o = o * lax.rsqrt(ms + EPS) * og_ref[:, pair * LANES:(pair + 1) * LANES]
        o_ref[:, pair * LANES:(pair + 1) * LANES] = o.astype(BF16)


def _diff_attention(proj, lam, slope_feat, pos_feat, out_gain, tq):
    B, S, W = proj.shape
    kern = functools.partial(_diff_kernel, tq=tq)
    return pl.pallas_call(
        kern,
        grid=(B, S // tq),
        in_specs=[
            pl.BlockSpec(memory_space=pltpu.SMEM),
            pl.BlockSpec((None, tq, COLB), lambda b, i: (b, i, 0)),
            pl.BlockSpec((None, S, COLB), lambda b, i: (b, 0, 1), pipeline_mode=pl.Buffered(1)),
            pl.BlockSpec((None, S, COLB), lambda b, i: (b, 0, 2), pipeline_mode=pl.Buffered(1)),
            pl.BlockSpec((S, LANES), lambda b, i: (0, 0), pipeline_mode=pl.Buffered(1)),
            pl.BlockSpec((8, LANES), lambda b, i: (0, 0)),
            pl.BlockSpec((1, COLB), lambda b, i: (0, 0)),
            pl.BlockSpec((LANES, LANES), lambda b, i: (0, 0)),
        ],
        out_specs=pl.BlockSpec((None, tq, COLB), lambda b, i: (b, i, 0)),
        out_shape=jax.ShapeDtypeStruct((B, S, COLB), BF16),
        scratch_shapes=[
            pltpu.VMEM((8, tq, LANES), F32),
            pltpu.VMEM((8, tq, LANES), F32),
            pltpu.VMEM((8, tq, LANES), F32),
            pltpu.VMEM((2, 4 * tq, 2 * LANES), BF16),
        ],
        compiler_params=_cparams("parallel", "parallel"),
        name="diff_attn",
    )(lam, proj, proj, proj, pos_feat, slope_feat, out_gain, _seg_matrix(LANES, HEAD_DIM))


def _merge_heads(o_refs, l_refs, d_ref):
    n = len(o_refs)
    pieces = [None] * n
    for pair in range(2):
        ls = [l_refs[2 * g + pair][...] for g in range(n // 2)]
        mx = functools.reduce(jnp.maximum, ls)
        es = [jnp.exp2(l - mx) for l in ls]
        inv = 1.0 / functools.reduce(jnp.add, es)
        for g, e in enumerate(es):
            pieces[2 * g + pair] = (e * inv * o_refs[2 * g + pair][...]).astype(BF16)
    return jnp.concatenate(pieces + [d_ref[...]], axis=1)


def _split_dot_lhs(m, x):
    hi, lo = _bf16_hi_lo(x)
    return jnp.dot(m, hi, preferred_element_type=F32) + jnp.dot(m, lo, preferred_element_type=F32)


def _chunk_mask(rows):
    ti = lax.broadcasted_iota(I32, (rows, rows), 0)
    si = lax.broadcasted_iota(I32, (rows, rows), 1)
    return (si <= ti) & (si >= (ti // HGRN_CHUNK) * HGRN_CHUNK)


def _hgrn_gates(xb, w_ref, lb_ref, qs_ref, ks_ref, qi_ref, kst_ref, v_ref, g_ref, dec_ref):
    C = HGRN_CHUNK
    tm = xb.shape[0]
    nc = tm // C
    yq, fl, yv, yg = [jnp.dot(xb, w_ref[:, c * D_MODEL:(c + 1) * D_MODEL], preferred_element_type=F32)
                      for c in range(4)]
    v_ref[...] = yv.astype(BF16)
    g_ref[...] = _sigmoid(yg).astype(BF16)
    q = _silu(yq)
    log_lb, log1m_lb, one_m_lb = lb_ref[0:1, :], lb_ref[1:2, :], lb_ref[2:3, :]
    t = jnp.exp(-jnp.abs(fl))
    r = 1.0 / (1.0 + t)
    c = log1m_lb + (jnp.minimum(fl, 0.0) + jnp.log(r))
    lf = jnp.maximum(log_lb, c) + jnp.log(1.0 + jnp.exp(-jnp.abs(log_lb - c)))
    k = one_m_lb * jnp.where(fl >= 0.0, t * r, r)
    tri = jnp.where(_chunk_mask(tm), 1.0, 0.0).astype(BF16)
    b = _split_dot_lhs(tri, lf)
    chunk_row = lambda i: jnp.concatenate(
        [jnp.broadcast_to(b[n * C + i:n * C + i + 1, :], (C, D_MODEL)) for n in range(nc)], axis=0)
    b_mid = chunk_row(C // 2)
    b_last = chunk_row(C - 1)
    qs_ref[...] = (q * jnp.exp(b - b_mid)).astype(BF16)
    ks_ref[...] = (k * jnp.exp(b_mid - b)).astype(BF16)
    qi_ref[...] = (q * jnp.exp(b)).astype(BF16)
    kst_ref[...] = (k * jnp.exp(b_last - b)).astype(BF16)
    last_rows = jnp.concatenate([b[n * C + C - 1:n * C + C, :] for n in range(nc)], axis=0)
    dec_ref[...] = jnp.concatenate(
        [jnp.exp(last_rows), jnp.zeros((dec_ref.shape[0] - nc, D_MODEL), F32)], axis=0)


def _layer0_tail_kernel(h_ref, *refs):
    n = 2 * len(DIL_GROUPS)
    o_refs, l_refs = refs[:n], refs[n:2 * n]
    (d_ref, wo_ref, fng_ref, wg_ref, wu_ref, wd_ref, hng_ref, wh_ref, lb_ref,
     h2_ref, qs_ref, ks_ref, qi_ref, kst_ref, v_ref, g_ref, dec_ref) = refs[2 * n:]
    mixed = _merge_heads(o_refs, l_refs, d_ref)
    h1 = h_ref[...] + jnp.dot(mixed, wo_ref[...], preferred_element_type=F32)
    xb = _rms_rows(h1, fng_ref[...]).astype(BF16)
    g = jnp.dot(xb, wg_ref[...], preferred_element_type=F32)
    u = jnp.dot(xb, wu_ref[...], preferred_element_type=F32)
    a = (_silu(g) * u).astype(BF16)
    h2 = h1 + jnp.dot(a, wd_ref[...], preferred_element_type=F32)
    h2_ref[...] = h2
    _hgrn_gates(_rms_rows(h2, hng_ref[...]).astype(BF16), wh_ref, lb_ref,
                qs_ref, ks_ref, qi_ref, kst_ref, v_ref, g_ref, dec_ref)


def _layer0_tail(h2d, outs, lses, diff, w_out, ffn_gain, wg, wu, wd, hgrn_gain, w_in, lb_rows, tm):
    T = h2d.shape[0]
    row = lambda i: (i, 0)
    full = pl.BlockSpec((tm, D_MODEL), row)
    half = pl.BlockSpec((tm, LANES), row)
    held = lambda shape: pl.BlockSpec(shape, lambda i: (0, 0), pipeline_mode=pl.Buffered(1))
    bf = jax.ShapeDtypeStruct((T, D_MODEL), BF16)
    f32 = jax.ShapeDtypeStruct((T, D_MODEL), F32)
    return pl.pallas_call(
        _layer0_tail_kernel,
        grid=(T // tm,),
        in_specs=[full] + [half] * (len(outs) + len(lses))
                 + [pl.BlockSpec((tm, COLB), row), held((D_MODEL, D_MODEL)), held((1, D_MODEL)),
                    held((D_MODEL, D_FF)), held((D_MODEL, D_FF)), held((D_FF, D_MODEL)),
                    held((1, D_MODEL)), held((D_MODEL, 4 * D_MODEL)), held((8, D_MODEL))],
        out_specs=[full] * 7 + [pl.BlockSpec((DEC_ROWS, D_MODEL), row)],
        out_shape=[f32] + [bf] * 6 + [jax.ShapeDtypeStruct((T // tm * DEC_ROWS, D_MODEL), F32)],
        compiler_params=_cparams("parallel"),
        name="layer0_tail_hgrn_inproj",
    )(h2d, *outs, *lses, diff, w_out, ffn_gain, wg, wu, wd, hgrn_gain, w_in, lb_rows)


def _hgrn_kernel(qs_ref, ks_ref, qi_ref, kst_ref, v_ref, g_ref, dec_ref, og_ref, o_ref, st_ref, *, rows):
    @pl.when(pl.program_id(1) == 0)
    def _():
        st_ref[...] = jnp.zeros(st_ref.shape, F32)

    C = HGRN_CHUNK
    nc = rows // C
    heads = range(HGRN_HEADS)
    lanes = [slice(hd * LANES, (hd + 1) * LANES) for hd in heads]
    causal = _chunk_mask(rows)
    nt = (((1,), (1,)), ((), ()))
    tn = (((0,), (0,)), ((), ()))
    scores = [lax.dot_general(qs_ref[:, lanes[hd]], ks_ref[:, lanes[hd]], nt, preferred_element_type=F32)
              for hd in heads]
    incs = [[lax.dot_general(v_ref[c * C:(c + 1) * C, lanes[hd]], kst_ref[c * C:(c + 1) * C, lanes[hd]], tn,
                             preferred_element_type=F32) for c in range(nc)] for hd in heads]
    probs = [jnp.where(causal, s, 0.0).astype(BF16) for s in scores]
    outs = [jnp.dot(probs[hd], v_ref[:, lanes[hd]], preferred_element_type=F32) for hd in heads]
    states = []
    for hd in heads:
        st, per_chunk = st_ref[hd], []
        for c in range(nc):
            per_chunk.append(st.astype(BF16))
            st = dec_ref[c:c + 1, lanes[hd]] * st + incs[hd][c]
        st_ref[hd] = st
        states.append(per_chunk)
    for hd in heads:
        inter = [lax.dot_general(qi_ref[c * C:(c + 1) * C, lanes[hd]], states[hd][c], nt,
                                 preferred_element_type=F32) for c in range(nc)]
        o = outs[hd] + jnp.concatenate(inter, axis=0)
        o = o * lax.rsqrt(jnp.mean(o * o, axis=-1, keepdims=True) + EPS) * og_ref[...]
        o_ref[:, lanes[hd]] = (o * g_ref[:, lanes[hd]].astype(F32)).astype(BF16)


def _hgrn_recurrence(gates, out_gain, B, S, rows):
    T = B * S
    nseq = S // rows
    row = pl.BlockSpec((rows, D_MODEL), lambda b, s: (b * nseq + s, 0))
    dec = pl.BlockSpec((DEC_ROWS, D_MODEL), lambda b, s: (b * nseq + s, 0))
    kern = functools.partial(_hgrn_kernel, rows=rows)
    return pl.pallas_call(
        kern,
        grid=(B, nseq),
        in_specs=[row] * 6 + [dec, pl.BlockSpec((1, LANES), lambda b, s: (0, 0))],
        out_specs=row,
        out_shape=jax.ShapeDtypeStruct((T, D_MODEL), BF16),
        scratch_shapes=[pltpu.VMEM((HGRN_HEADS, LANES, LANES), F32)],
        compiler_params=_cparams("parallel", "arbitrary"),
        name="hgrn_recurrence",
    )(*gates, out_gain)


def _proj_router_kernel(h_ref, x_ref, wo_ref, ng_ref, whi_ref, wlo_ref,
                        h2_ref, xn_ref, info_ref, cnt_ref, carry_ref, *, tm):
    @pl.when(pl.program_id(0) == 0)
    def _():
        carry_ref[...] = jnp.zeros(carry_ref.shape, F32)

    half = tm // 2
    halves = [slice(n * half, (n + 1) * half) for n in range(2)]
    h2 = [h_ref[rs, :] + jnp.dot(x_ref[rs, :], wo_ref[...], preferred_element_type=F32) for rs in halves]
    xn = [_rms_rows(v, ng_ref[...]) for v in h2]
    for rs, a, c in zip(halves, h2, xn):
        h2_ref[rs, :] = a
        xn_ref[rs, :] = c
    xhi, xlo = zip(*[_bf16_hi_lo(v) for v in xn])
    lane = lax.broadcasted_iota(I32, (half, LANES), 1)
    lanef = lane.astype(F32)
    logits = [jnp.where(lane < N_EXPERTS,
                        jnp.dot(hi, whi_ref[...], preferred_element_type=F32)
                        + jnp.dot(hi, wlo_ref[...], preferred_element_type=F32)
                        + jnp.dot(lo, whi_ref[...], preferred_element_type=F32), -jnp.inf)
              for hi, lo in zip(xhi, xlo)]
    m1 = [jnp.max(l, axis=-1, keepdims=True) for l in logits]
    i1 = [jnp.min(jnp.where(l == m, lanef, float(LANES)), axis=-1, keepdims=True) for l, m in zip(logits, m1)]
    oh1 = [lanef == i for i in i1]
    rest = [jnp.where(o, -jnp.inf, l) for o, l in zip(oh1, logits)]
    m2 = [jnp.max(l, axis=-1, keepdims=True) for l in rest]
    i2 = [jnp.min(jnp.where(l == m, lanef, float(LANES)), axis=-1, keepdims=True) for l, m in zip(rest, m2)]
    oh2 = [lanef == i for i in i2]
    e = [jnp.exp(b - a) for a, b in zip(m1, m2)]
    w1 = [1.0 / (1.0 + v) for v in e]
    w2 = [v * w for v, w in zip(e, w1)]

    chosen = [jnp.where(a | b, 1.0, 0.0) for a, b in zip(oh1, oh2)]
    r = lax.broadcasted_iota(I32, (half, half), 0)
    c = lax.broadcasted_iota(I32, (half, half), 1)
    before = jnp.where(c < r, 1.0, 0.0).astype(BF16)
    within = [jnp.dot(before, v.astype(BF16), preferred_element_type=F32) for v in chosen]
    counts = [jnp.sum(v, axis=0, keepdims=True) for v in chosen]
    base = [carry_ref[0:1, :], carry_ref[0:1, :] + counts[0]]
    total = base[1] + counts[1]
    carry_ref[...] = jnp.broadcast_to(total, carry_ref.shape)
    cnt_ref[...] = jnp.broadcast_to(total, cnt_ref.shape)
    for n, rs in enumerate(halves):
        excl = within[n] + base[n]
        rank1 = jnp.sum(jnp.where(oh1[n], excl, 0.0), axis=-1, keepdims=True)
        rank2 = jnp.sum(jnp.where(oh2[n], excl, 0.0), axis=-1, keepdims=True)
        info = jnp.where(lane == 0, i1[n], 0.0)
        info = jnp.where(lane == 1, i2[n], info)
        info = jnp.where(lane == 2, rank1, info)
        info = jnp.where(lane == 3, rank2, info)
        info = jnp.where(lane == 4, w1[n], info)
        info = jnp.where(lane == 5, w2[n], info)
        info_ref[rs, :] = info


def _proj_router(h2d, x_bf, w_out, norm_gain, w_hi, w_lo, tm):
    T = h2d.shape[0]
    const = lambda i: (0, 0)
    row = lambda i: (i, 0)
    full = pl.BlockSpec((tm, D_MODEL), row)
    kern = functools.partial(_proj_router_kernel, tm=tm)
    return pl.pallas_call(
        kern,
        grid=(T // tm,),
        in_specs=[full, full, pl.BlockSpec((D_MODEL, D_MODEL), const), pl.BlockSpec((1, D_MODEL), const),
                  pl.BlockSpec((D_MODEL, LANES), const), pl.BlockSpec((D_MODEL, LANES), const)],
        out_specs=[full, full, pl.BlockSpec((tm, LANES), row), pl.BlockSpec((8, LANES), const)],
        out_shape=[
            jax.ShapeDtypeStruct((T, D_MODEL), F32),
            jax.ShapeDtypeStruct((T, D_MODEL), F32),
            jax.ShapeDtypeStruct((T, LANES), F32),
            jax.ShapeDtypeStruct((8, LANES), F32),
        ],
        scratch_shapes=[pltpu.VMEM((8, LANES), F32)],
        compiler_params=_cparams("arbitrary"),
        name="hgrn_outproj_moe_router",
    )(h2d, x_bf, w_out, norm_gain, w_hi, w_lo)


def _dispatch_kernel(zs_ref, pos_ref, prev_pos_ref, xn_ref, xs_hbm, zero_ref, rows_ref, sem, zsem, *, tt, tme):
    i = pl.program_id(0)
    slot = i % 2

    @pl.when(i == 0)
    def _():
        zero_ref[...] = jnp.zeros(zero_ref.shape, F32)
        for e in range(2 * N_EXPERTS):
            row0 = pl.multiple_of(zs_ref[e], tme)
            fill = pltpu.make_async_copy(zero_ref, xs_hbm.at[pl.ds(row0, tme)], zsem)
            fill.start()
            fill.wait()

    def copy(pref, s, t, c):
        return pltpu.make_async_copy(
            rows_ref.at[s, pl.ds(t, 1)], xs_hbm.at[pl.ds(pref[0, 0, 2 * t + c], 1)], sem.at[s])

    def start(t, carry):
        copy(pos_ref, slot, t, 0).start()
        copy(pos_ref, slot, t, 1).start()
        return carry

    def wait_for(pref, s):
        def wait(t, carry):
            copy(pref, s, t, 0).wait()
            copy(pref, s, t, 1).wait()
            return carry
        lax.fori_loop(0, tt, wait, 0, unroll=8)

    rows_ref[slot] = xn_ref[...]
    lax.fori_loop(0, tt, start, 0, unroll=8)

    @pl.when(i > 0)
    def _():
        wait_for(prev_pos_ref, 1 - slot)

    @pl.when(i == pl.num_programs(0) - 1)
    def _():
        wait_for(pos_ref, slot)


def _dispatch(xn, pos, last_tile_start, n_rows, tt, tme):
    T = xn.shape[0]
    pos3 = pos.reshape(T // tt, 1, 2 * tt)
    kern = functools.partial(_dispatch_kernel, tt=tt, tme=tme)
    grid_spec = pltpu.PrefetchScalarGridSpec(
        num_scalar_prefetch=1,
        grid=(T // tt,),
        in_specs=[
            pl.BlockSpec((1, 1, 2 * tt), lambda i, zs: (i, 0, 0), memory_space=pltpu.SMEM),
            pl.BlockSpec((1, 1, 2 * tt), lambda i, zs: (jnp.maximum(i - 1, 0), 0, 0), memory_space=pltpu.SMEM),
            pl.BlockSpec((tt, D_MODEL), lambda i, zs: (i, 0)),
        ],
        out_specs=pl.BlockSpec(memory_space=pl.ANY),
        scratch_shapes=[pltpu.VMEM((tme, D_MODEL), F32), pltpu.VMEM((2, tt, D_MODEL), F32),
                        pltpu.SemaphoreType.DMA((2,)), pltpu.SemaphoreType.DMA(())],
    )
    return pl.pallas_call(
        kern,
        grid_spec=grid_spec,
        out_shape=jax.ShapeDtypeStruct((n_rows, D_MODEL), F32),
        compiler_params=_cparams("arbitrary"),
        name="moe_dispatch",
    )(last_tile_start, pos3, pos3, xn)


def _expert_kernel(te_ref, na_ref, xs_ref, wg_ref, wu_ref, wd_ref, y_ref, *, nchunk):
    del te_ref
    active = pl.program_id(0) < na_ref[0]

    @pl.when(jnp.logical_not(active))
    def _():
        y_ref[...] = jnp.zeros(y_ref.shape, F32)

    @pl.when(active)
    def _():
        xb = xs_ref[...].astype(BF16)
        tf = D_FF_EXPERT // nchunk
        acc = None
        for c in range(nchunk):
            cols = slice(c * tf, (c + 1) * tf)
            g = jnp.dot(xb, wg_ref[0, :, cols], preferred_element_type=F32)
            u = jnp.dot(xb, wu_ref[0, :, cols], preferred_element_type=F32)
            a = (_silu(g) * u).astype(BF16)
            d = jnp.dot(a, wd_ref[0, cols, :], preferred_element_type=F32)
            acc = d if acc is None else acc + d
        y_ref[...] = acc


def _experts(xs, tile_expert, n_active, wg, wu, wd, tme, nchunk):
    n_rows = xs.shape[0]
    n_tiles = n_rows // tme

    def tile(i, na):
        return jnp.minimum(i, na[0] - 1)

    resident = pl.Buffered(1)
    grid_spec = pltpu.PrefetchScalarGridSpec(
        num_scalar_prefetch=2,
        grid=(n_tiles,),
        in_specs=[
            pl.BlockSpec((tme, D_MODEL), lambda i, te, na: (tile(i, na), 0)),
            pl.BlockSpec((1, D_MODEL, D_FF_EXPERT), lambda i, te, na: (te[tile(i, na)], 0, 0),
                         pipeline_mode=resident),
            pl.BlockSpec((1, D_MODEL, D_FF_EXPERT), lambda i, te, na: (te[tile(i, na)], 0, 0)),
            pl.BlockSpec((1, D_FF_EXPERT, D_MODEL), lambda i, te, na: (te[tile(i, na)], 0, 0)),
        ],
        out_specs=pl.BlockSpec((tme, D_MODEL), lambda i, te, na: (i, 0)),
    )
    return pl.pallas_call(
        functools.partial(_expert_kernel, nchunk=nchunk),
        grid_spec=grid_spec,
        out_shape=jax.ShapeDtypeStruct((n_rows, D_MODEL), F32),
        compiler_params=_cparams("arbitrary"),
        name="moe_experts",
    )(tile_expert, n_active, xs, wg, wu, wd)


def _combine_kernel(pos_ref, next_pos_ref, info_ref, h_ref, y_hbm, o_ref, buf_ref, sem, *, tc):
    i = pl.program_id(0)
    slot = i % 2

    def copy(pref, s, t, c):
        return pltpu.make_async_copy(
            y_hbm.at[pl.ds(pref[0, 0, 2 * t + c], 1)], buf_ref.at[s, c, pl.ds(t, 1)], sem.at[s])

    def gather(pref, s):
        def start(t, carry):
            copy(pref, s, t, 0).start()
            copy(pref, s, t, 1).start()
            return carry
        lax.fori_loop(0, tc, start, 0, unroll=8)

    def wait(t, carry):
        copy(pos_ref, slot, t, 0).wait()
        copy(pos_ref, slot, t, 1).wait()
        return carry

    @pl.when(i == 0)
    def _():
        gather(pos_ref, slot)

    @pl.when(i + 1 < pl.num_programs(0))
    def _():
        gather(next_pos_ref, 1 - slot)

    lax.fori_loop(0, tc, wait, 0, unroll=8)
    info = info_ref[...]
    lane = lax.broadcasted_iota(I32, (tc, LANES), 1)
    w1 = jnp.sum(jnp.where(lane == 4, info, 0.0), axis=-1, keepdims=True)
    w2 = jnp.sum(jnp.where(lane == 5, info, 0.0), axis=-1, keepdims=True)
    o_ref[...] = h_ref[...] + (w1 * buf_ref[slot, 0] + w2 * buf_ref[slot, 1])


def _combine(h2d, info, pos, y, tc):
    T = h2d.shape[0]
    n = T // tc
    pos3 = pos.reshape(n, 1, 2 * tc)
    kern = functools.partial(_combine_kernel, tc=tc)
    return pl.pallas_call(
        kern,
        grid=(n,),
        in_specs=[
            pl.BlockSpec((1, 1, 2 * tc), lambda i: (i, 0, 0), memory_space=pltpu.SMEM),
            pl.BlockSpec((1, 1, 2 * tc), lambda i: (jnp.minimum(i + 1, n - 1), 0, 0), memory_space=pltpu.SMEM),
            pl.BlockSpec((tc, LANES), lambda i: (i, 0)),
            pl.BlockSpec((tc, D_MODEL), lambda i: (i, 0)),
            pl.BlockSpec(memory_space=pl.ANY),
        ],
        out_specs=pl.BlockSpec((tc, D_MODEL), lambda i: (i, 0)),
        out_shape=jax.ShapeDtypeStruct((T, D_MODEL), F32),
        scratch_shapes=[pltpu.VMEM((2, 2, tc, D_MODEL), F32), pltpu.SemaphoreType.DMA((2,))],
        compiler_params=_cparams("arbitrary"),
        name="moe_combine",
    )(pos3, pos3, info, h2d, y)


def _alibi_slopes():
    return [2.0 ** (-8.0 * (h + 1.0) / N_ATTN_HEADS) for h in range(N_ATTN_HEADS)]


def _alibi_features(slopes, S):
    def pieces(x):
        hi = x.astype(BF16)
        return hi.astype(np.float32), (x - hi.astype(np.float32)).astype(BF16).astype(np.float32)

    s_hi, s_lo = pieces(np.asarray(slopes, np.float32) * np.float32(LOG2E))
    slope_feat = np.zeros((8, LANES), np.float32)
    slope_feat[:len(slopes), 0], slope_feat[:len(slopes), 1] = s_hi, s_hi
    slope_feat[:len(slopes), 2], slope_feat[:len(slopes), 3] = s_lo, s_lo
    p_hi, p_lo = pieces(np.arange(S, dtype=np.float32))
    pos_feat = np.zeros((S, LANES), np.float32)
    pos_feat[:, 0], pos_feat[:, 1], pos_feat[:, 2], pos_feat[:, 3] = p_hi, p_lo, p_hi, p_lo
    return jnp.asarray(slope_feat), jnp.asarray(pos_feat.astype(BF16))


def _attn_col_gain(dil_q_gain, dil_k_gain, diff_q_gain, diff_k_gain):
    rep = lambda g, n: jnp.tile(g.astype(F32), n)
    ones = jnp.ones((COLB,), F32)
    dq = rep(dil_q_gain, COLB // HEAD_DIM) * (HEAD_DIM ** -0.5 * LOG2E)
    dk = rep(dil_k_gain, COLB // HEAD_DIM)
    fq = rep(diff_q_gain, COLB // DIFF_QK_DIM) * (DIFF_QK_DIM ** -0.5 * LOG2E)
    fk = rep(diff_k_gain, COLB // DIFF_QK_DIM)
    return jnp.stack([dq, dq, dq, dk, dk, dk, ones, ones, ones, fq, fk, ones])


def _lower_bound_rows(lb_logits, layer):
    sm = jax.nn.softmax(lb_logits.astype(F32), axis=0)
    lb = (jnp.cumsum(sm, axis=0) - sm[0])[layer]
    return jnp.zeros((8, D_MODEL), F32).at[0].set(jnp.log(lb)).at[1].set(jnp.log1p(-lb)).at[2].set(1.0 - lb)


def _layer0_and_hgrn_inproj(h2d, B, S, attn_norm, w_in, dq_g, dk_g, fq_g, fk_g, lq1, lk1, lq2, lk2, out_g,
                            w_out, ffn_norm, w_gate, w_up, w_down, hgrn_norm, hgrn_w_in, lb_logits):
    T = B * S
    tm = min(TM_INPROJ, T)
    slopes = _alibi_slopes()
    *cls, dproj = _attn_inproj(h2d, attn_norm.reshape(1, D_MODEL).astype(F32), w_in.astype(BF16),
                               _attn_col_gain(dq_g, dk_g, fq_g, fk_g), tm)
    outs, lses = [], []
    for g, (window, dilation) in enumerate(DIL_GROUPS):
        assert window // dilation == QBLOCK and S % (dilation * QBLOCK) == 0 and tm % dilation == 0
        o, lse = _dilated_group(cls[g], B, S, dilation, tuple(slopes[4 * g:4 * g + 4]))
        outs.extend(o)
        lses.extend(lse)

    lam_init = 0.8 - 0.6 * math.exp(-0.3 * 0)
    lam = (jnp.exp(jnp.sum(lq1.astype(F32) * lk1.astype(F32)))
           - jnp.exp(jnp.sum(lq2.astype(F32) * lk2.astype(F32))) + lam_init).reshape(1)
    slope_feat, pos_feat = _alibi_features(slopes[N_DIL_HEADS:], S)
    out_gain = (jnp.tile(out_g.astype(F32), COLB // HEAD_DIM) * (1.0 - lam_init)).reshape(1, COLB)
    diff = _diff_attention(dproj.reshape(B, S, _QKV), lam, slope_feat, pos_feat, out_gain, min(TQ_DIFF, S)).reshape(T, COLB)

    h2, *gates = _layer0_tail(
        h2d, outs, lses, diff, w_out.astype(BF16), ffn_norm.reshape(1, D_MODEL).astype(F32),
        w_gate.astype(BF16), w_up.astype(BF16), w_down.astype(BF16),
        hgrn_norm.reshape(1, D_MODEL).astype(F32), hgrn_w_in.astype(BF16),
        _lower_bound_rows(lb_logits, 1), min(HGRN_ROWS, S))
    return h2, gates


def _layer1_rest(h2d, gates, B, S, out_gain, w_out, moe_norm, w_router, w_gate, w_up, w_down):
    o = _hgrn_recurrence(gates, out_gain.reshape(1, LANES).astype(F32), B, S, min(HGRN_ROWS, S))
    return _moe_block(h2d, o, w_out, moe_norm, w_router, w_gate, w_up, w_down)


def _moe_block(h2d, mixer_out, w_out, moe_norm, w_router, w_gate, w_up, w_down):
    T = h2d.shape[0]
    tme = TME
    w_pad = jnp.zeros((D_MODEL, LANES), F32).at[:, :N_EXPERTS].set(w_router.astype(F32))
    w_hi, w_lo = _bf16_hi_lo(w_pad)
    h2d, xn, info, counts = _proj_router(h2d, mixer_out, w_out.astype(BF16),
                                         moe_norm.reshape(1, D_MODEL).astype(F32), w_hi, w_lo, min(TM_ROUTER, T))
    cnt = counts[0, :N_EXPERTS].astype(I32)
    padded = ((cnt + tme - 1) // tme) * tme
    ends = jnp.cumsum(padded)
    starts = ends - padded
    experts = info[:, 0:2].astype(I32)
    ranks = info[:, 2:4].astype(I32)
    pos = (starts[experts] + ranks).reshape(-1)
    n_rows = 2 * T + N_EXPERTS * tme
    n_tiles = n_rows // tme
    tile_start = jnp.arange(n_tiles, dtype=I32) * tme
    tile_expert = jnp.minimum(jnp.sum((ends[None, :] <= tile_start[:, None]).astype(I32), axis=1), N_EXPERTS - 1)
    n_active = (ends[-1] // tme).astype(I32).reshape(1)
    tail = jnp.minimum(ends[-1] + jnp.arange(N_EXPERTS, dtype=I32) * tme, n_rows - tme)
    zero_tiles = jnp.concatenate([jnp.maximum(ends - tme, 0), tail]).astype(I32)
    xs = _dispatch(xn, pos, zero_tiles, n_rows, min(TOKENS_PER_DMA_STEP, T), tme)
    y = _experts(xs, tile_expert, n_active, w_gate.astype(BF16), w_up.astype(BF16), w_down.astype(BF16),
                 tme, EXPERT_FF_CHUNKS)
    return _combine(h2d, info, pos, y, min(TOKENS_PER_DMA_STEP, T))


def kernel(x, attn_norm, attn_w_in, dil_q_gain, dil_k_gain, diff_q_gain, diff_k_gain, diff_lambda_q1, diff_lambda_k1, diff_lambda_q2, diff_lambda_k2, diff_out_gain, attn_w_out, ffn_norm, ffn_w_gate, ffn_w_up, ffn_w_down, hgrn_norm, hgrn_w_in, hgrn_lb_logits, hgrn_out_gain, hgrn_w_out, moe_norm, moe_w_router, moe_w_gate, moe_w_up, moe_w_down):
    B, S, D = x.shape
    assert D == D_MODEL
    h = x.astype(F32).reshape(B * S, D)
    h, gates = _layer0_and_hgrn_inproj(
        h, B, S, attn_norm[0], attn_w_in[0], dil_q_gain[0], dil_k_gain[0], diff_q_gain[0], diff_k_gain[0],
        diff_lambda_q1[0], diff_lambda_k1[0], diff_lambda_q2[0], diff_lambda_k2[0], diff_out_gain[0],
        attn_w_out[0], ffn_norm[0], ffn_w_gate[0], ffn_w_up[0], ffn_w_down[0],
        hgrn_norm[0], hgrn_w_in[0], hgrn_lb_logits)
    h = _layer1_rest(h, gates, B, S, hgrn_out_gain[0], hgrn_w_out[0], moe_norm[0], moe_w_router[0],
                     moe_w_gate[0], moe_w_up[0], moe_w_down[0])
    return h.reshape(B, S, D).astype(x.dtype)
```
